```python
import jax, jax.numpy as jnp
from jax import lax
import numpy as np

D_MODEL = 1024
BATCH = 8
SEQ = 2048
DEPTH = 4
DEC_BATCH = 128
DEC_SEQ = 1
PAST_LEN = 16384
PAGE_SIZE = 128

N_MIXERS = 2
EXPAND = 2
D_INNER = EXPAND * D_MODEL
A_EXPAND_RATIO = 128
A_HEADS = D_MODEL // A_EXPAND_RATIO
A_DK = A_EXPAND_RATIO
A_DV = D_INNER // A_HEADS
A_CHUNK = 64
B_CHUNK = 128
B_GROUPS = 8
B_DG = D_INNER // B_GROUPS
PLE_DIM = 256
N_A = (DEPTH + 1) // 2
N_B = DEPTH // 2
EPS = 1e-6

kernel_name = "hgrn2_chunkmlp_hybrid_step"


def rmsnorm(x, g):
    xf = x.astype(jnp.float32)
    y = xf * lax.rsqrt(jnp.mean(xf * xf, axis=-1, keepdims=True) + EPS)
    return (y * g.astype(jnp.float32)).astype(x.dtype)


def layernorm(x, g, b):
    xf = x.astype(jnp.float32)
    mu = jnp.mean(xf, axis=-1, keepdims=True)
    xc = xf - mu
    y = xc * lax.rsqrt(jnp.mean(xc * xc, axis=-1, keepdims=True) + EPS)
    return (y * g.astype(jnp.float32) + b.astype(jnp.float32)).astype(x.dtype)


def hgrn2_chunkwise(q, k, v, logf, s0):
    B, L = q.shape[0], q.shape[1]
    c = min(A_CHUNK, L)
    n = -(-L // c)
    pad = n * c - L

    def prep(a):
        a = jnp.pad(a.astype(jnp.float32), ((0, 0), (0, pad), (0, 0), (0, 0)))
        return a.reshape(B, n, c, a.shape[2], a.shape[3]).swapaxes(0, 1)

    qc, kc, vc, gc = prep(q), prep(k), prep(v), prep(logf)
    causal = jnp.tril(jnp.ones((c, c), dtype=bool))[None, :, :, None, None]

    def step(S, inp):
        qb, kb, vb, gb = inp
        b = jnp.cumsum(gb, axis=1)
        o_inter = jnp.einsum('bthk,bhkv->bthv', qb * jnp.exp(b), S)
        diff = jnp.where(causal, b[:, :, None] - b[:, None, :], -jnp.inf)
        att = jnp.einsum('bthk,bshk,btshk->btsh', qb, kb, jnp.exp(diff))
        o = o_inter + jnp.einsum('btsh,bshv->bthv', att, vb)
        b_last = b[:, -1]
        k_dec = kb * jnp.exp(b_last[:, None] - b)
        S = jnp.exp(b_last)[..., None] * S + jnp.einsum('bshk,bshv->bhkv', k_dec, vb)
        return S, o

    S, o = lax.scan(step, s0.astype(jnp.float32), (qc, kc, vc, gc))
    o = o.swapaxes(0, 1).reshape(B, n * c, q.shape[2], v.shape[3])[:, :L]
    return o, S


def hgrn2_mixer(h, s0, w_in, lb, g_norm, w_out):
    B, L, _ = h.shape
    proj = h @ w_in
    q, f, i, z = jnp.split(proj, [D_MODEL, 2 * D_MODEL, 2 * D_MODEL + D_INNER], axis=-1)
    q = jax.nn.silu(q).reshape(B, L, A_HEADS, A_DK)
    lbh = lb.astype(jnp.float32).reshape(A_HEADS, A_DK)
    fpre = f.astype(jnp.float32).reshape(B, L, A_HEADS, A_DK)
    logf = jnp.logaddexp(jnp.log(lbh), jnp.log1p(-lbh) + jax.nn.log_sigmoid(fpre))
    k = -jnp.expm1(logf)
    v = i.reshape(B, L, A_HEADS, A_DV)
    o, S = hgrn2_chunkwise(q, k, v, logf, s0)
    o = rmsnorm(o, g_norm).reshape(B, L, D_INNER).astype(h.dtype)
    return (o * jax.nn.silu(z)) @ w_out, S


def chunk_mlp_mixer(h, w_in, ln_g, ln_b, w_sp, b_sp, w_out):
    B, L, _ = h.shape
    proj = h @ w_in
    u, v, z = jnp.split(proj, 3, axis=-1)
    u = jax.nn.gelu(u)
    v = layernorm(jax.nn.gelu(v), ln_g, ln_b)
    c = B_CHUNK
    n = -(-L // c)
    pad = n * c - L
    vc = jnp.pad(v, ((0, 0), (0, pad), (0, 0))).reshape(B, n, c, B_GROUPS, B_DG)
    w = jnp.where(jnp.tril(jnp.ones((c, c), dtype=bool))[None], w_sp, 0.0)
    s = jnp.einsum('gts,bnsgd->bntgd', w, vc) + b_sp.T[None, None, :, :, None]
    s = s.reshape(B, n * c, D_INNER)[:, :L]
    y = (u * s * jax.nn.silu(z)) @ w_out
    v_tail = v[:, ((L - 1) // c) * c:]
    return y, v_tail


def trunk(x, p, s_hgrn, lbs, norm_mix, w_in_a, gnorm_a, w_out_a, w_in_b, ln_v_g, ln_v_b,
          w_spatial, b_spatial, w_out_b, norm_ple, w_ple_gate, w_ple_proj, norm_final):
    h = x
    new_s, new_v = [], []
    for i in range(DEPTH):
        j = i // N_MIXERS
        hn = rmsnorm(h, norm_mix[i])
        if i % N_MIXERS == 0:
            y, S = hgrn2_mixer(hn, s_hgrn[j], w_in_a[j], lbs[j], gnorm_a[j], w_out_a[j])
            new_s.append(S)
        else:
            y, vt = chunk_mlp_mixer(hn, w_in_b[j], ln_v_g[j], ln_v_b[j], w_spatial[j], b_spatial[j], w_out_b[j])
            new_v.append(vt)
        h = h + y
        gate = jax.nn.sigmoid(rmsnorm(h, norm_ple[i]) @ w_ple_gate[i])
        h = h + gate * (p[i] @ w_ple_proj[i])
    return rmsnorm(h, norm_final), jnp.stack(new_s), jnp.stack(new_v)


def setup_inputs(seed: int = 0) -> dict:
    key = jax.random.key(seed)
    ks = jax.random.split(key, 24)
    nrm = jax.random.normal
    f32 = jnp.float32
    return {
        "x_prompt": nrm(ks[0], (BATCH, SEQ, D_MODEL), f32),
        "x_sample": nrm(ks[1], (DEC_BATCH, DEC_SEQ, D_MODEL), f32),
        "state_hgrn": nrm(ks[2], (N_A, DEC_BATCH, A_HEADS, A_DK, A_DV), f32),
        "p_prompt": nrm(ks[3], (DEPTH, BATCH, SEQ, PLE_DIM), f32),
        "p_sample": nrm(ks[4], (DEPTH, DEC_BATCH, DEC_SEQ, PLE_DIM), f32),
        "norm_mix": 1.0 + 0.05 * nrm(ks[5], (DEPTH, D_MODEL), f32),
        "w_in_a": nrm(ks[6], (N_A, D_MODEL, 3 * D_INNER), f32) * D_MODEL ** -0.5,
        "lb_logits": nrm(ks[7], (N_A, A_HEADS * A_DK), f32),
        "gnorm_a": 1.0 + 0.05 * nrm(ks[8], (N_A, A_DV), f32),
        "w_out_a": nrm(ks[9], (N_A, D_INNER, D_MODEL), f32) * D_INNER ** -0.5,
        "w_in_b": nrm(ks[10], (N_B, D_MODEL, 3 * D_INNER), f32) * D_MODEL ** -0.5,
        "ln_v_g": 1.0 + 0.05 * nrm(ks[11], (N_B, D_INNER), f32),
        "ln_v_b": 0.02 * nrm(ks[12], (N_B, D_INNER), f32),
        "w_spatial": nrm(ks[13], (N_B, B_GROUPS, B_CHUNK, B_CHUNK), f32) * B_CHUNK ** -0.5,
        "b_spatial": 1.0 + 0.1 * nrm(ks[14], (N_B, B_GROUPS, B_CHUNK), f32),
        "w_out_b": nrm(ks[15], (N_B, D_INNER, D_MODEL), f32) * D_INNER ** -0.5,
        "norm_ple": 1.0 + 0.05 * nrm(ks[16], (DEPTH, D_MODEL), f32),
        "w_ple_gate": nrm(ks[17], (DEPTH, D_MODEL, D_MODEL), f32) * D_MODEL ** -0.5,
        "w_ple_proj": nrm(ks[18], (DEPTH, PLE_DIM, D_MODEL), f32) * PLE_DIM ** -0.5,
        "norm_final": 1.0 + 0.05 * nrm(ks[19], (D_MODEL,), f32),
    }


def reference(x_prompt, x_sample, state_hgrn, p_prompt, p_sample, norm_mix, w_in_a, lb_logits,
              gnorm_a, w_out_a, w_in_b, ln_v_g, ln_v_b, w_spatial, b_spatial, w_out_b,
              norm_ple, w_ple_gate, w_ple_proj, norm_final):
    lb_cum = jnp.cumsum(jax.nn.softmax(lb_logits.astype(jnp.float32), axis=0), axis=0)
    lbs = lb_cum - lb_cum[0:1]
    s_zero = jnp.zeros((N_A, x_prompt.shape[0], A_HEADS, A_DK, A_DV), jnp.float32)
    y_prompt, state_hgrn_prompt, chunk_v_prompt = trunk(
        x_prompt, p_prompt, s_zero, lbs, norm_mix, w_in_a, gnorm_a, w_out_a, w_in_b, ln_v_g, ln_v_b,
        w_spatial, b_spatial, w_out_b, norm_ple, w_ple_gate, w_ple_proj, norm_final)
    y_sample, state_hgrn_sample, chunk_v_sample = trunk(
        x_sample, p_sample, state_hgrn, lbs, norm_mix, w_in_a, gnorm_a, w_out_a, w_in_b, ln_v_g, ln_v_b,
        w_spatial, b_spatial, w_out_b, norm_ple, w_ple_gate, w_ple_proj, norm_final)
    return (y_prompt, y_sample, state_hgrn_prompt, state_hgrn_sample, chunk_v_prompt, chunk_v_sample)
```

```python
import functools
import math

import numpy as np
import jax
import jax.numpy as jnp
from jax import lax
from jax.experimental import pallas as pl
from jax.experimental.pallas import tpu as pltpu

D_MODEL = 1024
D_INNER = 2048
HEADS = 8
DK = 128
DV = 256
GROUPS = 8
DG = 256
B_CHUNK = 128
PLE_DIM = 256
EPS = 1e-6

TL = 256
N_LEVELS = int(math.log2(TL))
SAMPLE_TB = 4
VMEM_LIMIT = 58 * 1024 * 1024

F32 = jnp.float32
BF16 = jnp.bfloat16


def _build_level_consts(tl):
    t = np.arange(tl)[:, None]
    s = np.arange(tl)[None, :]
    blocks = [(s <= t)]
    hs = tl // 2
    while hs >= 1:
        g = 2 * hs
        mid = t - t % g + hs
        qrow = (t % g) >= hs
        blocks.append(np.where(qrow, (s >= mid) & (s <= t), (s > t) & (s < mid)))
        hs //= 2
    m = np.concatenate(blocks, axis=0).astype(np.float32)
    x = np.maximum(t ^ s, 1)
    p = np.floor(np.log2(x)).astype(np.int32)
    nlev = int(math.log2(tl))
    lvl = np.where(t == s, 0, np.where(t > s, nlev - p, -1)).astype(np.int32)
    return m, lvl


_M_STACK_NP, _LVL_NP = _build_level_consts(TL)


def _dot(a, b):
    return jnp.dot(a, b, preferred_element_type=F32)


def _dot_nt(a, b):
    return lax.dot_general(a, b, (((1,), (1,)), ((), ())), preferred_element_type=F32)


def _dot_tn(a, b):
    return lax.dot_general(a, b, (((0,), (0,)), ((), ())), preferred_element_type=F32)


def _rmsnorm(x, g):
    ms = jnp.mean(x * x, axis=-1, keepdims=True)
    return x * lax.rsqrt(ms + EPS) * g


def _silu(x):
    return x * jax.nn.sigmoid(x)


def _gelu_tanh(x):
    c = math.sqrt(2.0 / math.pi)
    return 0.5 * x * (1.0 + jnp.tanh(c * (x + 0.044715 * (x * x * x))))


def _forget_lower_bound(lbl, j):
    mx = jnp.max(lbl, axis=0, keepdims=True)
    e = jnp.exp(lbl - mx)
    sm = e / jnp.sum(e, axis=0, keepdims=True)
    cum0 = sm[0:1, :]
    cum = cum0
    for i in range(1, j + 1):
        cum = cum + sm[i:i + 1, :]
    return cum - cum0


def _forget_gates(fpre, lb):
    e = jnp.exp(-jnp.abs(fpre))
    log_sig = jnp.minimum(fpre, 0.0) - jnp.log1p(e)
    a = jnp.log(lb)
    y = jnp.log1p(-lb) + log_sig
    logf = jnp.maximum(a, y) + jnp.log1p(jnp.exp(-jnp.abs(a - y)))
    one_minus_f = (1.0 - lb) * (jnp.where(fpre >= 0.0, e, 1.0) / (1.0 + e))
    return logf, one_minus_f


def _hgrn_front(x, nm, w_in, lb):
    hn = _rmsnorm(x, nm).astype(BF16)
    proj = _dot(hn, w_in)
    q = _silu(proj[:, :D_MODEL])
    logf, kk = _forget_gates(proj[:, D_MODEL:2 * D_MODEL], lb)
    v = proj[:, 2 * D_MODEL:2 * D_MODEL + D_INNER]
    z = proj[:, 2 * D_MODEL + D_INNER:]
    return q, logf, kk, v, z


def _head_rmsnorm(o, gn):
    outs = []
    for h in range(HEADS):
        outs.append(_rmsnorm(o[:, h * DV:(h + 1) * DV], gn))
    return jnp.concatenate(outs, axis=1)


def _ple(h, p, npl, w_gate, w_proj):
    gate = jax.nn.sigmoid(_dot(_rmsnorm(h, npl).astype(BF16), w_gate))
    return h + gate * _dot(p.astype(BF16), w_proj)


def _mixer_back(x, branch, w_out, p, npl, w_gate, w_proj):
    h = x + _dot(branch.astype(BF16), w_out)
    return _ple(h, p, npl, w_gate, w_proj)


def _gmlp_front(x, nm, w_in, ln_g, ln_b):
    hn = _rmsnorm(x, nm).astype(BF16)
    proj = _dot(hn, w_in)
    u = _gelu_tanh(proj[:, :D_INNER])
    vg = _gelu_tanh(proj[:, D_INNER:2 * D_INNER])
    z = proj[:, 2 * D_INNER:]
    mu = jnp.mean(vg, axis=-1, keepdims=True)
    vc = vg - mu
    vn = vc * lax.rsqrt(jnp.mean(vc * vc, axis=-1, keepdims=True) + EPS) * ln_g + ln_b
    return u, vn, z


def _hgrn_prompt_kernel(x_ref, p_ref, lbl_ref, nm_ref, win_ref, gn_ref, wout_ref, npl_ref,
                        wg_ref, wp_ref, mst_ref, lvl_ref, y_ref, sfin_ref, st_ref, *, layer_j):
    l = pl.program_id(1)

    @pl.when(l == 0)
    def _():
        st_ref[...] = jnp.zeros_like(st_ref)

    x = x_ref[...]
    lb = _forget_lower_bound(lbl_ref[...], layer_j)
    q, logf, kk, v, z = _hgrn_front(x, nm_ref[...], win_ref[...], lb)

    g_hi = logf.astype(BF16)
    g_lo = (logf - g_hi.astype(F32)).astype(BF16)
    g2 = jnp.concatenate([g_hi, g_lo], axis=1)

    def summed(level):
        e2 = _dot(mst_ref[level * TL:(level + 1) * TL, :], g2)
        return e2[:, :D_MODEL] + e2[:, D_MODEL:]

    b = summed(0)
    b_last = b[TL - 1:TL, :]
    qs = (q * jnp.exp(b)).astype(BF16)
    kd = (kk * jnp.exp(b_last - b)).astype(BF16)
    dec = jnp.exp(b_last)
    qb = q.astype(BF16)
    kb = kk.astype(BF16)
    vb = v.astype(BF16)

    rows = lax.broadcasted_iota(jnp.int32, (TL, D_MODEL), 0)
    zs = []
    for level in range(1, N_LEVELS + 1):
        hs = TL >> level
        is_q = (rows & hs) != 0
        zs.append((jnp.where(is_q, q, kk) * jnp.exp(summed(level))).astype(BF16))

    lvl = lvl_ref[...]
    outs = []
    for h in range(HEADS):
        sl = slice(h * DK, (h + 1) * DK)
        vsl = slice(h * DV, (h + 1) * DV)
        st = st_ref[h]
        o = _dot_nt(qs[:, sl], st.astype(BF16))
        att = jnp.where(lvl == 0, _dot_nt(qb[:, sl], kb[:, sl]), 0.0)
        for level in range(1, N_LEVELS + 1):
            zl = zs[level - 1][:, sl]
            att = jnp.where(lvl == level, _dot_nt(zl, zl), att)
        o = o + _dot(att.astype(BF16), vb[:, vsl])
        st_ref[h] = st * dec[:, sl] + _dot_tn(vb[:, vsl], kd[:, sl])
        outs.append(o)
    o = jnp.concatenate(outs, axis=1)

    branch = _head_rmsnorm(o, gn_ref[...]) * _silu(z)
    y_ref[...] = _mixer_back(x, branch, wout_ref[...], p_ref[...], npl_ref[...],
                             wg_ref[...], wp_ref[...])

    @pl.when(l == pl.num_programs(1) - 1)
    def _():
        for h in range(HEADS):
            sfin_ref[h] = st_ref[h].T


def _gmlp_prompt_kernel(x_ref, p_ref, nm_ref, win_ref, lng_ref, lnb_ref, wsp_ref, bsp_ref,
                        wout_ref, npl_ref, wg_ref, wp_ref, nf_ref, y_ref, vt_ref, *, final_norm):
    l = pl.program_id(1)
    x = x_ref[...]
    u, vn, z = _gmlp_front(x, nm_ref[...], win_ref[...], lng_ref[...], lnb_ref[...])
    vb = vn.astype(BF16)
    r = lax.broadcasted_iota(jnp.int32, (B_CHUNK, B_CHUNK), 0)
    c = lax.broadcasted_iota(jnp.int32, (B_CHUNK, B_CHUNK), 1)
    tril = c <= r
    bias = bsp_ref[...]
    cols = []
    for g in range(GROUPS):
        w = jnp.where(tril, wsp_ref[g], 0.0).astype(BF16)
        gsl = slice(g * DG, (g + 1) * DG)
        chunks = []
        for ci in range(TL // B_CHUNK):
            rsl = slice(ci * B_CHUNK, (ci + 1) * B_CHUNK)
            chunks.append(_dot(w, vb[rsl, gsl]) + bias[:, gsl])
        cols.append(jnp.concatenate(chunks, axis=0))
    s = jnp.concatenate(cols, axis=1)
    branch = u * s * _silu(z)
    h = _mixer_back(x, branch, wout_ref[...], p_ref[...], npl_ref[...], wg_ref[...], wp_ref[...])
    if final_norm:
        h = _rmsnorm(h, nf_ref[...])
    y_ref[...] = h

    @pl.when(l == pl.num_programs(1) - 1)
    def _():
        vt_ref[...] = vn[TL - B_CHUNK:, :]


def _hgrn_sample_front_kernel(x_ref, lbl_ref, nm_ref, win_ref, q_ref, f_ref, k_ref, v_ref, z_ref,
                              *, layer_j):
    lb = _forget_lower_bound(lbl_ref[...], layer_j)
    q, logf, kk, v, z = _hgrn_front(x_ref[...], nm_ref[...], win_ref[...], lb)
    q_ref[...] = q
    f_ref[...] = jnp.exp(logf)
    k_ref[...] = kk
    v_ref[...] = v
    z_ref[...] = z


def _hgrn_sample_state_kernel(q_ref, f_ref, k_ref, v_ref, s_ref, *rest):
    snew_ref, o_ref = rest[-2:]
    for i in range(SAMPLE_TB):
        q_t = q_ref[i].T
        f_t = f_ref[i].T
        k_t = k_ref[i].T
        for h in range(HEADS):
            s_new = f_t[:, h:h + 1] * s_ref[i, h] + k_t[:, h:h + 1] * v_ref[i, h:h + 1, :]
            snew_ref[i, h] = s_new
            o_ref[i, h:h + 1, :] = jnp.sum(q_t[:, h:h + 1] * s_new, axis=0, keepdims=True)


def _hgrn_sample_back_kernel(x_ref, o_ref, z_ref, p_ref, gn_ref, wout_ref, npl_ref, wg_ref,
                             wp_ref, y_ref):
    branch = _head_rmsnorm(o_ref[...], gn_ref[...]) * _silu(z_ref[...])
    y_ref[...] = _mixer_back(x_ref[...], branch, wout_ref[...], p_ref[...], npl_ref[...],
                             wg_ref[...], wp_ref[...])


def _gmlp_sample_kernel(x_ref, p_ref, nm_ref, win_ref, lng_ref, lnb_ref, w00_ref, b0_ref,
                        wout_ref, npl_ref, wg_ref, wp_ref, nf_ref, y_ref, vt_ref, *, final_norm):
    x = x_ref[...]
    u, vn, z = _gmlp_front(x, nm_ref[...], win_ref[...], lng_ref[...], lnb_ref[...])
    s = w00_ref[...] * vn + b0_ref[...]
    branch = u * s * _silu(z)
    h = _mixer_back(x, branch, wout_ref[...], p_ref[...], npl_ref[...], wg_ref[...], wp_ref[...])
    if final_norm:
        h = _rmsnorm(h, nf_ref[...])
    y_ref[...] = h
    vt_ref[...] = vn


def _resident(shape, index):
    return pl.BlockSpec(shape, index, pipeline_mode=pl.Buffered(1))


def _params(*sem):
    return pltpu.CompilerParams(dimension_semantics=sem, vmem_limit_bytes=VMEM_LIMIT)


def _hgrn_prompt_layer(x, p_all, i, j, lb_logits, norm_mix, w_in, gnorm, w_out, norm_ple,
                       w_gate, w_proj, m_stack, lvl):
    bsz, seq, _ = x.shape
    n_a = lb_logits.shape[0]
    grid = (bsz, seq // TL)
    const2 = lambda b, l: (0, 0)
    in_specs = [
        pl.BlockSpec((None, TL, D_MODEL), lambda b, l: (b, l, 0)),
        pl.BlockSpec((None, None, TL, PLE_DIM), lambda b, l: (i, b, l, 0)),
        _resident((n_a, D_MODEL), const2),
        _resident((None, 1, D_MODEL), lambda b, l: (i, 0, 0)),
        _resident((None, D_MODEL, 3 * D_INNER), lambda b, l: (j, 0, 0)),
        _resident((None, 1, DV), lambda b, l: (j, 0, 0)),
        _resident((None, D_INNER, D_MODEL), lambda b, l: (j, 0, 0)),
        _resident((None, 1, D_MODEL), lambda b, l: (i, 0, 0)),
        _resident((None, D_MODEL, D_MODEL), lambda b, l: (i, 0, 0)),
        _resident((None, PLE_DIM, D_MODEL), lambda b, l: (i, 0, 0)),
        _resident(m_stack.shape, const2),
        _resident(lvl.shape, const2),
    ]
    out_specs = [
        pl.BlockSpec((None, TL, D_MODEL), lambda b, l: (b, l, 0)),
        pl.BlockSpec((None, HEADS, DK, DV), lambda b, l: (b, 0, 0, 0)),
    ]
    out_shape = [
        jax.ShapeDtypeStruct((bsz, seq, D_MODEL), F32),
        jax.ShapeDtypeStruct((bsz, HEADS, DK, DV), F32),
    ]
    return pl.pallas_call(
        functools.partial(_hgrn_prompt_kernel, layer_j=j),
        grid=grid, in_specs=in_specs, out_specs=out_specs, out_shape=out_shape,
        scratch_shapes=[pltpu.VMEM((HEADS, DV, DK), F32)],
        compiler_params=_params("arbitrary", "arbitrary"),
        name=f"hgrn_prompt_{j}",
    )(x, p_all, lb_logits, norm_mix, w_in, gnorm, w_out, norm_ple, w_gate, w_proj, m_stack, lvl)


def _gmlp_prompt_layer(x, p_all, i, j, norm_mix, w_in, ln_g, ln_b, w_sp, bias_full, w_out,
                       norm_ple, w_gate, w_proj, norm_final, final_norm):
    bsz, seq, _ = x.shape
    grid = (bsz, seq // TL)
    in_specs = [
        pl.BlockSpec((None, TL, D_MODEL), lambda b, l: (b, l, 0)),
        pl.BlockSpec((None, None, TL, PLE_DIM), lambda b, l: (i, b, l, 0)),
        _resident((None, 1, D_MODEL), lambda b, l: (i, 0, 0)),
        _resident((None, D_MODEL, 3 * D_INNER), lambda b, l: (j, 0, 0)),
        _resident((None, 1, D_INNER), lambda b, l: (j, 0, 0)),
        _resident((None, 1, D_INNER), lambda b, l: (j, 0, 0)),
        _resident((None, GROUPS, B_CHUNK, B_CHUNK), lambda b, l: (j, 0, 0, 0)),
        _resident((None, B_CHUNK, D_INNER), lambda b, l: (j, 0, 0)),
        _resident((None, D_INNER, D_MODEL), lambda b, l: (j, 0, 0)),
        _resident((None, 1, D_MODEL), lambda b, l: (i, 0, 0)),
        _resident((None, D_MODEL, D_MODEL), lambda b, l: (i, 0, 0)),
        _resident((None, PLE_DIM, D_MODEL), lambda b, l: (i, 0, 0)),
        _resident((1, D_MODEL), lambda b, l: (0, 0)),
    ]
    out_specs = [
        pl.BlockSpec((None, TL, D_MODEL), lambda b, l: (b, l, 0)),
        pl.BlockSpec((None, B_CHUNK, D_INNER), lambda b, l: (b, 0, 0)),
    ]
    out_shape = [
        jax.ShapeDtypeStruct((bsz, seq, D_MODEL), F32),
        jax.ShapeDtypeStruct((bsz, B_CHUNK, D_INNER), F32),
    ]
    return pl.pallas_call(
        functools.partial(_gmlp_prompt_kernel, final_norm=final_norm),
        grid=grid, in_specs=in_specs, out_specs=out_specs, out_shape=out_shape,
        compiler_params=_params("arbitrary", "arbitrary"),
        name=f"gmlp_prompt_{j}",
    )(x, p_all, norm_mix, w_in, ln_g, ln_b, w_sp, bias_full, w_out, norm_ple, w_gate, w_proj,
      norm_final)


def _whole(shape):
    nd = len(shape)
    return pl.BlockSpec(shape, lambda *_: (0,) * nd)


def _hgrn_sample_layer(x, p_all, s0, s_prev, i, j, lb_logits, norm_mix, w_in, gnorm, w_out,
                       norm_ple, w_gate, w_proj):
    n = x.shape[0]
    n_a = lb_logits.shape[0]
    q, f, kk, v, z = pl.pallas_call(
        functools.partial(_hgrn_sample_front_kernel, layer_j=j),
        grid=(1,),
        in_specs=[
            _whole((n, D_MODEL)),
            _whole((n_a, D_MODEL)),
            pl.BlockSpec((None, 1, D_MODEL), lambda g: (i, 0, 0)),
            pl.BlockSpec((None, D_MODEL, 3 * D_INNER), lambda g: (j, 0, 0)),
        ],
        out_specs=[_whole((n, D_MODEL))] * 3 + [_whole((n, D_INNER))] * 2,
        out_shape=[jax.ShapeDtypeStruct((n, D_MODEL), F32)] * 3
        + [jax.ShapeDtypeStruct((n, D_INNER), F32)] * 2,
        compiler_params=_params("arbitrary"),
        name=f"hgrn_sample_front_{j}",
    )(x, lb_logits, norm_mix, w_in)

    tb = SAMPLE_TB
    hk = pl.BlockSpec((tb, HEADS, DK), lambda g: (g, 0, 0))
    hv = pl.BlockSpec((tb, HEADS, DV), lambda g: (g, 0, 0))
    st = pl.BlockSpec((None, tb, HEADS, DK, DV), lambda g: (j, g, 0, 0, 0))
    operands = [q.reshape(n, HEADS, DK), f.reshape(n, HEADS, DK), kk.reshape(n, HEADS, DK),
                v.reshape(n, HEADS, DV), s0]
    in_specs = [hk, hk, hk, hv, st]
    aliases = {}
    if s_prev is not None:
        operands.append(s_prev)
        in_specs.append(pl.BlockSpec(memory_space=pl.ANY))
        aliases = {len(operands) - 1: 0}
    s_new, o = pl.pallas_call(
        _hgrn_sample_state_kernel,
        grid=(n // tb,),
        in_specs=in_specs,
        out_specs=[st, hv],
        out_shape=[jax.ShapeDtypeStruct(s0.shape, F32),
                   jax.ShapeDtypeStruct((n, HEADS, DV), F32)],
        input_output_aliases=aliases,
        compiler_params=_params("arbitrary"),
        name=f"hgrn_sample_state_{j}",
    )(*operands)

    y = pl.pallas_call(
        _hgrn_sample_back_kernel,
        grid=(1,),
        in_specs=[
            _whole((n, D_MODEL)),
            _whole((n, D_INNER)),
            _whole((n, D_INNER)),
            pl.BlockSpec((None, n, PLE_DIM), lambda g: (i, 0, 0)),
            pl.BlockSpec((None, 1, DV), lambda g: (j, 0, 0)),
            pl.BlockSpec((None, D_INNER, D_MODEL), lambda g: (j, 0, 0)),
            pl.BlockSpec((None, 1, D_MODEL), lambda g: (i, 0, 0)),
            pl.BlockSpec((None, D_MODEL, D_MODEL), lambda g: (i, 0, 0)),
            pl.BlockSpec((None, PLE_DIM, D_MODEL), lambda g: (i, 0, 0)),
        ],
        out_specs=_whole((n, D_MODEL)),
        out_shape=jax.ShapeDtypeStruct((n, D_MODEL), F32),
        compiler_params=_params("arbitrary"),
        name=f"hgrn_sample_back_{j}",
    )(x, o.reshape(n, D_INNER), z, p_all, gnorm, w_out, norm_ple, w_gate, w_proj)
    return y, s_new


def _gmlp_sample_layer(x, p_all, i, j, norm_mix, w_in, ln_g, ln_b, w00, b0, w_out, norm_ple,
                       w_gate, w_proj, norm_final, final_norm):
    n = x.shape[0]
    return pl.pallas_call(
        functools.partial(_gmlp_sample_kernel, final_norm=final_norm),
        grid=(1,),
        in_specs=[
            _whole((n, D_MODEL)),
            pl.BlockSpec((None, n, PLE_DIM), lambda g: (i, 0, 0)),
            pl.BlockSpec((None, 1, D_MODEL), lambda g: (i, 0, 0)),
            pl.BlockSpec((None, D_MODEL, 3 * D_INNER), lambda g: (j, 0, 0)),
            pl.BlockSpec((None, 1, D_INNER), lambda g: (j, 0, 0)),
            pl.BlockSpec((None, 1, D_INNER), lambda g: (j, 0, 0)),
            pl.BlockSpec((None, 1, D_INNER), lambda g: (j, 0, 0)),
            pl.BlockSpec((None, 1, D_INNER), lambda g: (j, 0, 0)),
            pl.BlockSpec((None, D_INNER, D_MODEL), lambda g: (j, 0, 0)),
            pl.BlockSpec((None, 1, D_MODEL), lambda g: (i, 0, 0)),
            pl.BlockSpec((None, D_MODEL, D_MODEL), lambda g: (i, 0, 0)),
            pl.BlockSpec((None, PLE_DIM, D_MODEL), lambda g: (i, 0, 0)),
            _whole((1, D_MODEL)),
        ],
        out_specs=[_whole((n, D_MODEL)), _whole((n, D_INNER))],
        out_shape=[jax.ShapeDtypeStruct((n, D_MODEL), F32),
                   jax.ShapeDtypeStruct((n, D_INNER), F32)],
        compiler_params=_params("arbitrary"),
        name=f"gmlp_sample_{j}",
    )(x, p_all, norm_mix, w_in, ln_g, ln_b, w00, b0, w_out, norm_ple, w_gate, w_proj, norm_final)


def kernel(x_prompt, x_sample, state_hgrn, p_prompt, p_sample, norm_mix, w_in_a, lb_logits,
           gnorm_a, w_out_a, w_in_b, ln_v_g, ln_v_b, w_spatial, b_spatial, w_out_b,
           norm_ple, w_ple_gate, w_ple_proj, norm_final):
    depth = norm_mix.shape[0]
    n_s = x_sample.shape[0]
    assert x_prompt.shape[1] % TL == 0 and TL % B_CHUNK == 0 and x_sample.shape[1] == 1
    assert n_s % SAMPLE_TB == 0

    w_in_a_b = w_in_a.astype(BF16)
    w_out_a_b = w_out_a.astype(BF16)
    w_in_b_b = w_in_b.astype(BF16)
    w_out_b_b = w_out_b.astype(BF16)
    w_gate_b = w_ple_gate.astype(BF16)
    w_proj_b = w_ple_proj.astype(BF16)

    m_stack = jnp.asarray(_M_STACK_NP, dtype=BF16)
    lvl = jnp.asarray(_LVL_NP)
    nf = norm_final.reshape(1, D_MODEL)
    bias_full = jnp.repeat(jnp.swapaxes(b_spatial, 1, 2), DG, axis=2)
    w00 = jnp.repeat(w_spatial[:, :, 0, 0], DG, axis=1)[:, None, :]
    b0 = jnp.repeat(b_spatial[:, :, 0], DG, axis=1)[:, None, :]
    nmix = norm_mix[:, None, :]
    nple = norm_ple[:, None, :]
    gn = gnorm_a[:, None, :]
    lng = ln_v_g[:, None, :]
    lnb = ln_v_b[:, None, :]

    if depth % 2 == 1:
        raise NotImplementedError("final norm is fused into the last (chunk-MLP) layer")
    hp = x_prompt
    hs = x_sample.reshape(n_s, D_MODEL)
    ps = p_sample.reshape(depth, n_s, PLE_DIM)
    st_p, vt_p, vt_s = [], [], []
    st_s = None
    for i in range(depth):
        j = i // 2
        last = i == depth - 1
        if i % 2 == 0:
            hp, s_fin = _hgrn_prompt_layer(hp, p_prompt, i, j, lb_logits, nmix, w_in_a_b,
                                           gn, w_out_a_b, nple, w_gate_b, w_proj_b,
                                           m_stack, lvl)
            st_p.append(s_fin)
            hs, st_s = _hgrn_sample_layer(hs, ps, state_hgrn, st_s, i, j, lb_logits, nmix,
                                          w_in_a_b, gn, w_out_a_b, nple, w_gate_b, w_proj_b)
        else:
            hp, vt = _gmlp_prompt_layer(hp, p_prompt, i, j, nmix, w_in_b_b, lng, lnb,
                                        w_spatial, bias_full, w_out_b_b, nple, w_gate_b,
                                        w_proj_b, nf, last)
            vt_p.append(vt)
            hs, vts = _gmlp_sample_layer(hs, ps, i, j, nmix, w_in_b_b, lng, lnb,
                                         w00, b0, w_out_b_b, nple, w_gate_b, w_proj_b, nf, last)
            vt_s.append(vts.reshape(n_s, 1, D_INNER))
    return (hp, hs.reshape(n_s, 1, D_MODEL), jnp.stack(st_p), st_s,
            jnp.stack(vt_p), jnp.stack(vt_s))
```

```python
import functools
import math

import numpy as np
import jax
import jax.numpy as jnp
from jax import lax
from jax.experimental import pallas as pl
from jax.experimental.pallas import tpu as pltpu

D_MODEL = 1024
D_INNER = 2048
HEADS = 8
DK = 128
DV = 256
GROUPS = 8
DG = 256
B_CHUNK = 128
PLE_DIM = 256
EPS = 1e-6

TL = 256
N_LEVELS = int(math.log2(TL))
SAMPLE_TB = 4
VMEM_LIMIT = 58 * 1024 * 1024

F32 = jnp.float32
BF16 = jnp.bfloat16


def _build_level_consts(tl):
    t = np.arange(tl)[:, None]
    s = np.arange(tl)[None, :]
    tri = (s <= t).astype(np.float32)
    x = np.maximum(t ^ s, 1)
    p = np.floor(np.log2(x)).astype(np.int32)
    nlev = int(math.log2(tl))
    lvl = np.where(t == s, 0, np.where(t > s, nlev - p, -1)).astype(np.int32)
    return tri, lvl


_TRI_NP, _LVL_NP = _build_level_consts(TL)


def _dot(a, b):
    return jnp.dot(a, b, preferred_element_type=F32)


def _dot_nt(a, b):
    return lax.dot_general(a, b, (((1,), (1,)), ((), ())), preferred_element_type=F32)


def _dot_tn(a, b):
    return lax.dot_general(a, b, (((0,), (0,)), ((), ())), preferred_element_type=F32)


def _rmsnorm(x, g):
    ms = jnp.mean(x * x, axis=-1, keepdims=True)
    return x * lax.rsqrt(ms + EPS) * g


def _silu(x):
    return x * jax.nn.sigmoid(x)


def _gelu_tanh(x):
    c = math.sqrt(2.0 / math.pi)
    return 0.5 * x * (1.0 + jnp.tanh(c * (x + 0.044715 * (x * x * x))))


def _forget_lower_bound(lbl, j):
    mx = jnp.max(lbl, axis=0, keepdims=True)
    e = jnp.exp(lbl - mx)
    sm = e / jnp.sum(e, axis=0, keepdims=True)
    cum0 = sm[0:1, :]
    cum = cum0
    for i in range(1, j + 1):
        cum = cum + sm[i:i + 1, :]
    return cum - cum0


def _forget_gates(fpre, lb):
    e = jnp.exp(-jnp.abs(fpre))
    log_sig = jnp.minimum(fpre, 0.0) - jnp.log1p(e)
    a = jnp.log(lb)
    y = jnp.log1p(-lb) + log_sig
    logf = jnp.maximum(a, y) + jnp.log1p(jnp.exp(-jnp.abs(a - y)))
    one_minus_f = (1.0 - lb) * (jnp.where(fpre >= 0.0, e, 1.0) / (1.0 + e))
    return logf, one_minus_f


def _hgrn_front(x, nm, w_in, lb):
    hn = _rmsnorm(x, nm).astype(BF16)
    proj = _dot(hn, w_in)
    q = _silu(proj[:, :D_MODEL])
    logf, kk = _forget_gates(proj[:, D_MODEL:2 * D_MODEL], lb)
    v = proj[:, 2 * D_MODEL:2 * D_MODEL + D_INNER]
    z = proj[:, 2 * D_MODEL + D_INNER:]
    return q, logf, kk, v, z


def _head_rmsnorm(o, gn):
    outs = []
    for h in range(HEADS):
        outs.append(_rmsnorm(o[:, h * DV:(h + 1) * DV], gn))
    return jnp.concatenate(outs, axis=1)


def _ple(h, p, npl, w_gate, w_proj):
    gate = jax.nn.sigmoid(_dot(_rmsnorm(h, npl).astype(BF16), w_gate))
    return h + gate * _dot(p.astype(BF16), w_proj)


def _mixer_back(x, branch, w_out, p, npl, w_gate, w_proj):
    h = x + _dot(branch.astype(BF16), w_out)
    return _ple(h, p, npl, w_gate, w_proj)


def _gmlp_front(x, nm, w_in, ln_g, ln_b):
    hn = _rmsnorm(x, nm).astype(BF16)
    proj = _dot(hn, w_in)
    u = _gelu_tanh(proj[:, :D_INNER])
    vg = _gelu_tanh(proj[:, D_INNER:2 * D_INNER])
    z = proj[:, 2 * D_INNER:]
    mu = jnp.mean(vg, axis=-1, keepdims=True)
    vc = vg - mu
    vn = vc * lax.rsqrt(jnp.mean(vc * vc, axis=-1, keepdims=True) + EPS) * ln_g + ln_b
    return u, vn, z


def _hgrn_prompt_kernel(x_ref, p_ref, lbl_ref, nm_ref, win_ref, gn_ref, wout_ref, npl_ref,
                        wg_ref, wp_ref, tri_ref, lvl_ref, y_ref, sfin_ref, st_ref, *, layer_j):
    l = pl.program_id(1)

    @pl.when(l == 0)
    def _():
        st_ref[...] = jnp.zeros_like(st_ref)

    x = x_ref[...]
    lb = _forget_lower_bound(lbl_ref[...], layer_j)
    q, logf, kk, v, z = _hgrn_front(x, nm_ref[...], win_ref[...], lb)

    g_hi = logf.astype(BF16)
    g_lo = (logf - g_hi.astype(F32)).astype(BF16)
    g2 = jnp.concatenate([g_hi, g_lo], axis=1)
    b2 = _dot(tri_ref[...], g2)
    b = b2[:, :D_MODEL] + b2[:, D_MODEL:]
    b_last = b[TL - 1:TL, :]
    qs = (q * jnp.exp(b)).astype(BF16)
    kd = (kk * jnp.exp(b_last - b)).astype(BF16)
    dec = jnp.exp(b_last)
    qb = q.astype(BF16)
    kb = kk.astype(BF16)
    vb = v.astype(BF16)

    zk, zq = {}, {}
    for level in range(1, N_LEVELS + 1):
        hs = TL >> level
        if hs >= 8:
            grp = 2 * hs
            shp = (TL // grp, grp, D_MODEL)
            b3, q3, k3 = b.reshape(shp), q.reshape(shp), kk.reshape(shp)
            mid = b3[:, hs - 1:hs, :]
            zq3 = q3[:, hs:, :] * jnp.exp(b3[:, hs:, :] - mid)
            zk3 = k3[:, :hs, :] * jnp.exp(mid - b3[:, :hs, :])
            zq[level] = zq3.reshape(TL // 2, D_MODEL).astype(BF16)
            zk[level] = jnp.concatenate([zk3, zq3], axis=1).reshape(TL, D_MODEL).astype(BF16)
        else:
            shp = (TL // 8, 8, D_MODEL)
            b8, q8, k8 = b.reshape(shp), q.reshape(shp), kk.reshape(shp)
            sub = lax.broadcasted_iota(jnp.int32, shp, 1)
            is_q = (sub & hs) != 0
            if hs == 4:
                mid = jnp.broadcast_to(b8[:, 3:4, :], shp)
            elif hs == 2:
                mid = jnp.where(sub < 4, jnp.broadcast_to(b8[:, 1:2, :], shp),
                                jnp.broadcast_to(b8[:, 5:6, :], shp))
            else:
                mid = None
            if mid is None:
                e = jnp.where(is_q, logf.reshape(shp), 0.0)
            else:
                e = jnp.where(is_q, b8 - mid, mid - b8)
            zfine = jnp.where(is_q, q8, k8) * jnp.exp(e)
            zk[level] = zfine.reshape(TL, D_MODEL).astype(BF16)

    lvl = lvl_ref[...]
    outs = []
    for h in range(HEADS):
        sl = slice(h * DK, (h + 1) * DK)
        vsl = slice(h * DV, (h + 1) * DV)
        st = st_ref[h]
        o = _dot_nt(qs[:, sl], st.astype(BF16))
        att = jnp.where(lvl == 0, _dot_nt(qb[:, sl], kb[:, sl]), 0.0)
        for level in range(N_LEVELS, 0, -1):
            hs = TL >> level
            zkl = zk[level][:, sl]
            if hs >= 8:
                grp = 2 * hs
                pr = _dot_nt(zq[level][:, sl], zkl).reshape(TL // grp, hs, TL)
                att3 = att.reshape(TL // grp, grp, TL)
                hit = lvl.reshape(TL // grp, grp, TL)[:, hs:, :] == level
                upper = jnp.where(hit, pr, att3[:, hs:, :])
                att = jnp.concatenate([att3[:, :hs, :], upper], axis=1).reshape(TL, TL)
            else:
                att = jnp.where(lvl == level, _dot_nt(zkl, zkl), att)
        o = o + _dot(att.astype(BF16), vb[:, vsl])
        st_ref[h] = st * dec[:, sl] + _dot_tn(vb[:, vsl], kd[:, sl])
        outs.append(o)
    o = jnp.concatenate(outs, axis=1)

    branch = _head_rmsnorm(o, gn_ref[...]) * _silu(z)
    y_ref[...] = _mixer_back(x, branch, wout_ref[...], p_ref[...], npl_ref[...],
                             wg_ref[...], wp_ref[...])

    @pl.when(l == pl.num_programs(1) - 1)
    def _():
        for h in range(HEADS):
            sfin_ref[h] = st_ref[h].T


def _gmlp_prompt_kernel(x_ref, p_ref, nm_ref, win_ref, lng_ref, lnb_ref, wsp_ref, bsp_ref,
                        wout_ref, npl_ref, wg_ref, wp_ref, nf_ref, y_ref, vt_ref, *, final_norm):
    l = pl.program_id(1)
    x = x_ref[...]
    u, vn, z = _gmlp_front(x, nm_ref[...], win_ref[...], lng_ref[...], lnb_ref[...])
    vb = vn.astype(BF16)
    r = lax.broadcasted_iota(jnp.int32, (B_CHUNK, B_CHUNK), 0)
    c = lax.broadcasted_iota(jnp.int32, (B_CHUNK, B_CHUNK), 1)
    tril = c <= r
    bias = bsp_ref[...]
    cols = []
    for g in range(GROUPS):
        w = jnp.where(tril, wsp_ref[g], 0.0).astype(BF16)
        gsl = slice(g * DG, (g + 1) * DG)
        chunks = []
        for ci in range(TL // B_CHUNK):
            rsl = slice(ci * B_CHUNK, (ci + 1) * B_CHUNK)
            chunks.append(_dot(w, vb[rsl, gsl]) + bias[:, gsl])
        cols.append(jnp.concatenate(chunks, axis=0))
    s = jnp.concatenate(cols, axis=1)
    branch = u * s * _silu(z)
    h = _mixer_back(x, branch, wout_ref[...], p_ref[...], npl_ref[...], wg_ref[...], wp_ref[...])
    if final_norm:
        h = _rmsnorm(h, nf_ref[...])
    y_ref[...] = h

    @pl.when(l == pl.num_programs(1) - 1)
    def _():
        vt_ref[...] = vn[TL - B_CHUNK:, :]


def _hgrn_sample_front_kernel(x_ref, lbl_ref, nm_ref, win_ref, q_ref, f_ref, k_ref, v_ref, z_ref,
                              *, layer_j):
    lb = _forget_lower_bound(lbl_ref[...], layer_j)
    q, logf, kk, v, z = _hgrn_front(x_ref[...], nm_ref[...], win_ref[...], lb)
    q_ref[...] = q
    f_ref[...] = jnp.exp(logf)
    k_ref[...] = kk
    v_ref[...] = v
    z_ref[...] = z


def _hgrn_sample_state_kernel(q_ref, f_ref, k_ref, v_ref, s_ref, *rest):
    snew_ref, o_ref = rest[-2:]
    for i in range(SAMPLE_TB):
        q_t = q_ref[i].T
        f_t = f_ref[i].T
        k_t = k_ref[i].T
        for h in range(HEADS):
            s_new = f_t[:, h:h + 1] * s_ref[i, h] + k_t[:, h:h + 1] * v_ref[i, h:h + 1, :]
            snew_ref[i, h] = s_new
            o_ref[i, h:h + 1, :] = jnp.sum(q_t[:, h:h + 1] * s_new, axis=0, keepdims=True)


def _hgrn_sample_back_kernel(x_ref, o_ref, z_ref, p_ref, gn_ref, wout_ref, npl_ref, wg_ref,
                             wp_ref, y_ref):
    branch = _head_rmsnorm(o_ref[...], gn_ref[...]) * _silu(z_ref[...])
    y_ref[...] = _mixer_back(x_ref[...], branch, wout_ref[...], p_ref[...], npl_ref[...],
                             wg_ref[...], wp_ref[...])


def _gmlp_sample_kernel(x_ref, p_ref, nm_ref, win_ref, lng_ref, lnb_ref, w00_ref, b0_ref,
                        wout_ref, npl_ref, wg_ref, wp_ref, nf_ref, y_ref, vt_ref, *, final_norm):
    x = x_ref[...]
    u, vn, z = _gmlp_front(x, nm_ref[...], win_ref[...], lng_ref[...], lnb_ref[...])
    s = w00_ref[...] * vn + b0_ref[...]
    branch = u * s * _silu(z)
    h = _mixer_back(x, branch, wout_ref[...], p_ref[...], npl_ref[...], wg_ref[...], wp_ref[...])
    if final_norm:
        h = _rmsnorm(h, nf_ref[...])
    y_ref[...] = h
    vt_ref[...] = vn


def _resident(shape, index):
    return pl.BlockSpec(shape, index, pipeline_mode=pl.Buffered(1))


def _params(*sem):
    return pltpu.CompilerParams(dimension_semantics=sem, vmem_limit_bytes=VMEM_LIMIT)


def _hgrn_prompt_layer(x, p_all, i, j, lb_logits, norm_mix, w_in, gnorm, w_out, norm_ple,
                       w_gate, w_proj, tri, lvl):
    bsz, seq, _ = x.shape
    n_a = lb_logits.shape[0]
    grid = (bsz, seq // TL)
    const2 = lambda b, l: (0, 0)
    in_specs = [
        pl.BlockSpec((None, TL, D_MODEL), lambda b, l: (b, l, 0)),
        pl.BlockSpec((None, None, TL, PLE_DIM), lambda b, l: (i, b, l, 0)),
        _resident((n_a, D_MODEL), const2),
        _resident((None, 1, D_MODEL), lambda b, l: (i, 0, 0)),
        _resident((None, D_MODEL, 3 * D_INNER), lambda b, l: (j, 0, 0)),
        _resident((None, 1, DV), lambda b, l: (j, 0, 0)),
        _resident((None, D_INNER, D_MODEL), lambda b, l: (j, 0, 0)),
        _resident((None, 1, D_MODEL), lambda b, l: (i, 0, 0)),
        _resident((None, D_MODEL, D_MODEL), lambda b, l: (i, 0, 0)),
        _resident((None, PLE_DIM, D_MODEL), lambda b, l: (i, 0, 0)),
        _resident(tri.shape, const2),
        _resident(lvl.shape, const2),
    ]
    out_specs = [
        pl.BlockSpec((None, TL, D_MODEL), lambda b, l: (b, l, 0)),
        pl.BlockSpec((None, HEADS, DK, DV), lambda b, l: (b, 0, 0, 0)),
    ]
    out_shape = [
        jax.ShapeDtypeStruct((bsz, seq, D_MODEL), F32),
        jax.ShapeDtypeStruct((bsz, HEADS, DK, DV), F32),
    ]
    return pl.pallas_call(
        functools.partial(_hgrn_prompt_kernel, layer_j=j),
        grid=grid, in_specs=in_specs, out_specs=out_specs, out_shape=out_shape,
        scratch_shapes=[pltpu.VMEM((HEADS, DV, DK), F32)],
        compiler_params=_params("arbitrary", "arbitrary"),
        name=f"hgrn_prompt_{j}",
    )(x, p_all, lb_logits, norm_mix, w_in, gnorm, w_out, norm_ple, w_gate, w_proj, tri, lvl)


def _gmlp_prompt_layer(x, p_all, i, j, norm_mix, w_in, ln_g, ln_b, w_sp, bias_full, w_out,
                       norm_ple, w_gate, w_proj, norm_final, final_norm):
    bsz, seq, _ = x.shape
    grid = (bsz, seq // TL)
    in_specs = [
        pl.BlockSpec((None, TL, D_MODEL), lambda b, l: (b, l, 0)),
        pl.BlockSpec((None, None, TL, PLE_DIM), lambda b, l: (i, b, l, 0)),
        _resident((None, 1, D_MODEL), lambda b, l: (i, 0, 0)),
        _resident((None, D_MODEL, 3 * D_INNER), lambda b, l: (j, 0, 0)),
        _resident((None, 1, D_INNER), lambda b, l: (j, 0, 0)),
        _resident((None, 1, D_INNER), lambda b, l: (j, 0, 0)),
        _resident((None, GROUPS, B_CHUNK, B_CHUNK), lambda b, l: (j, 0, 0, 0)),
        _resident((None, B_CHUNK, D_INNER), lambda b, l: (j, 0, 0)),
        _resident((None, D_INNER, D_MODEL), lambda b, l: (j, 0, 0)),
        _resident((None, 1, D_MODEL), lambda b, l: (i, 0, 0)),
        _resident((None, D_MODEL, D_MODEL), lambda b, l: (i, 0, 0)),
        _resident((None, PLE_DIM, D_MODEL), lambda b, l: (i, 0, 0)),
        _resident((1, D_MODEL), lambda b, l: (0, 0)),
    ]
    out_specs = [
        pl.BlockSpec((None, TL, D_MODEL), lambda b, l: (b, l, 0)),
        pl.BlockSpec((None, B_CHUNK, D_INNER), lambda b, l: (b, 0, 0)),
    ]
    out_shape = [
        jax.ShapeDtypeStruct((bsz, seq, D_MODEL), F32),
        jax.ShapeDtypeStruct((bsz, B_CHUNK, D_INNER), F32),
    ]
    return pl.pallas_call(
        functools.partial(_gmlp_prompt_kernel, final_norm=final_norm),
        grid=grid, in_specs=in_specs, out_specs=out_specs, out_shape=out_shape,
        compiler_params=_params("arbitrary", "arbitrary"),
        name=f"gmlp_prompt_{j}",
    )(x, p_all, norm_mix, w_in, ln_g, ln_b, w_sp, bias_full, w_out, norm_ple, w_gate, w_proj,
      norm_final)


def _whole(shape):
    nd = len(shape)
    return pl.BlockSpec(shape, lambda *_: (0,) * nd)


def _hgrn_sample_layer(x, p_all, s0, s_prev, i, j, lb_logits, norm_mix, w_in, gnorm, w_out,
                       norm_ple, w_gate, w_proj):
    n = x.shape[0]
    n_a = lb_logits.shape[0]
    q, f, kk, v, z = pl.pallas_call(
        functools.partial(_hgrn_sample_front_kernel, layer_j=j),
        grid=(1,),
        in_specs=[
            _whole((n, D_MODEL)),
            _whole((n_a, D_MODEL)),
            pl.BlockSpec((None, 1, D_MODEL), lambda g: (i, 0, 0)),
            pl.BlockSpec((None, D_MODEL, 3 * D_INNER), lambda g: (j, 0, 0)),
        ],
        out_specs=[_whole((n, D_MODEL))] * 3 + [_whole((n, D_INNER))] * 2,
        out_shape=[jax.ShapeDtypeStruct((n, D_MODEL), F32)] * 3
        + [jax.ShapeDtypeStruct((n, D_INNER), F32)] * 2,
        compiler_params=_params("arbitrary"),
        name=f"hgrn_sample_front_{j}",
    )(x, lb_logits, norm_mix, w_in)

    tb = SAMPLE_TB
    hk = pl.BlockSpec((tb, HEADS, DK), lambda g: (g, 0, 0))
    hv = pl.BlockSpec((tb, HEADS, DV), lambda g: (g, 0, 0))
    st = pl.BlockSpec((None, tb, HEADS, DK, DV), lambda g: (j, g, 0, 0, 0))
    operands = [q.reshape(n, HEADS, DK), f.reshape(n, HEADS, DK), kk.reshape(n, HEADS, DK),
                v.reshape(n, HEADS, DV), s0]
    in_specs = [hk, hk, hk, hv, st]
    aliases = {}
    if s_prev is not None:
        operands.append(s_prev)
        in_specs.append(pl.BlockSpec(memory_space=pl.ANY))
        aliases = {len(operands) - 1: 0}
    s_new, o = pl.pallas_call(
        _hgrn_sample_state_kernel,
        grid=(n // tb,),
        in_specs=in_specs,
        out_specs=[st, hv],
        out_shape=[jax.ShapeDtypeStruct(s0.shape, F32),
                   jax.ShapeDtypeStruct((n, HEADS, DV), F32)],
        input_output_aliases=aliases,
        compiler_params=_params("arbitrary"),
        name=f"hgrn_sample_state_{j}",
    )(*operands)

    y = pl.pallas_call(
        _hgrn_sample_back_kernel,
        grid=(1,),
        in_specs=[
            _whole((n, D_MODEL)),
            _whole((n, D_INNER)),
            _whole((n, D_INNER)),
            pl.BlockSpec((None, n, PLE_DIM), lambda g: (i, 0, 0)),
            pl.BlockSpec((None, 1, DV), lambda g: (j, 0, 0)),
            pl.BlockSpec((None, D_INNER, D_MODEL), lambda g: (j, 0, 0)),
            pl.BlockSpec((None, 1, D_MODEL), lambda g: (i, 0, 0)),
            pl.BlockSpec((None, D_MODEL, D_MODEL), lambda g: (i, 0, 0)),
            pl.BlockSpec((None, PLE_DIM, D_MODEL), lambda g: (i, 0, 0)),
        ],
        out_specs=_whole((n, D_MODEL)),
        out_shape=jax.ShapeDtypeStruct((n, D_MODEL), F32),
        compiler_params=_params("arbitrary"),
        name=f"hgrn_sample_back_{j}",
    )(x, o.reshape(n, D_INNER), z, p_all, gnorm, w_out, norm_ple, w_gate, w_proj)
    return y, s_new


def _gmlp_sample_layer(x, p_all, i, j, norm_mix, w_in, ln_g, ln_b, w00, b0, w_out, norm_ple,
                       w_gate, w_proj, norm_final, final_norm):
    n = x.shape[0]
    return pl.pallas_call(
        functools.partial(_gmlp_sample_kernel, final_norm=final_norm),
        grid=(1,),
        in_specs=[
            _whole((n, D_MODEL)),
            pl.BlockSpec((None, n, PLE_DIM), lambda g: (i, 0, 0)),
            pl.BlockSpec((None, 1, D_MODEL), lambda g: (i, 0, 0)),
            pl.BlockSpec((None, D_MODEL, 3 * D_INNER), lambda g: (j, 0, 0)),
            pl.BlockSpec((None, 1, D_INNER), lambda g: (j, 0, 0)),
            pl.BlockSpec((None, 1, D_INNER), lambda g: (j, 0, 0)),
            pl.BlockSpec((None, 1, D_INNER), lambda g: (j, 0, 0)),
            pl.BlockSpec((None, 1, D_INNER), lambda g: (j, 0, 0)),
            pl.BlockSpec((None, D_INNER, D_MODEL), lambda g: (j, 0, 0)),
            pl.BlockSpec((None, 1, D_MODEL), lambda g: (i, 0, 0)),
            pl.BlockSpec((None, D_MODEL, D_MODEL), lambda g: (i, 0, 0)),
            pl.BlockSpec((None, PLE_DIM, D_MODEL), lambda g: (i, 0, 0)),
            _whole((1, D_MODEL)),
        ],
        out_specs=[_whole((n, D_MODEL)), _whole((n, D_INNER))],
        out_shape=[jax.ShapeDtypeStruct((n, D_MODEL), F32),
                   jax.ShapeDtypeStruct((n, D_INNER), F32)],
        compiler_params=_params("arbitrary"),
        name=f"gmlp_sample_{j}",
    )(x, p_all, norm_mix, w_in, ln_g, ln_b, w00, b0, w_out, norm_ple, w_gate, w_proj, norm_final)


def kernel(x_prompt, x_sample, state_hgrn, p_prompt, p_sample, norm_mix, w_in_a, lb_logits,
           gnorm_a, w_out_a, w_in_b, ln_v_g, ln_v_b, w_spatial, b_spatial, w_out_b,
           norm_ple, w_ple_gate, w_ple_proj, norm_final):
    depth = norm_mix.shape[0]
    n_s = x_sample.shape[0]
    assert x_prompt.shape[1] % TL == 0 and TL % B_CHUNK == 0 and x_sample.shape[1] == 1
    assert n_s % SAMPLE_TB == 0

    w_in_a_b = w_in_a.astype(BF16)
    w_out_a_b = w_out_a.astype(BF16)
    w_in_b_b = w_in_b.astype(BF16)
    w_out_b_b = w_out_b.astype(BF16)
    w_gate_b = w_ple_gate.astype(BF16)
    w_proj_b = w_ple_proj.astype(BF16)

    tri = jnp.asarray(_TRI_NP, dtype=BF16)
    lvl = jnp.asarray(_LVL_NP)
    nf = norm_final.reshape(1, D_MODEL)
    bias_full = jnp.repeat(jnp.swapaxes(b_spatial, 1, 2), DG, axis=2)
    w00 = jnp.repeat(w_spatial[:, :, 0, 0], DG, axis=1)[:, None, :]
    b0 = jnp.repeat(b_spatial[:, :, 0], DG, axis=1)[:, None, :]
    nmix = norm_mix[:, None, :]
    nple = norm_ple[:, None, :]
    gn = gnorm_a[:, None, :]
    lng = ln_v_g[:, None, :]
    lnb = ln_v_b[:, None, :]

    if depth % 2 == 1:
        raise NotImplementedError("final norm is fused into the last (chunk-MLP) layer")
    hp = x_prompt
    hs = x_sample.reshape(n_s, D_MODEL)
    ps = p_sample.reshape(depth, n_s, PLE_DIM)
    st_p, vt_p, vt_s = [], [], []
    st_s = None
    for i in range(depth):
        j = i // 2
        last = i == depth - 1
        if i % 2 == 0:
            hp, s_fin = _hgrn_prompt_layer(hp, p_prompt, i, j, lb_logits, nmix, w_in_a_b,
                                           gn, w_out_a_b, nple, w_gate_b, w_proj_b,
                                           tri, lvl)
            st_p.append(s_fin)
            hs, st_s = _hgrn_sample_layer(hs, ps, state_hgrn, st_s, i, j, lb_logits, nmix,
                                          w_in_a_b, gn, w_out_a_b, nple, w_gate_b, w_proj_b)
        else:
            hp, vt = _gmlp_prompt_layer(hp, p_prompt, i, j, nmix, w_in_b_b, lng, lnb,
                                        w_spatial, bias_full, w_out_b_b, nple, w_gate_b,
                                        w_proj_b, nf, last)
            vt_p.append(vt)
            hs, vts = _gmlp_sample_layer(hs, ps, i, j, nmix, w_in_b_b, lng, lnb,
                                         w00, b0, w_out_b_b, nple, w_gate_b, w_proj_b, nf, last)
            vt_s.append(vts.reshape(n_s, 1, D_INNER))
    return (hp, hs.reshape(n_s, 1, D_MODEL), jnp.stack(st_p), st_s,
            jnp.stack(vt_p), jnp.stack(vt_s))
```

```python
import functools
import math

import numpy as np
import jax
import jax.numpy as jnp
from jax import lax
from jax.experimental import pallas as pl
from jax.experimental.pallas import tpu as pltpu

D_MODEL = 1024
D_INNER = 2048
HEADS = 8
DK = 128
DV = 256
GROUPS = 8
DG = 256
B_CHUNK = 128
PLE_DIM = 256
EPS = 1e-6
LOG2_E = 1.4426950408889634

TL = 256
HALF = TL // 2
N_LEVELS = int(math.log2(TL))
SAMPLE_TB = 4
WEIGHT_CHUNK_3 = 1536
WEIGHT_CHUNK_1 = 512
VMEM_LIMIT = 58 * 1024 * 1024

F32 = jnp.float32
BF16 = jnp.bfloat16


def _build_level_consts(tl):
    t = np.arange(tl)[:, None]
    s = np.arange(tl)[None, :]
    tri = (s <= t).astype(np.float32)
    x = np.maximum(t ^ s, 1)
    p = np.floor(np.log2(x)).astype(np.int32)
    nlev = int(math.log2(tl))
    lvl = np.where(t == s, 0, np.where(t > s, nlev - p, -1)).astype(np.int32)
    return tri, lvl[:tl // 2, :tl // 2]


_TRI_NP, _LVL_NP = _build_level_consts(TL)


def _dot(a, b):
    return jnp.dot(a, b, preferred_element_type=F32)


def _dot_cols(a, w_ref):
    return jnp.concatenate([_dot(a, w_ref[c]) for c in range(w_ref.shape[0])], axis=1)


def _dot_nt(a, b):
    return lax.dot_general(a, b, (((1,), (1,)), ((), ())), preferred_element_type=F32)


def _dot_tn(a, b):
    return lax.dot_general(a, b, (((0,), (0,)), ((), ())), preferred_element_type=F32)


def _rmsnorm(x, g):
    ms = jnp.mean(x * x, axis=-1, keepdims=True)
    return x * lax.rsqrt(ms + EPS) * g


def _silu(x):
    return x * jax.nn.sigmoid(x)


def _gelu_tanh(x):
    c = math.sqrt(2.0 / math.pi)
    return 0.5 * x * (1.0 + jnp.tanh(c * (x + 0.044715 * (x * x * x))))


def _forget_lower_bound(lbl, j):
    mx = jnp.max(lbl, axis=0, keepdims=True)
    e = jnp.exp(lbl - mx)
    sm = e / jnp.sum(e, axis=0, keepdims=True)
    cum0 = sm[0:1, :]
    cum = cum0
    for i in range(1, j + 1):
        cum = cum + sm[i:i + 1, :]
    return cum - cum0


def _forget_gates(fpre, lb):
    e = jnp.exp(-jnp.abs(fpre))
    log_sig = jnp.minimum(fpre, 0.0) - jnp.log1p(e)
    a = jnp.log(lb)
    y = jnp.log1p(-lb) + log_sig
    logf = jnp.maximum(a, y) + jnp.log1p(jnp.exp(-jnp.abs(a - y)))
    one_minus_f = (1.0 - lb) * (jnp.where(fpre >= 0.0, e, 1.0) / (1.0 + e))
    return logf, one_minus_f


def _hgrn_front(x, nm, w_in, lb):
    hn = _rmsnorm(x, nm).astype(BF16)
    proj = _dot_cols(hn, w_in)
    q = _silu(proj[:, :D_MODEL])
    logf, kk = _forget_gates(proj[:, D_MODEL:2 * D_MODEL], lb)
    v = proj[:, 2 * D_MODEL:2 * D_MODEL + D_INNER]
    z = proj[:, 2 * D_MODEL + D_INNER:]
    return q, logf, kk, v, z


def _head_rmsnorm(o, gn):
    outs = []
    for h in range(HEADS):
        outs.append(_rmsnorm(o[:, h * DV:(h + 1) * DV], gn))
    return jnp.concatenate(outs, axis=1)


def _ple(h, p, npl, w_gate, w_proj):
    gate = jax.nn.sigmoid(_dot_cols(_rmsnorm(h, npl).astype(BF16), w_gate))
    return h + gate * _dot_cols(p.astype(BF16), w_proj)


def _mixer_back(x, branch, w_out, p, npl, w_gate, w_proj):
    h = x + _dot_cols(branch.astype(BF16), w_out)
    return _ple(h, p, npl, w_gate, w_proj)


def _gmlp_front(x, nm, w_in, ln_g, ln_b):
    hn = _rmsnorm(x, nm).astype(BF16)
    proj = _dot_cols(hn, w_in)
    u = _gelu_tanh(proj[:, :D_INNER])
    vg = _gelu_tanh(proj[:, D_INNER:2 * D_INNER])
    z = proj[:, 2 * D_INNER:]
    mu = jnp.mean(vg, axis=-1, keepdims=True)
    vc = vg - mu
    vn = vc * lax.rsqrt(jnp.mean(vc * vc, axis=-1, keepdims=True) + EPS) * ln_g + ln_b
    return u, vn, z


def _hgrn_prompt_kernel(x_ref, p_ref, lbl_ref, nm_ref, win_ref, gn_ref, wout_ref, npl_ref,
                        wg_ref, wp_ref, tri_ref, lvl_ref, y_ref, sfin_ref, st_ref, *, layer_j):
    l = pl.program_id(1)

    @pl.when(l == 0)
    def _():
        st_ref[...] = jnp.zeros_like(st_ref)

    x = x_ref[...]
    lb = _forget_lower_bound(lbl_ref[...], layer_j)
    q, logf, kk, v, z = _hgrn_front(x, nm_ref[...], win_ref, lb)

    lg = logf * LOG2_E
    g_hi = lg.astype(BF16)
    g_lo = (lg - g_hi.astype(F32)).astype(BF16)
    g2 = jnp.concatenate([g_hi, g_lo], axis=1)
    b2 = _dot(tri_ref[...], g2)
    b = b2[:, :D_MODEL] + b2[:, D_MODEL:]
    b_last = b[TL - 1:TL, :]
    qs = (q * jnp.exp2(b)).astype(BF16)
    kd = (kk * jnp.exp2(b_last - b)).astype(BF16)
    dec = jnp.exp2(b_last)
    qb = q.astype(BF16)
    kb = kk.astype(BF16)
    vb = v.astype(BF16)

    zk, zq = {}, {}
    for level in range(1, N_LEVELS + 1):
        hs = TL >> level
        if hs >= 8:
            grp = 2 * hs
            shp = (TL // grp, grp, D_MODEL)
            b3, q3, k3 = b.reshape(shp), q.reshape(shp), kk.reshape(shp)
            mid = b3[:, hs - 1:hs, :]
            zq3 = q3[:, hs:, :] * jnp.exp2(b3[:, hs:, :] - mid)
            zk3 = k3[:, :hs, :] * jnp.exp2(mid - b3[:, :hs, :])
            zq[level] = zq3.reshape(TL // 2, D_MODEL).astype(BF16)
            zk[level] = jnp.concatenate([zk3, zq3], axis=1).reshape(TL, D_MODEL).astype(BF16)
        else:
            shp = (TL // 8, 8, D_MODEL)
            b8, q8, k8 = b.reshape(shp), q.reshape(shp), kk.reshape(shp)
            sub = lax.broadcasted_iota(jnp.int32, shp, 1)
            is_q = (sub & hs) != 0
            if hs == 4:
                mid = jnp.broadcast_to(b8[:, 3:4, :], shp)
            elif hs == 2:
                mid = jnp.where(sub < 4, jnp.broadcast_to(b8[:, 1:2, :], shp),
                                jnp.broadcast_to(b8[:, 5:6, :], shp))
            else:
                mid = None
            if mid is None:
                e = jnp.where(is_q, lg.reshape(shp), 0.0)
            else:
                e = jnp.where(is_q, b8 - mid, mid - b8)
            zfine = jnp.where(is_q, q8, k8) * jnp.exp2(e)
            zk[level] = zfine.reshape(TL, D_MODEL).astype(BF16)

    lvl = lvl_ref[...]
    outs = []
    for h in range(HEADS):
        sl = slice(h * DK, (h + 1) * DK)
        vsl = slice(h * DV, (h + 1) * DV)
        st = st_ref[h]
        o = _dot_nt(qs[:, sl], st.astype(BF16))
        pd = _dot_nt(qb[:, sl], kb[:, sl])
        diag = [jnp.where(lvl == 0, pd[:HALF, :HALF], 0.0),
                jnp.where(lvl == 0, pd[HALF:, HALF:], 0.0)]
        for level in range(N_LEVELS, 1, -1):
            hs = TL >> level
            zkl = zk[level][:, sl]
            if hs >= 8:
                grp = 2 * hs
                nq = HALF // 2
                pr = _dot_nt(zq[level][:, sl], zkl)
                hit = lvl.reshape(HALF // grp, grp, HALF)[:, hs:, :] == level
                for i in range(2):
                    pq = pr[i * nq:(i + 1) * nq, i * HALF:(i + 1) * HALF]
                    d3 = diag[i].reshape(HALF // grp, grp, HALF)
                    upper = jnp.where(hit, pq.reshape(HALF // grp, hs, HALF), d3[:, hs:, :])
                    diag[i] = jnp.concatenate([d3[:, :hs, :], upper], axis=1).reshape(HALF, HALF)
            else:
                pf = _dot_nt(zkl, zkl)
                hit = lvl == level
                diag[0] = jnp.where(hit, pf[:HALF, :HALF], diag[0])
                diag[1] = jnp.where(hit, pf[HALF:, HALF:], diag[1])
        a10 = _dot_nt(zq[1][:, sl], zk[1][:HALF, sl])
        att = jnp.concatenate(
            [jnp.concatenate([diag[0], jnp.zeros((HALF, HALF), F32)], axis=1),
             jnp.concatenate([a10, diag[1]], axis=1)], axis=0)
        o = o + _dot(att.astype(BF16), vb[:, vsl])
        st_ref[h] = st * dec[:, sl] + _dot_tn(vb[:, vsl], kd[:, sl])
        outs.append(o)
    o = jnp.concatenate(outs, axis=1)

    branch = _head_rmsnorm(o, gn_ref[...]) * _silu(z)
    y_ref[...] = _mixer_back(x, branch, wout_ref, p_ref[...], npl_ref[...],
                             wg_ref, wp_ref)

    @pl.when(l == pl.num_programs(1) - 1)
    def _():
        for h in range(HEADS):
            sfin_ref[h] = st_ref[h].T


def _gmlp_prompt_kernel(x_ref, p_ref, nm_ref, win_ref, lng_ref, lnb_ref, wsp_ref, bsp_ref,
                        wout_ref, npl_ref, wg_ref, wp_ref, nf_ref, y_ref, vt_ref, *, final_norm):
    l = pl.program_id(1)
    x = x_ref[...]
    u, vn, z = _gmlp_front(x, nm_ref[...], win_ref, lng_ref[...], lnb_ref[...])
    vb = vn.astype(BF16)
    r = lax.broadcasted_iota(jnp.int32, (B_CHUNK, B_CHUNK), 0)
    c = lax.broadcasted_iota(jnp.int32, (B_CHUNK, B_CHUNK), 1)
    tril = c <= r
    bias = bsp_ref[...]
    cols = []
    for g in range(GROUPS):
        w = jnp.where(tril, wsp_ref[g], 0.0).astype(BF16)
        gsl = slice(g * DG, (g + 1) * DG)
        chunks = []
        for ci in range(TL // B_CHUNK):
            rsl = slice(ci * B_CHUNK, (ci + 1) * B_CHUNK)
            chunks.append(_dot(w, vb[rsl, gsl]) + bias[:, gsl])
        cols.append(jnp.concatenate(chunks, axis=0))
    s = jnp.concatenate(cols, axis=1)
    branch = u * s * _silu(z)
    h = _mixer_back(x, branch, wout_ref, p_ref[...], npl_ref[...], wg_ref, wp_ref)
    if final_norm:
        h = _rmsnorm(h, nf_ref[...])
    y_ref[...] = h

    @pl.when(l == pl.num_programs(1) - 1)
    def _():
        vt_ref[...] = vn[TL - B_CHUNK:, :]


def _hgrn_sample_front_kernel(x_ref, lbl_ref, nm_ref, win_ref, q_ref, f_ref, k_ref, v_ref, z_ref,
                              *, layer_j):
    lb = _forget_lower_bound(lbl_ref[...], layer_j)
    q, logf, kk, v, z = _hgrn_front(x_ref[...], nm_ref[...], win_ref, lb)
    q_ref[...] = q
    f_ref[...] = jnp.exp(logf)
    k_ref[...] = kk
    v_ref[...] = v
    z_ref[...] = z


def _hgrn_sample_state_kernel(q_ref, f_ref, k_ref, v_ref, s_ref, *rest):
    snew_ref, o_ref = rest[-2:]
    for i in range(SAMPLE_TB):
        q_t = q_ref[i].T
        f_t = f_ref[i].T
        k_t = k_ref[i].T
        for h in range(HEADS):
            s_new = f_t[:, h:h + 1] * s_ref[i, h] + k_t[:, h:h + 1] * v_ref[i, h:h + 1, :]
            snew_ref[i, h] = s_new
            o_ref[i, h:h + 1, :] = jnp.sum(q_t[:, h:h + 1] * s_new, axis=0, keepdims=True)


def _hgrn_sample_back_kernel(x_ref, o_ref, z_ref, p_ref, gn_ref, wout_ref, npl_ref, wg_ref,
                             wp_ref, y_ref):
    branch = _head_rmsnorm(o_ref[...], gn_ref[...]) * _silu(z_ref[...])
    y_ref[...] = _mixer_back(x_ref[...], branch, wout_ref, p_ref[...], npl_ref[...],
                             wg_ref, wp_ref)


def _gmlp_sample_kernel(x_ref, p_ref, nm_ref, win_ref, lng_ref, lnb_ref, w00_ref, b0_ref,
                        wout_ref, npl_ref, wg_ref, wp_ref, nf_ref, y_ref, vt_ref, *, final_norm):
    x = x_ref[...]
    u, vn, z = _gmlp_front(x, nm_ref[...], win_ref, lng_ref[...], lnb_ref[...])
    s = w00_ref[...] * vn + b0_ref[...]
    branch = u * s * _silu(z)
    h = _mixer_back(x, branch, wout_ref, p_ref[...], npl_ref[...], wg_ref, wp_ref)
    if final_norm:
        h = _rmsnorm(h, nf_ref[...])
    y_ref[...] = h
    vt_ref[...] = vn


def _resident(shape, index):
    return pl.BlockSpec(shape, index, pipeline_mode=pl.Buffered(1))


def _params(*sem):
    return pltpu.CompilerParams(dimension_semantics=sem, vmem_limit_bytes=VMEM_LIMIT)


def _hgrn_prompt_layer(x, p_all, i, j, lb_logits, norm_mix, w_in, gnorm, w_out, norm_ple,
                       w_gate, w_proj, tri, lvl):
    bsz, seq, _ = x.shape
    n_a = lb_logits.shape[0]
    grid = (bsz, seq // TL)
    const2 = lambda b, l: (0, 0)
    in_specs = [
        pl.BlockSpec((None, TL, D_MODEL), lambda b, l: (b, l, 0)),
        pl.BlockSpec((None, None, TL, PLE_DIM), lambda b, l: (i, b, l, 0)),
        _resident((n_a, D_MODEL), const2),
        _resident((None, 1, D_MODEL), lambda b, l: (i, 0, 0)),
        _resident((None,) + w_in.shape[1:], lambda b, l: (j, 0, 0, 0)),
        _resident((None, 1, DV), lambda b, l: (j, 0, 0)),
        _resident((None,) + w_out.shape[1:], lambda b, l: (j, 0, 0, 0)),
        _resident((None, 1, D_MODEL), lambda b, l: (i, 0, 0)),
        _resident((None,) + w_gate.shape[1:], lambda b, l: (i, 0, 0, 0)),
        _resident((None,) + w_proj.shape[1:], lambda b, l: (i, 0, 0, 0)),
        _resident(tri.shape, const2),
        _resident(lvl.shape, const2),
    ]
    out_specs = [
        pl.BlockSpec((None, TL, D_MODEL), lambda b, l: (b, l, 0)),
        pl.BlockSpec((None, HEADS, DK, DV), lambda b, l: (b, 0, 0, 0)),
    ]
    out_shape = [
        jax.ShapeDtypeStruct((bsz, seq, D_MODEL), F32),
        jax.ShapeDtypeStruct((bsz, HEADS, DK, DV), F32),
    ]
    return pl.pallas_call(
        functools.partial(_hgrn_prompt_kernel, layer_j=j),
        grid=grid, in_specs=in_specs, out_specs=out_specs, out_shape=out_shape,
        scratch_shapes=[pltpu.VMEM((HEADS, DV, DK), F32)],
        compiler_params=_params("arbitrary", "arbitrary"),
        name=f"hgrn_prompt_{j}",
    )(x, p_all, lb_logits, norm_mix, w_in, gnorm, w_out, norm_ple, w_gate, w_proj, tri, lvl)


def _gmlp_prompt_layer(x, p_all, i, j, norm_mix, w_in, ln_g, ln_b, w_sp, bias_full, w_out,
                       norm_ple, w_gate, w_proj, norm_final, final_norm):
    bsz, seq, _ = x.shape
    grid = (bsz, seq // TL)
    in_specs = [
        pl.BlockSpec((None, TL, D_MODEL), lambda b, l: (b, l, 0)),
        pl.BlockSpec((None, None, TL, PLE_DIM), lambda b, l: (i, b, l, 0)),
        _resident((None, 1, D_MODEL), lambda b, l: (i, 0, 0)),
        _resident((None,) + w_in.shape[1:], lambda b, l: (j, 0, 0, 0)),
        _resident((None, 1, D_INNER), lambda b, l: (j, 0, 0)),
        _resident((None, 1, D_INNER), lambda b, l: (j, 0, 0)),
        _resident((None, GROUPS, B_CHUNK, B_CHUNK), lambda b, l: (j, 0, 0, 0)),
        _resident((None, B_CHUNK, D_INNER), lambda b, l: (j, 0, 0)),
        _resident((None,) + w_out.shape[1:], lambda b, l: (j, 0, 0, 0)),
        _resident((None, 1, D_MODEL), lambda b, l: (i, 0, 0)),
        _resident((None,) + w_gate.shape[1:], lambda b, l: (i, 0, 0, 0)),
        _resident((None,) + w_proj.shape[1:], lambda b, l: (i, 0, 0, 0)),
        _resident((1, D_MODEL), lambda b, l: (0, 0)),
    ]
    out_specs = [
        pl.BlockSpec((None, TL, D_MODEL), lambda b, l: (b, l, 0)),
        pl.BlockSpec((None, B_CHUNK, D_INNER), lambda b, l: (b, 0, 0)),
    ]
    out_shape = [
        jax.ShapeDtypeStruct((bsz, seq, D_MODEL), F32),
        jax.ShapeDtypeStruct((bsz, B_CHUNK, D_INNER), F32),
    ]
    return pl.pallas_call(
        functools.partial(_gmlp_prompt_kernel, final_norm=final_norm),
        grid=grid, in_specs=in_specs, out_specs=out_specs, out_shape=out_shape,
        compiler_params=_params("arbitrary", "arbitrary"),
        name=f"gmlp_prompt_{j}",
    )(x, p_all, norm_mix, w_in, ln_g, ln_b, w_sp, bias_full, w_out, norm_ple, w_gate, w_proj,
      norm_final)


def _whole(shape):
    nd = len(shape)
    return pl.BlockSpec(shape, lambda *_: (0,) * nd)


def _hgrn_sample_layer(x, p_all, s0, s_prev, i, j, lb_logits, norm_mix, w_in, gnorm, w_out,
                       norm_ple, w_gate, w_proj):
    n = x.shape[0]
    n_a = lb_logits.shape[0]
    q, f, kk, v, z = pl.pallas_call(
        functools.partial(_hgrn_sample_front_kernel, layer_j=j),
        grid=(1,),
        in_specs=[
            _whole((n, D_MODEL)),
            _whole((n_a, D_MODEL)),
            pl.BlockSpec((None, 1, D_MODEL), lambda g: (i, 0, 0)),
            pl.BlockSpec((None,) + w_in.shape[1:], lambda g: (j, 0, 0, 0)),
        ],
        out_specs=[_whole((n, D_MODEL))] * 3 + [_whole((n, D_INNER))] * 2,
        out_shape=[jax.ShapeDtypeStruct((n, D_MODEL), F32)] * 3
        + [jax.ShapeDtypeStruct((n, D_INNER), F32)] * 2,
        compiler_params=_params("arbitrary"),
        name=f"hgrn_sample_front_{j}",
    )(x, lb_logits, norm_mix, w_in)

    tb = SAMPLE_TB
    hk = pl.BlockSpec((tb, HEADS, DK), lambda g: (g, 0, 0))
    hv = pl.BlockSpec((tb, HEADS, DV), lambda g: (g, 0, 0))
    st = pl.BlockSpec((None, tb, HEADS, DK, DV), lambda g: (j, g, 0, 0, 0))
    operands = [q.reshape(n, HEADS, DK), f.reshape(n, HEADS, DK), kk.reshape(n, HEADS, DK),
                v.reshape(n, HEADS, DV), s0]
    in_specs = [hk, hk, hk, hv, st]
    aliases = {}
    if s_prev is not None:
        operands.append(s_prev)
        in_specs.append(pl.BlockSpec(memory_space=pl.ANY))
        aliases = {len(operands) - 1: 0}
    s_new, o = pl.pallas_call(
        _hgrn_sample_state_kernel,
        grid=(n // tb,),
        in_specs=in_specs,
        out_specs=[st, hv],
        out_shape=[jax.ShapeDtypeStruct(s0.shape, F32),
                   jax.ShapeDtypeStruct((n, HEADS, DV), F32)],
        input_output_aliases=aliases,
        compiler_params=_params("arbitrary"),
        name=f"hgrn_sample_state_{j}",
    )(*operands)

    y = pl.pallas_call(
        _hgrn_sample_back_kernel,
        grid=(1,),
        in_specs=[
            _whole((n, D_MODEL)),
            _whole((n, D_INNER)),
            _whole((n, D_INNER)),
            pl.BlockSpec((None, n, PLE_DIM), lambda g: (i, 0, 0)),
            pl.BlockSpec((None, 1, DV), lambda g: (j, 0, 0)),
            pl.BlockSpec((None,) + w_out.shape[1:], lambda g: (j, 0, 0, 0)),
            pl.BlockSpec((None, 1, D_MODEL), lambda g: (i, 0, 0)),
            pl.BlockSpec((None,) + w_gate.shape[1:], lambda g: (i, 0, 0, 0)),
            pl.BlockSpec((None,) + w_proj.shape[1:], lambda g: (i, 0, 0, 0)),
        ],
        out_specs=_whole((n, D_MODEL)),
        out_shape=jax.ShapeDtypeStruct((n, D_MODEL), F32),
        compiler_params=_params("arbitrary"),
        name=f"hgrn_sample_back_{j}",
    )(x, o.reshape(n, D_INNER), z, p_all, gnorm, w_out, norm_ple, w_gate, w_proj)
    return y, s_new


def _gmlp_sample_layer(x, p_all, i, j, norm_mix, w_in, ln_g, ln_b, w00, b0, w_out, norm_ple,
                       w_gate, w_proj, norm_final, final_norm):
    n = x.shape[0]
    return pl.pallas_call(
        functools.partial(_gmlp_sample_kernel, final_norm=final_norm),
        grid=(1,),
        in_specs=[
            _whole((n, D_MODEL)),
            pl.BlockSpec((None, n, PLE_DIM), lambda g: (i, 0, 0)),
            pl.BlockSpec((None, 1, D_MODEL), lambda g: (i, 0, 0)),
            pl.BlockSpec((None,) + w_in.shape[1:], lambda g: (j, 0, 0, 0)),
            pl.BlockSpec((None, 1, D_INNER), lambda g: (j, 0, 0)),
            pl.BlockSpec((None, 1, D_INNER), lambda g: (j, 0, 0)),
            pl.BlockSpec((None, 1, D_INNER), lambda g: (j, 0, 0)),
            pl.BlockSpec((None, 1, D_INNER), lambda g: (j, 0, 0)),
            pl.BlockSpec((None,) + w_out.shape[1:], lambda g: (j, 0, 0, 0)),
            pl.BlockSpec((None, 1, D_MODEL), lambda g: (i, 0, 0)),
            pl.BlockSpec((None,) + w_gate.shape[1:], lambda g: (i, 0, 0, 0)),
            pl.BlockSpec((None,) + w_proj.shape[1:], lambda g: (i, 0, 0, 0)),
            _whole((1, D_MODEL)),
        ],
        out_specs=[_whole((n, D_MODEL)), _whole((n, D_INNER))],
        out_shape=[jax.ShapeDtypeStruct((n, D_MODEL), F32),
                   jax.ShapeDtypeStruct((n, D_INNER), F32)],
        compiler_params=_params("arbitrary"),
        name=f"gmlp_sample_{j}",
    )(x, p_all, norm_mix, w_in, ln_g, ln_b, w00, b0, w_out, norm_ple, w_gate, w_proj, norm_final)


def _as_col_chunks(w):
    layers, k, n = w.shape
    width = WEIGHT_CHUNK_3 if n % WEIGHT_CHUNK_3 == 0 else WEIGHT_CHUNK_1
    assert n % width == 0 and width % 1024 != 0
    return jnp.swapaxes(w.astype(BF16).reshape(layers, k, n // width, width), 1, 2)


def kernel(x_prompt, x_sample, state_hgrn, p_prompt, p_sample, norm_mix, w_in_a, lb_logits,
           gnorm_a, w_out_a, w_in_b, ln_v_g, ln_v_b, w_spatial, b_spatial, w_out_b,
           norm_ple, w_ple_gate, w_ple_proj, norm_final):
    depth = norm_mix.shape[0]
    n_s = x_sample.shape[0]
    assert x_prompt.shape[1] % TL == 0 and TL % B_CHUNK == 0 and x_sample.shape[1] == 1
    assert n_s % SAMPLE_TB == 0

    w_in_a_b = _as_col_chunks(w_in_a)
    w_out_a_b = _as_col_chunks(w_out_a)
    w_in_b_b = _as_col_chunks(w_in_b)
    w_out_b_b = _as_col_chunks(w_out_b)
    w_gate_b = _as_col_chunks(w_ple_gate)
    w_proj_b = _as_col_chunks(w_ple_proj)

    tri = jnp.asarray(_TRI_NP, dtype=BF16)
    lvl = jnp.asarray(_LVL_NP)
    nf = norm_final.reshape(1, D_MODEL)
    bias_full = jnp.repeat(jnp.swapaxes(b_spatial, 1, 2), DG, axis=2)
    w00 = jnp.repeat(w_spatial[:, :, 0, 0], DG, axis=1)[:, None, :]
    b0 = jnp.repeat(b_spatial[:, :, 0], DG, axis=1)[:, None, :]
    nmix = norm_mix[:, None, :]
    nple = norm_ple[:, None, :]
    gn = gnorm_a[:, None, :]
    lng = ln_v_g[:, None, :]
    lnb = ln_v_b[:, None, :]

    if depth % 2 == 1:
        raise NotImplementedError("final norm is fused into the last (chunk-MLP) layer")
    hp = x_prompt
    hs = x_sample.reshape(n_s, D_MODEL)
    ps = p_sample.reshape(depth, n_s, PLE_DIM)
    st_p, vt_p, vt_s = [], [], []
    st_s = None
    for i in range(depth):
        j = i // 2
        last = i == depth - 1
        if i % 2 == 0:
            hp, s_fin = _hgrn_prompt_layer(hp, p_prompt, i, j, lb_logits, nmix, w_in_a_b,
                                           gn, w_out_a_b, nple, w_gate_b, w_proj_b,
                                           tri, lvl)
            st_p.append(s_fin)
            hs, st_s = _hgrn_sample_layer(hs, ps, state_hgrn, st_s, i, j, lb_logits, nmix,
                                          w_in_a_b, gn, w_out_a_b, nple, w_gate_b, w_proj_b)
        else:
            hp, vt = _gmlp_prompt_layer(hp, p_prompt, i, j, nmix, w_in_b_b, lng, lnb,
                                        w_spatial, bias_full, w_out_b_b, nple, w_gate_b,
                                        w_proj_b, nf, last)
            vt_p.append(vt)
            hs, vts = _gmlp_sample_layer(hs, ps, i, j, nmix, w_in_b_b, lng, lnb,
                                         w00, b0, w_out_b_b, nple, w_gate_b, w_proj_b, nf, last)
            vt_s.append(vts.reshape(n_s, 1, D_INNER))
    return (hp, hs.reshape(n_s, 1, D_MODEL), jnp.stack(st_p), st_s,
            jnp.stack(vt_p), jnp.stack(vt_s))
```

```python
import functools
import math

import numpy as np
import jax
import jax.numpy as jnp
from jax import lax
from jax.experimental import pallas as pl
from jax.experimental.pallas import tpu as pltpu

D_MODEL = 1024
D_INNER = 2048
HEADS = 8
DK = 128
DV = 256
GROUPS = 8
DG = 256
B_CHUNK = 128
PLE_DIM = 256
EPS = 1e-6
LOG2_E = 1.4426950408889634

TL = 256
HALF = TL // 2
N_LEVELS = int(math.log2(TL))
SAMPLE_TB = 4
WEIGHT_CHUNK_3 = 1536
WEIGHT_CHUNK_1 = 512
VMEM_LIMIT = 58 * 1024 * 1024

F32 = jnp.float32
BF16 = jnp.bfloat16


def _build_level_consts(tl):
    t = np.arange(tl)[:, None]
    s = np.arange(tl)[None, :]
    tri = (s <= t).astype(np.float32)
    x = np.maximum(t ^ s, 1)
    p = np.floor(np.log2(x)).astype(np.int32)
    nlev = int(math.log2(tl))
    lvl = np.where(t == s, 0, np.where(t > s, nlev - p, -1)).astype(np.int32)
    return tri, lvl[:tl // 2, :tl // 2]


_TRI_NP, _LVL_NP = _build_level_consts(TL)


def _dot(a, b):
    return jnp.dot(a, b, preferred_element_type=F32)


def _dot_cols(a, w_refs):
    return jnp.concatenate([_dot(a, w[...]) for w in w_refs], axis=1)


def _dot_nt(a, b):
    return lax.dot_general(a, b, (((1,), (1,)), ((), ())), preferred_element_type=F32)


def _dot_tn(a, b):
    return lax.dot_general(a, b, (((0,), (0,)), ((), ())), preferred_element_type=F32)


def _rmsnorm(x, g):
    ms = jnp.mean(x * x, axis=-1, keepdims=True)
    return x * lax.rsqrt(ms + EPS) * g


def _silu(x):
    return x * jax.nn.sigmoid(x)


def _gelu_tanh(x):
    c = math.sqrt(2.0 / math.pi)
    return 0.5 * x * (1.0 + jnp.tanh(c * (x + 0.044715 * (x * x * x))))


def _forget_lower_bound(lbl, j):
    mx = jnp.max(lbl, axis=0, keepdims=True)
    e = jnp.exp(lbl - mx)
    sm = e / jnp.sum(e, axis=0, keepdims=True)
    cum0 = sm[0:1, :]
    cum = cum0
    for i in range(1, j + 1):
        cum = cum + sm[i:i + 1, :]
    return cum - cum0


def _forget_gates(fpre, lb):
    e = jnp.exp(-jnp.abs(fpre))
    log_sig = jnp.minimum(fpre, 0.0) - jnp.log1p(e)
    a = jnp.log(lb)
    y = jnp.log1p(-lb) + log_sig
    logf = jnp.maximum(a, y) + jnp.log1p(jnp.exp(-jnp.abs(a - y)))
    one_minus_f = (1.0 - lb) * (jnp.where(fpre >= 0.0, e, 1.0) / (1.0 + e))
    return logf, one_minus_f


def _hgrn_front(x, nm, w_in, lb):
    hn = _rmsnorm(x, nm).astype(BF16)
    proj = _dot_cols(hn, w_in)
    q = _silu(proj[:, :D_MODEL])
    logf, kk = _forget_gates(proj[:, D_MODEL:2 * D_MODEL], lb)
    v = proj[:, 2 * D_MODEL:2 * D_MODEL + D_INNER]
    z = proj[:, 2 * D_MODEL + D_INNER:]
    return q, logf, kk, v, z


def _head_rmsnorm(o, gn):
    outs = []
    for h in range(HEADS):
        outs.append(_rmsnorm(o[:, h * DV:(h + 1) * DV], gn))
    return jnp.concatenate(outs, axis=1)


def _ple(h, p, npl, w_gate, w_proj):
    gate = jax.nn.sigmoid(_dot_cols(_rmsnorm(h, npl).astype(BF16), w_gate))
    return h + gate * _dot_cols(p.astype(BF16), w_proj)


def _mixer_back(x, branch, w_out, p, npl, w_gate, w_proj):
    h = x + _dot_cols(branch.astype(BF16), w_out)
    return _ple(h, p, npl, w_gate, w_proj)


def _gmlp_front(x, nm, w_in, ln_g, ln_b):
    hn = _rmsnorm(x, nm).astype(BF16)
    proj = _dot_cols(hn, w_in)
    u = _gelu_tanh(proj[:, :D_INNER])
    vg = _gelu_tanh(proj[:, D_INNER:2 * D_INNER])
    z = proj[:, 2 * D_INNER:]
    mu = jnp.mean(vg, axis=-1, keepdims=True)
    vc = vg - mu
    vn = vc * lax.rsqrt(jnp.mean(vc * vc, axis=-1, keepdims=True) + EPS) * ln_g + ln_b
    return u, vn, z


def _hgrn_prompt_kernel(x_ref, p_ref, lbl_ref, nm_ref, win_ref, gn_ref, wout_ref, npl_ref,
                        wg_ref, wp_ref, tri_ref, lvl_ref, y_ref, sfin_ref, st_ref, *, layer_j):
    l = pl.program_id(1)

    @pl.when(l == 0)
    def _():
        st_ref[...] = jnp.zeros_like(st_ref)

    x = x_ref[...]
    lb = _forget_lower_bound(lbl_ref[...], layer_j)
    q, logf, kk, v, z = _hgrn_front(x, nm_ref[...], win_ref, lb)

    lg = logf * LOG2_E
    g_hi = lg.astype(BF16)
    g_lo = (lg - g_hi.astype(F32)).astype(BF16)
    g2 = jnp.concatenate([g_hi, g_lo], axis=1)
    b2 = _dot(tri_ref[...], g2)
    b = b2[:, :D_MODEL] + b2[:, D_MODEL:]
    b_last = b[TL - 1:TL, :]
    qs = (q * jnp.exp2(b)).astype(BF16)
    kd = (kk * jnp.exp2(b_last - b)).astype(BF16)
    b_tail = b[TL - 8:, :]
    qb = q.astype(BF16)
    kb = kk.astype(BF16)
    vb = v.astype(BF16)

    zk, zq = {}, {}
    for level in range(1, N_LEVELS + 1):
        hs = TL >> level
        if hs >= 8:
            grp = 2 * hs
            shp = (TL // grp, grp, D_MODEL)
            b3, q3, k3 = b.reshape(shp), q.reshape(shp), kk.reshape(shp)
            mid = b3[:, hs - 1:hs, :]
            zq3 = q3[:, hs:, :] * jnp.exp2(b3[:, hs:, :] - mid)
            zk3 = k3[:, :hs, :] * jnp.exp2(mid - b3[:, :hs, :])
            zq[level] = zq3.reshape(TL // 2, D_MODEL).astype(BF16)
            zk[level] = jnp.concatenate([zk3, zq3], axis=1).reshape(TL, D_MODEL).astype(BF16)
        else:
            shp = (TL // 8, 8, D_MODEL)
            b8, q8, k8 = b.reshape(shp), q.reshape(shp), kk.reshape(shp)
            sub = lax.broadcasted_iota(jnp.int32, shp, 1)
            is_q = (sub & hs) != 0
            if hs == 4:
                mid = jnp.broadcast_to(b8[:, 3:4, :], shp)
            elif hs == 2:
                mid = jnp.where(sub < 4, jnp.broadcast_to(b8[:, 1:2, :], shp),
                                jnp.broadcast_to(b8[:, 5:6, :], shp))
            else:
                mid = None
            if mid is None:
                e = jnp.where(is_q, lg.reshape(shp), 0.0)
            else:
                e = jnp.where(is_q, b8 - mid, mid - b8)
            zfine = jnp.where(is_q, q8, k8) * jnp.exp2(e)
            zk[level] = zfine.reshape(TL, D_MODEL).astype(BF16)

    lvl = lvl_ref[...]
    outs = []
    for h in range(HEADS):
        sl = slice(h * DK, (h + 1) * DK)
        vsl = slice(h * DV, (h + 1) * DV)
        st = st_ref[h]
        o = _dot(qs[:, sl], st.astype(BF16))
        pd = _dot_nt(qb[:, sl], kb[:, sl])
        diag = [jnp.where(lvl == 0, pd[:HALF, :HALF], 0.0),
                jnp.where(lvl == 0, pd[HALF:, HALF:], 0.0)]
        for level in range(N_LEVELS, 1, -1):
            hs = TL >> level
            zkl = zk[level][:, sl]
            if hs >= 8:
                grp = 2 * hs
                nq = HALF // 2
                pr = _dot_nt(zq[level][:, sl], zkl)
                hit = lvl.reshape(HALF // grp, grp, HALF)[:, hs:, :] == level
                for i in range(2):
                    pq = pr[i * nq:(i + 1) * nq, i * HALF:(i + 1) * HALF]
                    d3 = diag[i].reshape(HALF // grp, grp, HALF)
                    upper = jnp.where(hit, pq.reshape(HALF // grp, hs, HALF), d3[:, hs:, :])
                    diag[i] = jnp.concatenate([d3[:, :hs, :], upper], axis=1).reshape(HALF, HALF)
            else:
                pf = _dot_nt(zkl, zkl)
                hit = lvl == level
                diag[0] = jnp.where(hit, pf[:HALF, :HALF], diag[0])
                diag[1] = jnp.where(hit, pf[HALF:, HALF:], diag[1])
        a10 = _dot_nt(zq[1][:, sl], zk[1][:HALF, sl])
        att = jnp.concatenate(
            [jnp.concatenate([diag[0], jnp.zeros((HALF, HALF), F32)], axis=1),
             jnp.concatenate([a10, diag[1]], axis=1)], axis=0)
        o = o + _dot(att.astype(BF16), vb[:, vsl])
        dec = jnp.exp2(b_tail[:, sl].T[:, 7:8])
        st_ref[h] = st * dec + _dot_tn(kd[:, sl], vb[:, vsl])
        outs.append(o)
    o = jnp.concatenate(outs, axis=1)

    branch = _head_rmsnorm(o, gn_ref[...]) * _silu(z)
    y_ref[...] = _mixer_back(x, branch, wout_ref, p_ref[...], npl_ref[...],
                             wg_ref, wp_ref)

    @pl.when(l == pl.num_programs(1) - 1)
    def _():
        sfin_ref[...] = st_ref[...]


def _gmlp_prompt_kernel(x_ref, p_ref, nm_ref, win_ref, lng_ref, lnb_ref, wsp_ref, bsp_ref,
                        wout_ref, npl_ref, wg_ref, wp_ref, nf_ref, y_ref, vt_ref, *, final_norm):
    l = pl.program_id(1)
    x = x_ref[...]
    u, vn, z = _gmlp_front(x, nm_ref[...], win_ref, lng_ref[...], lnb_ref[...])
    vb = vn.astype(BF16)
    r = lax.broadcasted_iota(jnp.int32, (B_CHUNK, B_CHUNK), 0)
    c = lax.broadcasted_iota(jnp.int32, (B_CHUNK, B_CHUNK), 1)
    tril = c <= r
    bias = bsp_ref[...]
    cols = []
    for g in range(GROUPS):
        w = jnp.where(tril, wsp_ref[g], 0.0).astype(BF16)
        gsl = slice(g * DG, (g + 1) * DG)
        chunks = []
        for ci in range(TL // B_CHUNK):
            rsl = slice(ci * B_CHUNK, (ci + 1) * B_CHUNK)
            chunks.append(_dot(w, vb[rsl, gsl]) + bias[:, gsl])
        cols.append(jnp.concatenate(chunks, axis=0))
    s = jnp.concatenate(cols, axis=1)
    branch = u * s * _silu(z)
    h = _mixer_back(x, branch, wout_ref, p_ref[...], npl_ref[...], wg_ref, wp_ref)
    if final_norm:
        h = _rmsnorm(h, nf_ref[...])
    y_ref[...] = h

    @pl.when(l == pl.num_programs(1) - 1)
    def _():
        vt_ref[...] = vn[TL - B_CHUNK:, :]


def _hgrn_sample_front_kernel(x_ref, lbl_ref, nm_ref, win_ref, q_ref, f_ref, k_ref, v_ref, z_ref,
                              *, layer_j):
    lb = _forget_lower_bound(lbl_ref[...], layer_j)
    q, logf, kk, v, z = _hgrn_front(x_ref[...], nm_ref[...], win_ref, lb)
    q_ref[...] = q
    f_ref[...] = jnp.exp(logf)
    k_ref[...] = kk
    v_ref[...] = v
    z_ref[...] = z


def _hgrn_sample_state_kernel(q_ref, f_ref, k_ref, v_ref, s_ref, *rest):
    snew_ref, o_ref = rest[-2:]
    for i in range(SAMPLE_TB):
        q_t = q_ref[i].T
        f_t = f_ref[i].T
        k_t = k_ref[i].T
        for h in range(HEADS):
            s_new = f_t[:, h:h + 1] * s_ref[i, h] + k_t[:, h:h + 1] * v_ref[i, h:h + 1, :]
            snew_ref[i, h] = s_new
            o_ref[i, h:h + 1, :] = jnp.sum(q_t[:, h:h + 1] * s_new, axis=0, keepdims=True)


def _hgrn_sample_back_kernel(x_ref, o_ref, z_ref, p_ref, gn_ref, wout_ref, npl_ref, wg_ref,
                             wp_ref, y_ref):
    branch = _head_rmsnorm(o_ref[...], gn_ref[...]) * _silu(z_ref[...])
    y_ref[...] = _mixer_back(x_ref[...], branch, wout_ref, p_ref[...], npl_ref[...],
                             wg_ref, wp_ref)


def _gmlp_sample_kernel(x_ref, p_ref, nm_ref, win_ref, lng_ref, lnb_ref, w00_ref, b0_ref,
                        wout_ref, npl_ref, wg_ref, wp_ref, nf_ref, y_ref, vt_ref, *, final_norm):
    x = x_ref[...]
    u, vn, z = _gmlp_front(x, nm_ref[...], win_ref, lng_ref[...], lnb_ref[...])
    s = w00_ref[...] * vn + b0_ref[...]
    branch = u * s * _silu(z)
    h = _mixer_back(x, branch, wout_ref, p_ref[...], npl_ref[...], wg_ref, wp_ref)
    if final_norm:
        h = _rmsnorm(h, nf_ref[...])
    y_ref[...] = h
    vt_ref[...] = vn


def _resident(shape, index):
    return pl.BlockSpec(shape, index, pipeline_mode=pl.Buffered(1))


def _chunk_width(n):
    width = WEIGHT_CHUNK_3 if n % WEIGHT_CHUNK_3 == 0 else WEIGHT_CHUNK_1
    assert n % width == 0 and width % 1024 != 0
    return width


def _weight_cols(w, layer):
    _, k, n = w.shape
    width = _chunk_width(n)
    specs = [_resident((None, k, width), lambda *_, c=c: (layer, 0, c)) for c in range(n // width)]
    return [w] * len(specs), specs


def _expand(args):
    ops, specs, sizes = [], [], []
    for o, s in args:
        if isinstance(o, list):
            ops += o
            specs += s
            sizes.append(len(o))
        else:
            ops.append(o)
            specs.append(s)
            sizes.append(None)
    return ops, specs, sizes


def _regroup(fn, sizes):
    def wrapped(*refs):
        pos, grouped = 0, []
        for n in sizes:
            if n is None:
                grouped.append(refs[pos])
                pos += 1
            else:
                grouped.append(tuple(refs[pos:pos + n]))
                pos += n
        return fn(*grouped, *refs[pos:])
    return wrapped


def _params(*sem):
    return pltpu.CompilerParams(dimension_semantics=sem, vmem_limit_bytes=VMEM_LIMIT)


def _hgrn_prompt_layer(x, p_all, i, j, lb_logits, norm_mix, w_in, gnorm, w_out, norm_ple,
                       w_gate, w_proj, tri, lvl):
    bsz, seq, _ = x.shape
    n_a = lb_logits.shape[0]
    grid = (bsz, seq // TL)
    const2 = lambda b, l: (0, 0)
    operands, in_specs, sizes = _expand([
        (x, pl.BlockSpec((None, TL, D_MODEL), lambda b, l: (b, l, 0))),
        (p_all, pl.BlockSpec((None, None, TL, PLE_DIM), lambda b, l: (i, b, l, 0))),
        (lb_logits, _resident((n_a, D_MODEL), const2)),
        (norm_mix, _resident((None, 1, D_MODEL), lambda b, l: (i, 0, 0))),
        _weight_cols(w_in, j),
        (gnorm, _resident((None, 1, DV), lambda b, l: (j, 0, 0))),
        _weight_cols(w_out, j),
        (norm_ple, _resident((None, 1, D_MODEL), lambda b, l: (i, 0, 0))),
        _weight_cols(w_gate, i),
        _weight_cols(w_proj, i),
        (tri, _resident(tri.shape, const2)),
        (lvl, _resident(lvl.shape, const2)),
    ])
    out_specs = [
        pl.BlockSpec((None, TL, D_MODEL), lambda b, l: (b, l, 0)),
        pl.BlockSpec((None, HEADS, DK, DV), lambda b, l: (b, 0, 0, 0)),
    ]
    out_shape = [
        jax.ShapeDtypeStruct((bsz, seq, D_MODEL), F32),
        jax.ShapeDtypeStruct((bsz, HEADS, DK, DV), F32),
    ]
    return pl.pallas_call(
        _regroup(functools.partial(_hgrn_prompt_kernel, layer_j=j), sizes),
        grid=grid, in_specs=in_specs, out_specs=out_specs, out_shape=out_shape,
        scratch_shapes=[pltpu.VMEM((HEADS, DK, DV), F32)],
        compiler_params=_params("arbitrary", "arbitrary"),
        name=f"hgrn_prompt_{j}",
    )(*operands)


def _gmlp_prompt_layer(x, p_all, i, j, norm_mix, w_in, ln_g, ln_b, w_sp, bias_full, w_out,
                       norm_ple, w_gate, w_proj, norm_final, final_norm):
    bsz, seq, _ = x.shape
    grid = (bsz, seq // TL)
    operands, in_specs, sizes = _expand([
        (x, pl.BlockSpec((None, TL, D_MODEL), lambda b, l: (b, l, 0))),
        (p_all, pl.BlockSpec((None, None, TL, PLE_DIM), lambda b, l: (i, b, l, 0))),
        (norm_mix, _resident((None, 1, D_MODEL), lambda b, l: (i, 0, 0))),
        _weight_cols(w_in, j),
        (ln_g, _resident((None, 1, D_INNER), lambda b, l: (j, 0, 0))),
        (ln_b, _resident((None, 1, D_INNER), lambda b, l: (j, 0, 0))),
        (w_sp, _resident((None, GROUPS, B_CHUNK, B_CHUNK), lambda b, l: (j, 0, 0, 0))),
        (bias_full, _resident((None, B_CHUNK, D_INNER), lambda b, l: (j, 0, 0))),
        _weight_cols(w_out, j),
        (norm_ple, _resident((None, 1, D_MODEL), lambda b, l: (i, 0, 0))),
        _weight_cols(w_gate, i),
        _weight_cols(w_proj, i),
        (norm_final, _resident((1, D_MODEL), lambda b, l: (0, 0))),
    ])
    out_specs = [
        pl.BlockSpec((None, TL, D_MODEL), lambda b, l: (b, l, 0)),
        pl.BlockSpec((None, B_CHUNK, D_INNER), lambda b, l: (b, 0, 0)),
    ]
    out_shape = [
        jax.ShapeDtypeStruct((bsz, seq, D_MODEL), F32),
        jax.ShapeDtypeStruct((bsz, B_CHUNK, D_INNER), F32),
    ]
    return pl.pallas_call(
        _regroup(functools.partial(_gmlp_prompt_kernel, final_norm=final_norm), sizes),
        grid=grid, in_specs=in_specs, out_specs=out_specs, out_shape=out_shape,
        compiler_params=_params("arbitrary", "arbitrary"),
        name=f"gmlp_prompt_{j}",
    )(*operands)


def _whole(shape):
    nd = len(shape)
    return pl.BlockSpec(shape, lambda *_: (0,) * nd)


def _hgrn_sample_layer(x, p_all, s0, s_prev, i, j, lb_logits, norm_mix, w_in, gnorm, w_out,
                       norm_ple, w_gate, w_proj):
    n = x.shape[0]
    n_a = lb_logits.shape[0]
    operands, in_specs, sizes = _expand([
        (x, _whole((n, D_MODEL))),
        (lb_logits, _whole((n_a, D_MODEL))),
        (norm_mix, pl.BlockSpec((None, 1, D_MODEL), lambda g: (i, 0, 0))),
        _weight_cols(w_in, j),
    ])
    q, f, kk, v, z = pl.pallas_call(
        _regroup(functools.partial(_hgrn_sample_front_kernel, layer_j=j), sizes),
        grid=(1,),
        in_specs=in_specs,
        out_specs=[_whole((n, D_MODEL))] * 3 + [_whole((n, D_INNER))] * 2,
        out_shape=[jax.ShapeDtypeStruct((n, D_MODEL), F32)] * 3
        + [jax.ShapeDtypeStruct((n, D_INNER), F32)] * 2,
        compiler_params=_params("arbitrary"),
        name=f"hgrn_sample_front_{j}",
    )(*operands)

    tb = SAMPLE_TB
    hk = pl.BlockSpec((tb, HEADS, DK), lambda g: (g, 0, 0))
    hv = pl.BlockSpec((tb, HEADS, DV), lambda g: (g, 0, 0))
    st = pl.BlockSpec((None, tb, HEADS, DK, DV), lambda g: (j, g, 0, 0, 0))
    operands = [q.reshape(n, HEADS, DK), f.reshape(n, HEADS, DK), kk.reshape(n, HEADS, DK),
                v.reshape(n, HEADS, DV), s0]
    in_specs = [hk, hk, hk, hv, st]
    aliases = {}
    if s_prev is not None:
        operands.append(s_prev)
        in_specs.append(pl.BlockSpec(memory_space=pl.ANY))
        aliases = {len(operands) - 1: 0}
    s_new, o = pl.pallas_call(
        _hgrn_sample_state_kernel,
        grid=(n // tb,),
        in_specs=in_specs,
        out_specs=[st, hv],
        out_shape=[jax.ShapeDtypeStruct(s0.shape, F32),
                   jax.ShapeDtypeStruct((n, HEADS, DV), F32)],
        input_output_aliases=aliases,
        compiler_params=_params("arbitrary"),
        name=f"hgrn_sample_state_{j}",
    )(*operands)

    operands, in_specs, sizes = _expand([
        (x, _whole((n, D_MODEL))),
        (o.reshape(n, D_INNER), _whole((n, D_INNER))),
        (z, _whole((n, D_INNER))),
        (p_all, pl.BlockSpec((None, n, PLE_DIM), lambda g: (i, 0, 0))),
        (gnorm, pl.BlockSpec((None, 1, DV), lambda g: (j, 0, 0))),
        _weight_cols(w_out, j),
        (norm_ple, pl.BlockSpec((None, 1, D_MODEL), lambda g: (i, 0, 0))),
        _weight_cols(w_gate, i),
        _weight_cols(w_proj, i),
    ])
    y = pl.pallas_call(
        _regroup(_hgrn_sample_back_kernel, sizes),
        grid=(1,),
        in_specs=in_specs,
        out_specs=_whole((n, D_MODEL)),
        out_shape=jax.ShapeDtypeStruct((n, D_MODEL), F32),
        compiler_params=_params("arbitrary"),
        name=f"hgrn_sample_back_{j}",
    )(*operands)
    return y, s_new


def _gmlp_sample_layer(x, p_all, i, j, norm_mix, w_in, ln_g, ln_b, w00, b0, w_out, norm_ple,
                       w_gate, w_proj, norm_final, final_norm):
    n = x.shape[0]
    vec = lambda layer, width: pl.BlockSpec((None, 1, width), lambda g: (layer, 0, 0))
    operands, in_specs, sizes = _expand([
        (x, _whole((n, D_MODEL))),
        (p_all, pl.BlockSpec((None, n, PLE_DIM), lambda g: (i, 0, 0))),
        (norm_mix, vec(i, D_MODEL)),
        _weight_cols(w_in, j),
        (ln_g, vec(j, D_INNER)),
        (ln_b, vec(j, D_INNER)),
        (w00, vec(j, D_INNER)),
        (b0, vec(j, D_INNER)),
        _weight_cols(w_out, j),
        (norm_ple, vec(i, D_MODEL)),
        _weight_cols(w_gate, i),
        _weight_cols(w_proj, i),
        (norm_final, _whole((1, D_MODEL))),
    ])
    return pl.pallas_call(
        _regroup(functools.partial(_gmlp_sample_kernel, final_norm=final_norm), sizes),
        grid=(1,),
        in_specs=in_specs,
        out_specs=[_whole((n, D_MODEL)), _whole((n, D_INNER))],
        out_shape=[jax.ShapeDtypeStruct((n, D_MODEL), F32),
                   jax.ShapeDtypeStruct((n, D_INNER), F32)],
        compiler_params=_params("arbitrary"),
        name=f"gmlp_sample_{j}",
    )(*operands)


def kernel(x_prompt, x_sample, state_hgrn, p_prompt, p_sample, norm_mix, w_in_a, lb_logits,
           gnorm_a, w_out_a, w_in_b, ln_v_g, ln_v_b, w_spatial, b_spatial, w_out_b,
           norm_ple, w_ple_gate, w_ple_proj, norm_final):
    depth = norm_mix.shape[0]
    n_s = x_sample.shape[0]
    assert x_prompt.shape[1] % TL == 0 and TL % B_CHUNK == 0 and x_sample.shape[1] == 1
    assert n_s % SAMPLE_TB == 0

    w_in_a_b = w_in_a.astype(BF16)
    w_out_a_b = w_out_a.astype(BF16)
    w_in_b_b = w_in_b.astype(BF16)
    w_out_b_b = w_out_b.astype(BF16)
    w_gate_b = w_ple_gate.astype(BF16)
    w_proj_b = w_ple_proj.astype(BF16)

    tri = jnp.asarray(_TRI_NP, dtype=BF16)
    lvl = jnp.asarray(_LVL_NP)
    nf = norm_final.reshape(1, D_MODEL)
    bias_full = jnp.repeat(jnp.swapaxes(b_spatial, 1, 2), DG, axis=2)
    w00 = jnp.repeat(w_spatial[:, :, 0, 0], DG, axis=1)[:, None, :]
    b0 = jnp.repeat(b_spatial[:, :, 0], DG, axis=1)[:, None, :]
    nmix = norm_mix[:, None, :]
    nple = norm_ple[:, None, :]
    gn = gnorm_a[:, None, :]
    lng = ln_v_g[:, None, :]
    lnb = ln_v_b[:, None, :]

    if depth % 2 == 1:
        raise NotImplementedError("final norm is fused into the last (chunk-MLP) layer")
    hp = x_prompt
    hs = x_sample.reshape(n_s, D_MODEL)
    ps = p_sample.reshape(depth, n_s, PLE_DIM)
    st_p, vt_p, vt_s = [], [], []
    st_s = None
    for i in range(depth):
        j = i // 2
        last = i == depth - 1
        if i % 2 == 0:
            hp, s_fin = _hgrn_prompt_layer(hp, p_prompt, i, j, lb_logits, nmix, w_in_a_b,
                                           gn, w_out_a_b, nple, w_gate_b, w_proj_b,
                                           tri, lvl)
            st_p.append(s_fin)
            hs, st_s = _hgrn_sample_layer(hs, ps, state_hgrn, st_s, i, j, lb_logits, nmix,
                                          w_in_a_b, gn, w_out_a_b, nple, w_gate_b, w_proj_b)
        else:
            hp, vt = _gmlp_prompt_layer(hp, p_prompt, i, j, nmix, w_in_b_b, lng, lnb,
                                        w_spatial, bias_full, w_out_b_b, nple, w_gate_b,
                                        w_proj_b, nf, last)
            vt_p.append(vt)
            hs, vts = _gmlp_sample_layer(hs, ps, i, j, nmix, w_in_b_b, lng, lnb,
                                         w00, b0, w_out_b_b, nple, w_gate_b, w_proj_b, nf, last)
            vt_s.append(vts.reshape(n_s, 1, D_INNER))
    return (hp, hs.reshape(n_s, 1, D_MODEL), jnp.stack(st_p), st_s,
            jnp.stack(vt_p), jnp.stack(vt_s))
```

```python
import functools
import math

import numpy as np
import jax
import jax.numpy as jnp
from jax import lax
from jax.experimental import pallas as pl
from jax.experimental.pallas import tpu as pltpu

D_MODEL = 1024
D_INNER = 2048
HEADS = 8
DK = 128
DV = 256
GROUPS = 8
DG = 256
B_CHUNK = 128
PLE_DIM = 256
EPS = 1e-6
LOG2_E = 1.4426950408889634

TL = 256
HALF = TL // 2
N_LEVELS = int(math.log2(TL))
SAMPLE_TB = 4
WEIGHT_CHUNK_3 = 1536
WEIGHT_CHUNK_1 = 512
VMEM_LIMIT = 58 * 1024 * 1024

F32 = jnp.float32
BF16 = jnp.bfloat16


def _build_level_consts(tl):
    t = np.arange(tl)[:, None]
    s = np.arange(tl)[None, :]
    tri = (s <= t).astype(np.float32)
    x = np.maximum(t ^ s, 1)
    p = np.floor(np.log2(x)).astype(np.int32)
    nlev = int(math.log2(tl))
    lvl = np.where(t == s, 0, np.where(t > s, nlev - p, -1)).astype(np.int32)
    return tri, lvl[:tl // 2, :tl // 2]


_TRI_NP, _LVL_NP = _build_level_consts(TL)


def _dot(a, b):
    return jnp.dot(a, b, preferred_element_type=F32)


def _dot_cols(a, w_refs):
    return jnp.concatenate([_dot(a, w[...]) for w in w_refs], axis=1)


def _dot_nt(a, b):
    return lax.dot_general(a, b, (((1,), (1,)), ((), ())), preferred_element_type=F32)


def _dot_tn(a, b):
    return lax.dot_general(a, b, (((0,), (0,)), ((), ())), preferred_element_type=F32)


def _rmsnorm(x, g):
    ms = jnp.mean(x * x, axis=-1, keepdims=True)
    return x * lax.rsqrt(ms + EPS) * g


def _silu(x):
    return x * jax.nn.sigmoid(x)


def _gelu_tanh(x):
    c = math.sqrt(2.0 / math.pi)
    return 0.5 * x * (1.0 + jnp.tanh(c * (x + 0.044715 * (x * x * x))))


def _forget_lower_bound(lbl, j):
    mx = jnp.max(lbl, axis=0, keepdims=True)
    e = jnp.exp(lbl - mx)
    sm = e / jnp.sum(e, axis=0, keepdims=True)
    cum0 = sm[0:1, :]
    cum = cum0
    for i in range(1, j + 1):
        cum = cum + sm[i:i + 1, :]
    return cum - cum0


def _forget_gates(fpre, lb):
    e = jnp.exp(-jnp.abs(fpre))
    log_sig = jnp.minimum(fpre, 0.0) - jnp.log1p(e)
    a = jnp.log(lb)
    y = jnp.log1p(-lb) + log_sig
    logf = jnp.maximum(a, y) + jnp.log1p(jnp.exp(-jnp.abs(a - y)))
    one_minus_f = (1.0 - lb) * (jnp.where(fpre >= 0.0, e, 1.0) / (1.0 + e))
    return logf, one_minus_f


def _hgrn_front(x, nm, w_in, lb):
    hn = _rmsnorm(x, nm).astype(BF16)
    proj = _dot_cols(hn, w_in)
    q = _silu(proj[:, :D_MODEL])
    logf, kk = _forget_gates(proj[:, D_MODEL:2 * D_MODEL], lb)
    v = proj[:, 2 * D_MODEL:2 * D_MODEL + D_INNER]
    z = proj[:, 2 * D_MODEL + D_INNER:]
    return q, logf, kk, v, z


def _head_rmsnorm(o, gn):
    outs = []
    for h in range(HEADS):
        outs.append(_rmsnorm(o[:, h * DV:(h + 1) * DV], gn))
    return jnp.concatenate(outs, axis=1)


def _ple(h, p, npl, w_gate, w_proj):
    gate = jax.nn.sigmoid(_dot_cols(_rmsnorm(h, npl).astype(BF16), w_gate))
    return h + gate * _dot_cols(p.astype(BF16), w_proj)


def _mixer_back(x, branch, w_out, p, npl, w_gate, w_proj):
    h = x + _dot_cols(branch.astype(BF16), w_out)
    return _ple(h, p, npl, w_gate, w_proj)


def _gmlp_front(x, nm, w_in, ln_g, ln_b):
    hn = _rmsnorm(x, nm).astype(BF16)
    proj = _dot_cols(hn, w_in)
    u = _gelu_tanh(proj[:, :D_INNER])
    vg = _gelu_tanh(proj[:, D_INNER:2 * D_INNER])
    z = proj[:, 2 * D_INNER:]
    mu = jnp.mean(vg, axis=-1, keepdims=True)
    vc = vg - mu
    vn = vc * lax.rsqrt(jnp.mean(vc * vc, axis=-1, keepdims=True) + EPS) * ln_g + ln_b
    return u, vn, z


def _hgrn_prompt_kernel(x_ref, p_ref, lbl_ref, nm_ref, win_ref, gn_ref, wout_ref, npl_ref,
                        wg_ref, wp_ref, tri_ref, lvl_ref, y_ref, sfin_ref, st_ref, *, layer_j):
    l = pl.program_id(1)

    @pl.when(l == 0)
    def _():
        st_ref[...] = jnp.zeros_like(st_ref)

    x = x_ref[...]
    lb = _forget_lower_bound(lbl_ref[...], layer_j)
    q, logf, kk, v, z = _hgrn_front(x, nm_ref[...], win_ref, lb)

    lg = logf * LOG2_E
    g_hi = lg.astype(BF16)
    g_lo = (lg - g_hi.astype(F32)).astype(BF16)
    g2 = jnp.concatenate([g_hi, g_lo], axis=1)
    b2 = _dot(tri_ref[...], g2)
    b = b2[:, :D_MODEL] + b2[:, D_MODEL:]
    b_last = b[TL - 1:TL, :]
    qs = (q * jnp.exp2(b)).astype(BF16)
    kd = (kk * jnp.exp2(b_last - b)).astype(BF16)
    dec = jnp.exp2(b_last)
    qb = q.astype(BF16)
    kb = kk.astype(BF16)
    vb = v.astype(BF16)

    zk, zq = {}, {}
    for level in range(1, N_LEVELS + 1):
        hs = TL >> level
        if hs >= 8:
            grp = 2 * hs
            shp = (TL // grp, grp, D_MODEL)
            b3, q3, k3 = b.reshape(shp), q.reshape(shp), kk.reshape(shp)
            mid = b3[:, hs - 1:hs, :]
            zq3 = q3[:, hs:, :] * jnp.exp2(b3[:, hs:, :] - mid)
            zk3 = k3[:, :hs, :] * jnp.exp2(mid - b3[:, :hs, :])
            zq[level] = zq3.reshape(TL // 2, D_MODEL).astype(BF16)
            zk[level] = jnp.concatenate([zk3, zq3], axis=1).reshape(TL, D_MODEL).astype(BF16)
        else:
            shp = (TL // 8, 8, D_MODEL)
            b8, q8, k8 = b.reshape(shp), q.reshape(shp), kk.reshape(shp)
            sub = lax.broadcasted_iota(jnp.int32, shp, 1)
            is_q = (sub & hs) != 0
            if hs == 4:
                mid = jnp.broadcast_to(b8[:, 3:4, :], shp)
            elif hs == 2:
                mid = jnp.where(sub < 4, jnp.broadcast_to(b8[:, 1:2, :], shp),
                                jnp.broadcast_to(b8[:, 5:6, :], shp))
            else:
                mid = None
            if mid is None:
                e = jnp.where(is_q, lg.reshape(shp), 0.0)
            else:
                e = jnp.where(is_q, b8 - mid, mid - b8)
            zfine = jnp.where(is_q, q8, k8) * jnp.exp2(e)
            zk[level] = zfine.reshape(TL, D_MODEL).astype(BF16)

    lvl = lvl_ref[...]
    atts = []
    for h in range(HEADS):
        sl = slice(h * DK, (h + 1) * DK)
        pd = _dot_nt(qb[:, sl], kb[:, sl])
        diag = [jnp.where(lvl == 0, pd[:HALF, :HALF], 0.0),
                jnp.where(lvl == 0, pd[HALF:, HALF:], 0.0)]
        for level in range(N_LEVELS, 1, -1):
            hs = TL >> level
            zkl = zk[level][:, sl]
            if hs >= 8:
                grp = 2 * hs
                nq = HALF // 2
                pr = _dot_nt(zq[level][:, sl], zkl)
                hit = lvl.reshape(HALF // grp, grp, HALF)[:, hs:, :] == level
                for i in range(2):
                    pq = pr[i * nq:(i + 1) * nq, i * HALF:(i + 1) * HALF]
                    d3 = diag[i].reshape(HALF // grp, grp, HALF)
                    upper = jnp.where(hit, pq.reshape(HALF // grp, hs, HALF), d3[:, hs:, :])
                    diag[i] = jnp.concatenate([d3[:, :hs, :], upper], axis=1).reshape(HALF, HALF)
            else:
                pf = _dot_nt(zkl, zkl)
                hit = lvl == level
                diag[0] = jnp.where(hit, pf[:HALF, :HALF], diag[0])
                diag[1] = jnp.where(hit, pf[HALF:, HALF:], diag[1])
        a10 = _dot_nt(zq[1][:, sl], zk[1][:HALF, sl])
        att = jnp.concatenate(
            [jnp.concatenate([diag[0], jnp.zeros((HALF, HALF), F32)], axis=1),
             jnp.concatenate([a10, diag[1]], axis=1)], axis=0)
        atts.append(att.astype(BF16))
    outs = []
    for h in range(HEADS):
        sl = slice(h * DK, (h + 1) * DK)
        vsl = slice(h * DV, (h + 1) * DV)
        st = st_ref[h]
        o = _dot_nt(qs[:, sl], st.astype(BF16)) + _dot(atts[h], vb[:, vsl])
        st_ref[h] = st * dec[:, sl] + _dot_tn(vb[:, vsl], kd[:, sl])
        outs.append(o)
    o = jnp.concatenate(outs, axis=1)

    branch = _head_rmsnorm(o, gn_ref[...]) * _silu(z)
    y_ref[...] = _mixer_back(x, branch, wout_ref, p_ref[...], npl_ref[...],
                             wg_ref, wp_ref)

    @pl.when(l == pl.num_programs(1) - 1)
    def _():
        for h in range(HEADS):
            sfin_ref[h] = st_ref[h].T


def _gmlp_prompt_kernel(x_ref, p_ref, nm_ref, win_ref, lng_ref, lnb_ref, wsp_ref, bsp_ref,
                        wout_ref, npl_ref, wg_ref, wp_ref, nf_ref, y_ref, vt_ref, *, final_norm):
    l = pl.program_id(1)
    x = x_ref[...]
    u, vn, z = _gmlp_front(x, nm_ref[...], win_ref, lng_ref[...], lnb_ref[...])
    vb = vn.astype(BF16)
    r = lax.broadcasted_iota(jnp.int32, (B_CHUNK, B_CHUNK), 0)
    c = lax.broadcasted_iota(jnp.int32, (B_CHUNK, B_CHUNK), 1)
    tril = c <= r
    bias = bsp_ref[...]
    cols = []
    for g in range(GROUPS):
        w = jnp.where(tril, wsp_ref[g], 0.0).astype(BF16)
        gsl = slice(g * DG, (g + 1) * DG)
        chunks = []
        for ci in range(TL // B_CHUNK):
            rsl = slice(ci * B_CHUNK, (ci + 1) * B_CHUNK)
            chunks.append(_dot(w, vb[rsl, gsl]) + bias[:, gsl])
        cols.append(jnp.concatenate(chunks, axis=0))
    s = jnp.concatenate(cols, axis=1)
    branch = u * s * _silu(z)
    h = _mixer_back(x, branch, wout_ref, p_ref[...], npl_ref[...], wg_ref, wp_ref)
    if final_norm:
        h = _rmsnorm(h, nf_ref[...])
    y_ref[...] = h

    @pl.when(l == pl.num_programs(1) - 1)
    def _():
        vt_ref[...] = vn[TL - B_CHUNK:, :]


def _hgrn_sample_front_kernel(x_ref, lbl_ref, nm_ref, win_ref, q_ref, f_ref, k_ref, v_ref, z_ref,
                              *, layer_j):
    lb = _forget_lower_bound(lbl_ref[...], layer_j)
    q, logf, kk, v, z = _hgrn_front(x_ref[...], nm_ref[...], win_ref, lb)
    q_ref[...] = q
    f_ref[...] = jnp.exp(logf)
    k_ref[...] = kk
    v_ref[...] = v
    z_ref[...] = z


def _hgrn_sample_state_kernel(q_ref, f_ref, k_ref, v_ref, s_ref, *rest):
    snew_ref, o_ref = rest[-2:]
    for i in range(SAMPLE_TB):
        q_t = q_ref[i].T
        f_t = f_ref[i].T
        k_t = k_ref[i].T
        for h in range(HEADS):
            s_new = f_t[:, h:h + 1] * s_ref[i, h] + k_t[:, h:h + 1] * v_ref[i, h:h + 1, :]
            snew_ref[i, h] = s_new
            o_ref[i, h:h + 1, :] = jnp.sum(q_t[:, h:h + 1] * s_new, axis=0, keepdims=True)


def _hgrn_sample_back_kernel(x_ref, o_ref, z_ref, p_ref, gn_ref, wout_ref, npl_ref, wg_ref,
                             wp_ref, y_ref):
    branch = _head_rmsnorm(o_ref[...], gn_ref[...]) * _silu(z_ref[...])
    y_ref[...] = _mixer_back(x_ref[...], branch, wout_ref, p_ref[...], npl_ref[...],
                             wg_ref, wp_ref)


def _gmlp_sample_kernel(x_ref, p_ref, nm_ref, win_ref, lng_ref, lnb_ref, w00_ref, b0_ref,
                        wout_ref, npl_ref, wg_ref, wp_ref, nf_ref, y_ref, vt_ref, *, final_norm):
    x = x_ref[...]
    u, vn, z = _gmlp_front(x, nm_ref[...], win_ref, lng_ref[...], lnb_ref[...])
    s = w00_ref[...] * vn + b0_ref[...]
    branch = u * s * _silu(z)
    h = _mixer_back(x, branch, wout_ref, p_ref[...], npl_ref[...], wg_ref, wp_ref)
    if final_norm:
        h = _rmsnorm(h, nf_ref[...])
    y_ref[...] = h
    vt_ref[...] = vn


def _resident(shape, index):
    return pl.BlockSpec(shape, index, pipeline_mode=pl.Buffered(1))


def _chunk_width(n):
    width = WEIGHT_CHUNK_3 if n % WEIGHT_CHUNK_3 == 0 else WEIGHT_CHUNK_1
    assert n % width == 0 and width % 1024 != 0
    return width


def _weight_cols(w, layer):
    _, k, n = w.shape
    width = _chunk_width(n)
    specs = [_resident((None, k, width), lambda *_, c=c: (layer, 0, c)) for c in range(n // width)]
    return [w] * len(specs), specs


def _expand(args):
    ops, specs, sizes = [], [], []
    for o, s in args:
        if isinstance(o, list):
            ops += o
            specs += s
            sizes.append(len(o))
        else:
            ops.append(o)
            specs.append(s)
            sizes.append(None)
    return ops, specs, sizes


def _regroup(fn, sizes):
    def wrapped(*refs):
        pos, grouped = 0, []
        for n in sizes:
            if n is None:
                grouped.append(refs[pos])
                pos += 1
            else:
                grouped.append(tuple(refs[pos:pos + n]))
                pos += n
        return fn(*grouped, *refs[pos:])
    return wrapped


def _params(*sem):
    return pltpu.CompilerParams(dimension_semantics=sem, vmem_limit_bytes=VMEM_LIMIT)


def _hgrn_prompt_layer(x, p_all, i, j, lb_logits, norm_mix, w_in, gnorm, w_out, norm_ple,
                       w_gate, w_proj, tri, lvl):
    bsz, seq, _ = x.shape
    n_a = lb_logits.shape[0]
    grid = (bsz, seq // TL)
    const2 = lambda b, l: (0, 0)
    operands, in_specs, sizes = _expand([
        (x, pl.BlockSpec((None, TL, D_MODEL), lambda b, l: (b, l, 0))),
        (p_all, pl.BlockSpec((None, None, TL, PLE_DIM), lambda b, l: (i, b, l, 0))),
        (lb_logits, _resident((n_a, D_MODEL), const2)),
        (norm_mix, _resident((None, 1, D_MODEL), lambda b, l: (i, 0, 0))),
        _weight_cols(w_in, j),
        (gnorm, _resident((None, 1, DV), lambda b, l: (j, 0, 0))),
        _weight_cols(w_out, j),
        (norm_ple, _resident((None, 1, D_MODEL), lambda b, l: (i, 0, 0))),
        _weight_cols(w_gate, i),
        _weight_cols(w_proj, i),
        (tri, _resident(tri.shape, const2)),
        (lvl, _resident(lvl.shape, const2)),
    ])
    out_specs = [
        pl.BlockSpec((None, TL, D_MODEL), lambda b, l: (b, l, 0)),
        pl.BlockSpec((None, HEADS, DK, DV), lambda b, l: (b, 0, 0, 0)),
    ]
    out_shape = [
        jax.ShapeDtypeStruct((bsz, seq, D_MODEL), F32),
        jax.ShapeDtypeStruct((bsz, HEADS, DK, DV), F32),
    ]
    return pl.pallas_call(
        _regroup(functools.partial(_hgrn_prompt_kernel, layer_j=j), sizes),
        grid=grid, in_specs=in_specs, out_specs=out_specs, out_shape=out_shape,
        scratch_shapes=[pltpu.VMEM((HEADS, DV, DK), F32)],
        compiler_params=_params("arbitrary", "arbitrary"),
        name=f"hgrn_prompt_{j}",
    )(*operands)


def _gmlp_prompt_layer(x, p_all, i, j, norm_mix, w_in, ln_g, ln_b, w_sp, bias_full, w_out,
                       norm_ple, w_gate, w_proj, norm_final, final_norm):
    bsz, seq, _ = x.shape
    grid = (bsz, seq // TL)
    operands, in_specs, sizes = _expand([
        (x, pl.BlockSpec((None, TL, D_MODEL), lambda b, l: (b, l, 0))),
        (p_all, pl.BlockSpec((None, None, TL, PLE_DIM), lambda b, l: (i, b, l, 0))),
        (norm_mix, _resident((None, 1, D_MODEL), lambda b, l: (i, 0, 0))),
        _weight_cols(w_in, j),
        (ln_g, _resident((None, 1, D_INNER), lambda b, l: (j, 0, 0))),
        (ln_b, _resident((None, 1, D_INNER), lambda b, l: (j, 0, 0))),
        (w_sp, _resident((None, GROUPS, B_CHUNK, B_CHUNK), lambda b, l: (j, 0, 0, 0))),
        (bias_full, _resident((None, B_CHUNK, D_INNER), lambda b, l: (j, 0, 0))),
        _weight_cols(w_out, j),
        (norm_ple, _resident((None, 1, D_MODEL), lambda b, l: (i, 0, 0))),
        _weight_cols(w_gate, i),
        _weight_cols(w_proj, i),
        (norm_final, _resident((1, D_MODEL), lambda b, l: (0, 0))),
    ])
    out_specs = [
        pl.BlockSpec((None, TL, D_MODEL), lambda b, l: (b, l, 0)),
        pl.BlockSpec((None, B_CHUNK, D_INNER), lambda b, l: (b, 0, 0)),
    ]
    out_shape = [
        jax.ShapeDtypeStruct((bsz, seq, D_MODEL), F32),
        jax.ShapeDtypeStruct((bsz, B_CHUNK, D_INNER), F32),
    ]
    return pl.pallas_call(
        _regroup(functools.partial(_gmlp_prompt_kernel, final_norm=final_norm), sizes),
        grid=grid, in_specs=in_specs, out_specs=out_specs, out_shape=out_shape,
        compiler_params=_params("arbitrary", "arbitrary"),
        name=f"gmlp_prompt_{j}",
    )(*operands)


def _whole(shape):
    nd = len(shape)
    return pl.BlockSpec(shape, lambda *_: (0,) * nd)


def _hgrn_sample_layer(x, p_all, s0, s_prev, i, j, lb_logits, norm_mix, w_in, gnorm, w_out,
                       norm_ple, w_gate, w_proj):
    n = x.shape[0]
    n_a = lb_logits.shape[0]
    operands, in_specs, sizes = _expand([
        (x, _whole((n, D_MODEL))),
        (lb_logits, _whole((n_a, D_MODEL))),
        (norm_mix, pl.BlockSpec((None, 1, D_MODEL), lambda g: (i, 0, 0))),
        _weight_cols(w_in, j),
    ])
    q, f, kk, v, z = pl.pallas_call(
        _regroup(functools.partial(_hgrn_sample_front_kernel, layer_j=j), sizes),
        grid=(1,),
        in_specs=in_specs,
        out_specs=[_whole((n, D_MODEL))] * 3 + [_whole((n, D_INNER))] * 2,
        out_shape=[jax.ShapeDtypeStruct((n, D_MODEL), F32)] * 3
        + [jax.ShapeDtypeStruct((n, D_INNER), F32)] * 2,
        compiler_params=_params("arbitrary"),
        name=f"hgrn_sample_front_{j}",
    )(*operands)

    tb = SAMPLE_TB
    hk = pl.BlockSpec((tb, HEADS, DK), lambda g: (g, 0, 0))
    hv = pl.BlockSpec((tb, HEADS, DV), lambda g: (g, 0, 0))
    st = pl.BlockSpec((None, tb, HEADS, DK, DV), lambda g: (j, g, 0, 0, 0))
    operands = [q.reshape(n, HEADS, DK), f.reshape(n, HEADS, DK), kk.reshape(n, HEADS, DK),
                v.reshape(n, HEADS, DV), s0]
    in_specs = [hk, hk, hk, hv, st]
    aliases = {}
    if s_prev is not None:
        operands.append(s_prev)
        in_specs.append(pl.BlockSpec(memory_space=pl.ANY))
        aliases = {len(operands) - 1: 0}
    s_new, o = pl.pallas_call(
        _hgrn_sample_state_kernel,
        grid=(n // tb,),
        in_specs=in_specs,
        out_specs=[st, hv],
        out_shape=[jax.ShapeDtypeStruct(s0.shape, F32),
                   jax.ShapeDtypeStruct((n, HEADS, DV), F32)],
        input_output_aliases=aliases,
        compiler_params=_params("arbitrary"),
        name=f"hgrn_sample_state_{j}",
    )(*operands)

    operands, in_specs, sizes = _expand([
        (x, _whole((n, D_MODEL))),
        (o.reshape(n, D_INNER), _whole((n, D_INNER))),
        (z, _whole((n, D_INNER))),
        (p_all, pl.BlockSpec((None, n, PLE_DIM), lambda g: (i, 0, 0))),
        (gnorm, pl.BlockSpec((None, 1, DV), lambda g: (j, 0, 0))),
        _weight_cols(w_out, j),
        (norm_ple, pl.BlockSpec((None, 1, D_MODEL), lambda g: (i, 0, 0))),
        _weight_cols(w_gate, i),
        _weight_cols(w_proj, i),
    ])
    y = pl.pallas_call(
        _regroup(_hgrn_sample_back_kernel, sizes),
        grid=(1,),
        in_specs=in_specs,
        out_specs=_whole((n, D_MODEL)),
        out_shape=jax.ShapeDtypeStruct((n, D_MODEL), F32),
        compiler_params=_params("arbitrary"),
        name=f"hgrn_sample_back_{j}",
    )(*operands)
    return y, s_new


def _gmlp_sample_layer(x, p_all, i, j, norm_mix, w_in, ln_g, ln_b, w00, b0, w_out, norm_ple,
                       w_gate, w_proj, norm_final, final_norm):
    n = x.shape[0]
    vec = lambda layer, width: pl.BlockSpec((None, 1, width), lambda g: (layer, 0, 0))
    operands, in_specs, sizes = _expand([
        (x, _whole((n, D_MODEL))),
        (p_all, pl.BlockSpec((None, n, PLE_DIM), lambda g: (i, 0, 0))),
        (norm_mix, vec(i, D_MODEL)),
        _weight_cols(w_in, j),
        (ln_g, vec(j, D_INNER)),
        (ln_b, vec(j, D_INNER)),
        (w00, vec(j, D_INNER)),
        (b0, vec(j, D_INNER)),
        _weight_cols(w_out, j),
        (norm_ple, vec(i, D_MODEL)),
        _weight_cols(w_gate, i),
        _weight_cols(w_proj, i),
        (norm_final, _whole((1, D_MODEL))),
    ])
    return pl.pallas_call(
        _regroup(functools.partial(_gmlp_sample_kernel, final_norm=final_norm), sizes),
        grid=(1,),
        in_specs=in_specs,
        out_specs=[_whole((n, D_MODEL)), _whole((n, D_INNER))],
        out_shape=[jax.ShapeDtypeStruct((n, D_MODEL), F32),
                   jax.ShapeDtypeStruct((n, D_INNER), F32)],
        compiler_params=_params("arbitrary"),
        name=f"gmlp_sample_{j}",
    )(*operands)


def kernel(x_prompt, x_sample, state_hgrn, p_prompt, p_sample, norm_mix, w_in_a, lb_logits,
           gnorm_a, w_out_a, w_in_b, ln_v_g, ln_v_b, w_spatial, b_spatial, w_out_b,
           norm_ple, w_ple_gate, w_ple_proj, norm_final):
    depth = norm_mix.shape[0]
    n_s = x_sample.shape[0]
    assert x_prompt.shape[1] % TL == 0 and TL % B_CHUNK == 0 and x_sample.shape[1] == 1
    assert n_s % SAMPLE_TB == 0

    w_in_a_b = w_in_a.astype(BF16)
    w_out_a_b = w_out_a.astype(BF16)
    w_in_b_b = w_in_b.astype(BF16)
    w_out_b_b = w_out_b.astype(BF16)
    w_gate_b = w_ple_gate.astype(BF16)
    w_proj_b = w_ple_proj.astype(BF16)

    tri = jnp.asarray(_TRI_NP, dtype=BF16)
    lvl = jnp.asarray(_LVL_NP)
    nf = norm_final.reshape(1, D_MODEL)
    bias_full = jnp.repeat(jnp.swapaxes(b_spatial, 1, 2), DG, axis=2)
    w00 = jnp.repeat(w_spatial[:, :, 0, 0], DG, axis=1)[:, None, :]
    b0 = jnp.repeat(b_spatial[:, :, 0], DG, axis=1)[:, None, :]
    nmix = norm_mix[:, None, :]
    nple = norm_ple[:, None, :]
    gn = gnorm_a[:, None, :]
    lng = ln_v_g[:, None, :]
    lnb = ln_v_b[:, None, :]

    if depth % 2 == 1:
        raise NotImplementedError("final norm is fused into the last (chunk-MLP) layer")
    hp = x_prompt
    hs = x_sample.reshape(n_s, D_MODEL)
    ps = p_sample.reshape(depth, n_s, PLE_DIM)
    st_p, vt_p, vt_s = [], [], []
    st_s = None
    for i in range(depth):
        j = i // 2
        last = i == depth - 1
        if i % 2 == 0:
            hp, s_fin = _hgrn_prompt_layer(hp, p_prompt, i, j, lb_logits, nmix, w_in_a_b,
                                           gn, w_out_a_b, nple, w_gate_b, w_proj_b,
                                           tri, lvl)
            st_p.append(s_fin)
            hs, st_s = _hgrn_sample_layer(hs, ps, state_hgrn, st_s, i, j, lb_logits, nmix,
                                          w_in_a_b, gn, w_out_a_b, nple, w_gate_b, w_proj_b)
        else:
            hp, vt = _gmlp_prompt_layer(hp, p_prompt, i, j, nmix, w_in_b_b, lng, lnb,
                                        w_spatial, bias_full, w_out_b_b, nple, w_gate_b,
                                        w_proj_b, nf, last)
            vt_p.append(vt)
            hs, vts = _gmlp_sample_layer(hs, ps, i, j, nmix, w_in_b_b, lng, lnb,
                                         w00, b0, w_out_b_b, nple, w_gate_b, w_proj_b, nf, last)
            vt_s.append(vts.reshape(n_s, 1, D_INNER))
    return (hp, hs.reshape(n_s, 1, D_MODEL), jnp.stack(st_p), st_s,
            jnp.stack(vt_p), jnp.stack(vt_s))
```

```python
import functools
import math

import numpy as np
import jax
import jax.numpy as jnp
from jax import lax
from jax.experimental import pallas as pl
from jax.experimental.pallas import tpu as pltpu

D_MODEL = 1024
D_INNER = 2048
HEADS = 8
DK = 128
DV = 256
GROUPS = 8
DG = 256
B_CHUNK = 128
PLE_DIM = 256
EPS = 1e-6
LOG2_E = 1.4426950408889634

TL = 256
HALF = TL // 2
N_LEVELS = int(math.log2(TL))
WEIGHT_CHUNK_3 = 1536
WEIGHT_CHUNK_1 = 512
VMEM_LIMIT = 58 * 1024 * 1024

F32 = jnp.float32
BF16 = jnp.bfloat16


def _build_level_consts(tl):
    t = np.arange(tl)[:, None]
    s = np.arange(tl)[None, :]
    tri = (s <= t).astype(np.float32)
    x = np.maximum(t ^ s, 1)
    p = np.floor(np.log2(x)).astype(np.int32)
    nlev = int(math.log2(tl))
    lvl = np.where(t == s, 0, np.where(t > s, nlev - p, -1)).astype(np.int32)
    return tri, lvl[:tl // 2, :tl // 2]


_TRI_NP, _LVL_NP = _build_level_consts(TL)


def _dot(a, b):
    return jnp.dot(a, b, preferred_element_type=F32)


def _dot_cols(a, w_refs):
    return jnp.concatenate([_dot(a, w[...]) for w in w_refs], axis=1)


def _dot_nt(a, b):
    return lax.dot_general(a, b, (((1,), (1,)), ((), ())), preferred_element_type=F32)


def _dot_tn(a, b):
    return lax.dot_general(a, b, (((0,), (0,)), ((), ())), preferred_element_type=F32)


def _rmsnorm(x, g):
    ms = jnp.mean(x * x, axis=-1, keepdims=True)
    return x * lax.rsqrt(ms + EPS) * g


def _silu(x):
    return x * jax.nn.sigmoid(x)


def _gelu_tanh(x):
    c = math.sqrt(2.0 / math.pi)
    return 0.5 * x * (1.0 + jnp.tanh(c * (x + 0.044715 * (x * x * x))))


def _forget_lower_bound(lbl, j):
    mx = jnp.max(lbl, axis=0, keepdims=True)
    e = jnp.exp(lbl - mx)
    sm = e / jnp.sum(e, axis=0, keepdims=True)
    cum0 = sm[0:1, :]
    cum = cum0
    for i in range(1, j + 1):
        cum = cum + sm[i:i + 1, :]
    return cum - cum0


def _forget_gates(fpre, lb):
    e = jnp.exp(-jnp.abs(fpre))
    log_sig = jnp.minimum(fpre, 0.0) - jnp.log1p(e)
    a = jnp.log(lb)
    y = jnp.log1p(-lb) + log_sig
    logf = jnp.maximum(a, y) + jnp.log1p(jnp.exp(-jnp.abs(a - y)))
    one_minus_f = (1.0 - lb) * (jnp.where(fpre >= 0.0, e, 1.0) / (1.0 + e))
    return logf, one_minus_f


def _hgrn_front(x, nm, w_in, lb):
    hn = _rmsnorm(x, nm).astype(BF16)
    proj = _dot_cols(hn, w_in)
    q = _silu(proj[:, :D_MODEL])
    logf, kk = _forget_gates(proj[:, D_MODEL:2 * D_MODEL], lb)
    v = proj[:, 2 * D_MODEL:2 * D_MODEL + D_INNER]
    z = proj[:, 2 * D_MODEL + D_INNER:]
    return q, logf, kk, v, z


def _head_rmsnorm(o, gn):
    outs = []
    for h in range(HEADS):
        outs.append(_rmsnorm(o[:, h * DV:(h + 1) * DV], gn))
    return jnp.concatenate(outs, axis=1)


def _ple(h, p, npl, w_gate, w_proj):
    gate = jax.nn.sigmoid(_dot_cols(_rmsnorm(h, npl).astype(BF16), w_gate))
    return h + gate * _dot_cols(p.astype(BF16), w_proj)


def _mixer_back(x, branch, w_out, p, npl, w_gate, w_proj):
    h = x + _dot_cols(branch.astype(BF16), w_out)
    return _ple(h, p, npl, w_gate, w_proj)


def _gmlp_front(x, nm, w_in, ln_g, ln_b):
    hn = _rmsnorm(x, nm).astype(BF16)
    proj = _dot_cols(hn, w_in)
    u = _gelu_tanh(proj[:, :D_INNER])
    vg = _gelu_tanh(proj[:, D_INNER:2 * D_INNER])
    z = proj[:, 2 * D_INNER:]
    mu = jnp.mean(vg, axis=-1, keepdims=True)
    vc = vg - mu
    vn = vc * lax.rsqrt(jnp.mean(vc * vc, axis=-1, keepdims=True) + EPS) * ln_g + ln_b
    return u, vn, z


def _hgrn_prompt_kernel(x_ref, p_ref, lbl_ref, nm_ref, win_ref, gn_ref, wout_ref, npl_ref,
                        wg_ref, wp_ref, tri_ref, lvl_ref, sq_ref, sf_ref, sk_ref, sv_ref, ss_ref,
                        *rest, layer_j):
    y_ref, sfin_ref, ssnew_ref, so_ref, st_ref = rest[-5:]
    l = pl.program_id(1)

    @pl.when(l == 0)
    def _():
        st_ref[...] = jnp.zeros_like(st_ref)

    _sample_state_update(sq_ref, sf_ref, sk_ref, sv_ref, ss_ref, ssnew_ref, so_ref)

    x = x_ref[...]
    lb = _forget_lower_bound(lbl_ref[...], layer_j)
    q, logf, kk, v, z = _hgrn_front(x, nm_ref[...], win_ref, lb)

    lg = logf * LOG2_E
    g_hi = lg.astype(BF16)
    g_lo = (lg - g_hi.astype(F32)).astype(BF16)
    g2 = jnp.concatenate([g_hi, g_lo], axis=1)
    b2 = _dot(tri_ref[...], g2)
    b = b2[:, :D_MODEL] + b2[:, D_MODEL:]
    b_last = b[TL - 1:TL, :]
    qs = (q * jnp.exp2(b)).astype(BF16)
    kd = (kk * jnp.exp2(b_last - b)).astype(BF16)
    dec = jnp.exp2(b_last)
    qb = q.astype(BF16)
    kb = kk.astype(BF16)
    vb = v.astype(BF16)

    zk, zq = {}, {}
    for level in range(1, N_LEVELS + 1):
        hs = TL >> level
        if hs >= 8:
            grp = 2 * hs
            shp = (TL // grp, grp, D_MODEL)
            b3, q3, k3 = b.reshape(shp), q.reshape(shp), kk.reshape(shp)
            mid = b3[:, hs - 1:hs, :]
            zq3 = q3[:, hs:, :] * jnp.exp2(b3[:, hs:, :] - mid)
            zk3 = k3[:, :hs, :] * jnp.exp2(mid - b3[:, :hs, :])
            zq[level] = zq3.reshape(TL // 2, D_MODEL).astype(BF16)
            zk[level] = jnp.concatenate([zk3, zq3], axis=1).reshape(TL, D_MODEL).astype(BF16)
        else:
            shp = (TL // 8, 8, D_MODEL)
            b8, q8, k8 = b.reshape(shp), q.reshape(shp), kk.reshape(shp)
            sub = lax.broadcasted_iota(jnp.int32, shp, 1)
            is_q = (sub & hs) != 0
            if hs == 4:
                mid = jnp.broadcast_to(b8[:, 3:4, :], shp)
            elif hs == 2:
                mid = jnp.where(sub < 4, jnp.broadcast_to(b8[:, 1:2, :], shp),
                                jnp.broadcast_to(b8[:, 5:6, :], shp))
            else:
                mid = None
            if mid is None:
                e = jnp.where(is_q, lg.reshape(shp), 0.0)
            else:
                e = jnp.where(is_q, b8 - mid, mid - b8)
            zfine = jnp.where(is_q, q8, k8) * jnp.exp2(e)
            zk[level] = zfine.reshape(TL, D_MODEL).astype(BF16)

    lvl = lvl_ref[...]
    atts = []
    for h in range(HEADS):
        sl = slice(h * DK, (h + 1) * DK)
        pd = _dot_nt(qb[:, sl], kb[:, sl])
        diag = [jnp.where(lvl == 0, pd[:HALF, :HALF], 0.0),
                jnp.where(lvl == 0, pd[HALF:, HALF:], 0.0)]
        for level in range(N_LEVELS, 1, -1):
            hs = TL >> level
            zkl = zk[level][:, sl]
            if hs >= 8:
                grp = 2 * hs
                nq = HALF // 2
                pr = _dot_nt(zq[level][:, sl], zkl)
                hit = lvl.reshape(HALF // grp, grp, HALF)[:, hs:, :] == level
                for i in range(2):
                    pq = pr[i * nq:(i + 1) * nq, i * HALF:(i + 1) * HALF]
                    d3 = diag[i].reshape(HALF // grp, grp, HALF)
                    upper = jnp.where(hit, pq.reshape(HALF // grp, hs, HALF), d3[:, hs:, :])
                    diag[i] = jnp.concatenate([d3[:, :hs, :], upper], axis=1).reshape(HALF, HALF)
            else:
                pf = _dot_nt(zkl, zkl)
                hit = lvl == level
                diag[0] = jnp.where(hit, pf[:HALF, :HALF], diag[0])
                diag[1] = jnp.where(hit, pf[HALF:, HALF:], diag[1])
        a10 = _dot_nt(zq[1][:, sl], zk[1][:HALF, sl])
        att = jnp.concatenate(
            [jnp.concatenate([diag[0], jnp.zeros((HALF, HALF), F32)], axis=1),
             jnp.concatenate([a10, diag[1]], axis=1)], axis=0)
        atts.append(att.astype(BF16))
    outs = []
    for h in range(HEADS):
        sl = slice(h * DK, (h + 1) * DK)
        vsl = slice(h * DV, (h + 1) * DV)
        st = st_ref[h]
        o = _dot_nt(qs[:, sl], st.astype(BF16)) + _dot(atts[h], vb[:, vsl])
        st_ref[h] = st * dec[:, sl] + _dot_tn(vb[:, vsl], kd[:, sl])
        outs.append(o)
    o = jnp.concatenate(outs, axis=1)

    branch = _head_rmsnorm(o, gn_ref[...]) * _silu(z)
    y_ref[...] = _mixer_back(x, branch, wout_ref, p_ref[...], npl_ref[...],
                             wg_ref, wp_ref)

    @pl.when(l == pl.num_programs(1) - 1)
    def _():
        for h in range(HEADS):
            sfin_ref[h] = st_ref[h].T


def _gmlp_prompt_kernel(x_ref, p_ref, nm_ref, win_ref, lng_ref, lnb_ref, wsp_ref, bsp_ref,
                        wout_ref, npl_ref, wg_ref, wp_ref, nf_ref, y_ref, vt_ref, *, final_norm):
    l = pl.program_id(1)
    x = x_ref[...]
    u, vn, z = _gmlp_front(x, nm_ref[...], win_ref, lng_ref[...], lnb_ref[...])
    vb = vn.astype(BF16)
    r = lax.broadcasted_iota(jnp.int32, (B_CHUNK, B_CHUNK), 0)
    c = lax.broadcasted_iota(jnp.int32, (B_CHUNK, B_CHUNK), 1)
    tril = c <= r
    bias = bsp_ref[...]
    cols = []
    for g in range(GROUPS):
        w = jnp.where(tril, wsp_ref[g], 0.0).astype(BF16)
        gsl = slice(g * DG, (g + 1) * DG)
        chunks = []
        for ci in range(TL // B_CHUNK):
            rsl = slice(ci * B_CHUNK, (ci + 1) * B_CHUNK)
            chunks.append(_dot(w, vb[rsl, gsl]) + bias[:, gsl])
        cols.append(jnp.concatenate(chunks, axis=0))
    s = jnp.concatenate(cols, axis=1)
    branch = u * s * _silu(z)
    h = _mixer_back(x, branch, wout_ref, p_ref[...], npl_ref[...], wg_ref, wp_ref)
    if final_norm:
        h = _rmsnorm(h, nf_ref[...])
    y_ref[...] = h

    @pl.when(l == pl.num_programs(1) - 1)
    def _():
        vt_ref[...] = vn[TL - B_CHUNK:, :]


def _hgrn_sample_front_kernel(x_ref, lbl_ref, nm_ref, win_ref, q_ref, f_ref, k_ref, v_ref, z_ref,
                              *, layer_j):
    lb = _forget_lower_bound(lbl_ref[...], layer_j)
    q, logf, kk, v, z = _hgrn_front(x_ref[...], nm_ref[...], win_ref, lb)
    q_ref[...] = q
    f_ref[...] = jnp.exp(logf)
    k_ref[...] = kk
    v_ref[...] = v
    z_ref[...] = z


def _sample_state_update(q_ref, f_ref, k_ref, v_ref, s_ref, snew_ref, o_ref):
    for i in range(q_ref.shape[0]):
        q_t = q_ref[i].T
        f_t = f_ref[i].T
        k_t = k_ref[i].T
        for h in range(HEADS):
            s_new = f_t[:, h:h + 1] * s_ref[i, h] + k_t[:, h:h + 1] * v_ref[i, h:h + 1, :]
            snew_ref[i, h] = s_new
            o_ref[i, h:h + 1, :] = jnp.sum(q_t[:, h:h + 1] * s_new, axis=0, keepdims=True)


def _hgrn_sample_back_kernel(x_ref, o_ref, z_ref, p_ref, gn_ref, wout_ref, npl_ref, wg_ref,
                             wp_ref, y_ref):
    branch = _head_rmsnorm(o_ref[...], gn_ref[...]) * _silu(z_ref[...])
    y_ref[...] = _mixer_back(x_ref[...], branch, wout_ref, p_ref[...], npl_ref[...],
                             wg_ref, wp_ref)


def _gmlp_sample_kernel(x_ref, p_ref, nm_ref, win_ref, lng_ref, lnb_ref, w00_ref, b0_ref,
                        wout_ref, npl_ref, wg_ref, wp_ref, nf_ref, y_ref, vt_ref, *, final_norm):
    x = x_ref[...]
    u, vn, z = _gmlp_front(x, nm_ref[...], win_ref, lng_ref[...], lnb_ref[...])
    s = w00_ref[...] * vn + b0_ref[...]
    branch = u * s * _silu(z)
    h = _mixer_back(x, branch, wout_ref, p_ref[...], npl_ref[...], wg_ref, wp_ref)
    if final_norm:
        h = _rmsnorm(h, nf_ref[...])
    y_ref[...] = h
    vt_ref[...] = vn


def _resident(shape, index):
    return pl.BlockSpec(shape, index, pipeline_mode=pl.Buffered(1))


def _chunk_width(n):
    width = WEIGHT_CHUNK_3 if n % WEIGHT_CHUNK_3 == 0 else WEIGHT_CHUNK_1
    assert n % width == 0 and width % 1024 != 0
    return width


def _weight_cols(w, layer):
    _, k, n = w.shape
    width = _chunk_width(n)
    specs = [_resident((None, k, width), lambda *_, c=c: (layer, 0, c)) for c in range(n // width)]
    return [w] * len(specs), specs


def _expand(args):
    ops, specs, sizes = [], [], []
    for o, s in args:
        if isinstance(o, list):
            ops += o
            specs += s
            sizes.append(len(o))
        else:
            ops.append(o)
            specs.append(s)
            sizes.append(None)
    return ops, specs, sizes


def _regroup(fn, sizes):
    def wrapped(*refs):
        pos, grouped = 0, []
        for n in sizes:
            if n is None:
                grouped.append(refs[pos])
                pos += 1
            else:
                grouped.append(tuple(refs[pos:pos + n]))
                pos += n
        return fn(*grouped, *refs[pos:])
    return wrapped


def _params(*sem):
    return pltpu.CompilerParams(dimension_semantics=sem, vmem_limit_bytes=VMEM_LIMIT)


def _hgrn_prompt_layer(x, p_all, i, j, lb_logits, norm_mix, w_in, gnorm, w_out, norm_ple,
                       w_gate, w_proj, tri, lvl, sample_qfkv, s0, s_prev):
    bsz, seq, _ = x.shape
    n_a = lb_logits.shape[0]
    n_l = seq // TL
    grid = (bsz, n_l)
    n_s = s0.shape[1]
    assert n_s % (bsz * n_l) == 0
    spt = n_s // (bsz * n_l)
    const2 = lambda b, l: (0, 0)
    s_k = pl.BlockSpec((spt, HEADS, DK), lambda b, l: (b * n_l + l, 0, 0))
    s_v = pl.BlockSpec((spt, HEADS, DV), lambda b, l: (b * n_l + l, 0, 0))
    s_st = pl.BlockSpec((None, spt, HEADS, DK, DV), lambda b, l: (j, b * n_l + l, 0, 0, 0))
    sq, sf, sk, sv = sample_qfkv
    operands, in_specs, sizes = _expand([
        (x, pl.BlockSpec((None, TL, D_MODEL), lambda b, l: (b, l, 0))),
        (p_all, pl.BlockSpec((None, None, TL, PLE_DIM), lambda b, l: (i, b, l, 0))),
        (lb_logits, _resident((n_a, D_MODEL), const2)),
        (norm_mix, _resident((None, 1, D_MODEL), lambda b, l: (i, 0, 0))),
        _weight_cols(w_in, j),
        (gnorm, _resident((None, 1, DV), lambda b, l: (j, 0, 0))),
        _weight_cols(w_out, j),
        (norm_ple, _resident((None, 1, D_MODEL), lambda b, l: (i, 0, 0))),
        _weight_cols(w_gate, i),
        _weight_cols(w_proj, i),
        (tri, _resident(tri.shape, const2)),
        (lvl, _resident(lvl.shape, const2)),
        (sq, s_k), (sf, s_k), (sk, s_k), (sv, s_v), (s0, s_st),
    ])
    aliases = {}
    if s_prev is not None:
        operands.append(s_prev)
        in_specs.append(pl.BlockSpec(memory_space=pl.ANY))
        aliases = {len(operands) - 1: 2}
    out_specs = [
        pl.BlockSpec((None, TL, D_MODEL), lambda b, l: (b, l, 0)),
        pl.BlockSpec((None, HEADS, DK, DV), lambda b, l: (b, 0, 0, 0)),
        s_st,
        s_v,
    ]
    out_shape = [
        jax.ShapeDtypeStruct((bsz, seq, D_MODEL), F32),
        jax.ShapeDtypeStruct((bsz, HEADS, DK, DV), F32),
        jax.ShapeDtypeStruct(s0.shape, F32),
        jax.ShapeDtypeStruct((n_s, HEADS, DV), F32),
    ]
    return pl.pallas_call(
        _regroup(functools.partial(_hgrn_prompt_kernel, layer_j=j), sizes),
        grid=grid, in_specs=in_specs, out_specs=out_specs, out_shape=out_shape,
        scratch_shapes=[pltpu.VMEM((HEADS, DV, DK), F32)],
        input_output_aliases=aliases,
        compiler_params=_params("arbitrary", "arbitrary"),
        name=f"hgrn_prompt_{j}",
    )(*operands)


def _gmlp_prompt_layer(x, p_all, i, j, norm_mix, w_in, ln_g, ln_b, w_sp, bias_full, w_out,
                       norm_ple, w_gate, w_proj, norm_final, final_norm):
    bsz, seq, _ = x.shape
    grid = (bsz, seq // TL)
    operands, in_specs, sizes = _expand([
        (x, pl.BlockSpec((None, TL, D_MODEL), lambda b, l: (b, l, 0))),
        (p_all, pl.BlockSpec((None, None, TL, PLE_DIM), lambda b, l: (i, b, l, 0))),
        (norm_mix, _resident((None, 1, D_MODEL), lambda b, l: (i, 0, 0))),
        _weight_cols(w_in, j),
        (ln_g, _resident((None, 1, D_INNER), lambda b, l: (j, 0, 0))),
        (ln_b, _resident((None, 1, D_INNER), lambda b, l: (j, 0, 0))),
        (w_sp, _resident((None, GROUPS, B_CHUNK, B_CHUNK), lambda b, l: (j, 0, 0, 0))),
        (bias_full, _resident((None, B_CHUNK, D_INNER), lambda b, l: (j, 0, 0))),
        _weight_cols(w_out, j),
        (norm_ple, _resident((None, 1, D_MODEL), lambda b, l: (i, 0, 0))),
        _weight_cols(w_gate, i),
        _weight_cols(w_proj, i),
        (norm_final, _resident((1, D_MODEL), lambda b, l: (0, 0))),
    ])
    out_specs = [
        pl.BlockSpec((None, TL, D_MODEL), lambda b, l: (b, l, 0)),
        pl.BlockSpec((None, B_CHUNK, D_INNER), lambda b, l: (b, 0, 0)),
    ]
    out_shape = [
        jax.ShapeDtypeStruct((bsz, seq, D_MODEL), F32),
        jax.ShapeDtypeStruct((bsz, B_CHUNK, D_INNER), F32),
    ]
    return pl.pallas_call(
        _regroup(functools.partial(_gmlp_prompt_kernel, final_norm=final_norm), sizes),
        grid=grid, in_specs=in_specs, out_specs=out_specs, out_shape=out_shape,
        compiler_params=_params("arbitrary", "arbitrary"),
        name=f"gmlp_prompt_{j}",
    )(*operands)


def _whole(shape):
    nd = len(shape)
    return pl.BlockSpec(shape, lambda *_: (0,) * nd)


def _hgrn_sample_front(x, i, j, lb_logits, norm_mix, w_in):
    n = x.shape[0]
    n_a = lb_logits.shape[0]
    operands, in_specs, sizes = _expand([
        (x, _whole((n, D_MODEL))),
        (lb_logits, _whole((n_a, D_MODEL))),
        (norm_mix, pl.BlockSpec((None, 1, D_MODEL), lambda g: (i, 0, 0))),
        _weight_cols(w_in, j),
    ])
    q, f, kk, v, z = pl.pallas_call(
        _regroup(functools.partial(_hgrn_sample_front_kernel, layer_j=j), sizes),
        grid=(1,),
        in_specs=in_specs,
        out_specs=[_whole((n, D_MODEL))] * 3 + [_whole((n, D_INNER))] * 2,
        out_shape=[jax.ShapeDtypeStruct((n, D_MODEL), F32)] * 3
        + [jax.ShapeDtypeStruct((n, D_INNER), F32)] * 2,
        compiler_params=_params("arbitrary"),
        name=f"hgrn_sample_front_{j}",
    )(*operands)

    qfkv = (q.reshape(n, HEADS, DK), f.reshape(n, HEADS, DK), kk.reshape(n, HEADS, DK),
            v.reshape(n, HEADS, DV))
    return qfkv, z


def _hgrn_sample_back(x, o, z, p_all, i, j, gnorm, w_out, norm_ple, w_gate, w_proj):
    n = x.shape[0]
    operands, in_specs, sizes = _expand([
        (x, _whole((n, D_MODEL))),
        (o.reshape(n, D_INNER), _whole((n, D_INNER))),
        (z, _whole((n, D_INNER))),
        (p_all, pl.BlockSpec((None, n, PLE_DIM), lambda g: (i, 0, 0))),
        (gnorm, pl.BlockSpec((None, 1, DV), lambda g: (j, 0, 0))),
        _weight_cols(w_out, j),
        (norm_ple, pl.BlockSpec((None, 1, D_MODEL), lambda g: (i, 0, 0))),
        _weight_cols(w_gate, i),
        _weight_cols(w_proj, i),
    ])
    return pl.pallas_call(
        _regroup(_hgrn_sample_back_kernel, sizes),
        grid=(1,),
        in_specs=in_specs,
        out_specs=_whole((n, D_MODEL)),
        out_shape=jax.ShapeDtypeStruct((n, D_MODEL), F32),
        compiler_params=_params("arbitrary"),
        name=f"hgrn_sample_back_{j}",
    )(*operands)


def _gmlp_sample_layer(x, p_all, i, j, norm_mix, w_in, ln_g, ln_b, w00, b0, w_out, norm_ple,
                       w_gate, w_proj, norm_final, final_norm):
    n = x.shape[0]
    vec = lambda layer, width: pl.BlockSpec((None, 1, width), lambda g: (layer, 0, 0))
    operands, in_specs, sizes = _expand([
        (x, _whole((n, D_MODEL))),
        (p_all, pl.BlockSpec((None, n, PLE_DIM), lambda g: (i, 0, 0))),
        (norm_mix, vec(i, D_MODEL)),
        _weight_cols(w_in, j),
        (ln_g, vec(j, D_INNER)),
        (ln_b, vec(j, D_INNER)),
        (w00, vec(j, D_INNER)),
        (b0, vec(j, D_INNER)),
        _weight_cols(w_out, j),
        (norm_ple, vec(i, D_MODEL)),
        _weight_cols(w_gate, i),
        _weight_cols(w_proj, i),
        (norm_final, _whole((1, D_MODEL))),
    ])
    return pl.pallas_call(
        _regroup(functools.partial(_gmlp_sample_kernel, final_norm=final_norm), sizes),
        grid=(1,),
        in_specs=in_specs,
        out_specs=[_whole((n, D_MODEL)), _whole((n, D_INNER))],
        out_shape=[jax.ShapeDtypeStruct((n, D_MODEL), F32),
                   jax.ShapeDtypeStruct((n, D_INNER), F32)],
        compiler_params=_params("arbitrary"),
        name=f"gmlp_sample_{j}",
    )(*operands)


def kernel(x_prompt, x_sample, state_hgrn, p_prompt, p_sample, norm_mix, w_in_a, lb_logits,
           gnorm_a, w_out_a, w_in_b, ln_v_g, ln_v_b, w_spatial, b_spatial, w_out_b,
           norm_ple, w_ple_gate, w_ple_proj, norm_final):
    depth = norm_mix.shape[0]
    n_s = x_sample.shape[0]
    assert x_prompt.shape[1] % TL == 0 and TL % B_CHUNK == 0 and x_sample.shape[1] == 1

    w_in_a_b = w_in_a.astype(BF16)
    w_out_a_b = w_out_a.astype(BF16)
    w_in_b_b = w_in_b.astype(BF16)
    w_out_b_b = w_out_b.astype(BF16)
    w_gate_b = w_ple_gate.astype(BF16)
    w_proj_b = w_ple_proj.astype(BF16)

    tri = jnp.asarray(_TRI_NP, dtype=BF16)
    lvl = jnp.asarray(_LVL_NP)
    nf = norm_final.reshape(1, D_MODEL)
    bias_full = jnp.repeat(jnp.swapaxes(b_spatial, 1, 2), DG, axis=2)
    w00 = jnp.repeat(w_spatial[:, :, 0, 0], DG, axis=1)[:, None, :]
    b0 = jnp.repeat(b_spatial[:, :, 0], DG, axis=1)[:, None, :]
    nmix = norm_mix[:, None, :]
    nple = norm_ple[:, None, :]
    gn = gnorm_a[:, None, :]
    lng = ln_v_g[:, None, :]
    lnb = ln_v_b[:, None, :]

    if depth % 2 == 1:
        raise NotImplementedError("final norm is fused into the last (chunk-MLP) layer")
    hp = x_prompt
    hs = x_sample.reshape(n_s, D_MODEL)
    ps = p_sample.reshape(depth, n_s, PLE_DIM)
    st_p, vt_p, vt_s = [], [], []
    st_s = None
    for i in range(depth):
        j = i // 2
        last = i == depth - 1
        if i % 2 == 0:
            qfkv, z_s = _hgrn_sample_front(hs, i, j, lb_logits, nmix, w_in_a_b)
            hp, s_fin, st_s, o_s = _hgrn_prompt_layer(
                hp, p_prompt, i, j, lb_logits, nmix, w_in_a_b, gn, w_out_a_b, nple, w_gate_b,
                w_proj_b, tri, lvl, qfkv, state_hgrn, st_s)
            st_p.append(s_fin)
            hs = _hgrn_sample_back(hs, o_s, z_s, ps, i, j, gn, w_out_a_b, nple, w_gate_b,
                                   w_proj_b)
        else:
            hp, vt = _gmlp_prompt_layer(hp, p_prompt, i, j, nmix, w_in_b_b, lng, lnb,
                                        w_spatial, bias_full, w_out_b_b, nple, w_gate_b,
                                        w_proj_b, nf, last)
            vt_p.append(vt)
            hs, vts = _gmlp_sample_layer(hs, ps, i, j, nmix, w_in_b_b, lng, lnb,
                                         w00, b0, w_out_b_b, nple, w_gate_b, w_proj_b, nf, last)
            vt_s.append(vts.reshape(n_s, 1, D_INNER))
    return (hp, hs.reshape(n_s, 1, D_MODEL), jnp.stack(st_p), st_s,
            jnp.stack(vt_p), jnp.stack(vt_s))
```

```python
import functools
import math

import numpy as np
import jax
import jax.numpy as jnp
from jax import lax
from jax.experimental import pallas as pl
from jax.experimental.pallas import tpu as pltpu

D_MODEL = 1024
D_INNER = 2048
HEADS = 8
DK = 128
DV = 256
GROUPS = 8
DG = 256
B_CHUNK = 128
PLE_DIM = 256
EPS = 1e-6
LOG2_E = 1.4426950408889634

TL = 256
HALF = TL // 2
N_LEVELS = int(math.log2(TL))
WEIGHT_CHUNK_3 = 1536
WEIGHT_CHUNK_1 = 512
VMEM_LIMIT = 58 * 1024 * 1024

F32 = jnp.float32
BF16 = jnp.bfloat16


def _build_level_consts(tl):
    t = np.arange(tl)[:, None]
    s = np.arange(tl)[None, :]
    tri = (s <= t).astype(np.float32)
    x = np.maximum(t ^ s, 1)
    p = np.floor(np.log2(x)).astype(np.int32)
    nlev = int(math.log2(tl))
    lvl = np.where(t == s, 0, np.where(t > s, nlev - p, -1)).astype(np.int32)
    return tri, lvl[:tl // 2, :tl // 2]


_TRI_NP, _LVL_NP = _build_level_consts(TL)


def _dot(a, b):
    return jnp.dot(a, b, preferred_element_type=F32)


def _dot_cols(a, w_refs):
    return jnp.concatenate([_dot(a, w[...]) for w in w_refs], axis=1)


def _dot_nt(a, b):
    return lax.dot_general(a, b, (((1,), (1,)), ((), ())), preferred_element_type=F32)


def _dot_tn(a, b):
    return lax.dot_general(a, b, (((0,), (0,)), ((), ())), preferred_element_type=F32)


def _rmsnorm(x, g):
    ms = jnp.mean(x * x, axis=-1, keepdims=True)
    return x * lax.rsqrt(ms + EPS) * g


def _silu(x):
    return x * jax.nn.sigmoid(x)


def _gelu_tanh(x):
    c = math.sqrt(2.0 / math.pi)
    return 0.5 * x * (1.0 + jnp.tanh(c * (x + 0.044715 * (x * x * x))))


def _forget_lower_bound(lbl, j):
    mx = jnp.max(lbl, axis=0, keepdims=True)
    e = jnp.exp(lbl - mx)
    sm = e / jnp.sum(e, axis=0, keepdims=True)
    cum0 = sm[0:1, :]
    cum = cum0
    for i in range(1, j + 1):
        cum = cum + sm[i:i + 1, :]
    return cum - cum0


def _forget_gates(fpre, lb):
    e = jnp.exp(-jnp.abs(fpre))
    log_sig = jnp.minimum(fpre, 0.0) - jnp.log1p(e)
    a = jnp.log(lb)
    y = jnp.log1p(-lb) + log_sig
    logf = jnp.maximum(a, y) + jnp.log1p(jnp.exp(-jnp.abs(a - y)))
    one_minus_f = (1.0 - lb) * (jnp.where(fpre >= 0.0, e, 1.0) / (1.0 + e))
    return logf, one_minus_f


def _hgrn_front(x, nm, w_in, lb):
    hn = _rmsnorm(x, nm).astype(BF16)
    proj = _dot_cols(hn, w_in)
    q = _silu(proj[:, :D_MODEL])
    logf, kk = _forget_gates(proj[:, D_MODEL:2 * D_MODEL], lb)
    v = proj[:, 2 * D_MODEL:2 * D_MODEL + D_INNER]
    z = proj[:, 2 * D_MODEL + D_INNER:]
    return q, logf, kk, v, z


def _head_rmsnorm(o, gn):
    outs = []
    for h in range(HEADS):
        outs.append(_rmsnorm(o[:, h * DV:(h + 1) * DV], gn))
    return jnp.concatenate(outs, axis=1)


def _ple(h, p, npl, w_gate, w_proj):
    gate = jax.nn.sigmoid(_dot_cols(_rmsnorm(h, npl).astype(BF16), w_gate))
    return h + gate * _dot_cols(p.astype(BF16), w_proj)


def _mixer_back(x, branch, w_out, p, npl, w_gate, w_proj):
    h = x + _dot_cols(branch.astype(BF16), w_out)
    return _ple(h, p, npl, w_gate, w_proj)


def _gmlp_front(x, nm, w_in, ln_g, ln_b):
    hn = _rmsnorm(x, nm).astype(BF16)
    proj = _dot_cols(hn, w_in)
    u = _gelu_tanh(proj[:, :D_INNER])
    vg = _gelu_tanh(proj[:, D_INNER:2 * D_INNER])
    z = proj[:, 2 * D_INNER:]
    mu = jnp.mean(vg, axis=-1, keepdims=True)
    vc = vg - mu
    vn = vc * lax.rsqrt(jnp.mean(vc * vc, axis=-1, keepdims=True) + EPS) * ln_g + ln_b
    return u, vn, z


def _hgrn_prompt_kernel(x_ref, p_ref, lbl_ref, nm_ref, win_ref, gn_ref, wout_ref, npl_ref,
                        wg_ref, wp_ref, tri_ref, lvl_ref, sq_ref, sf_ref, sk_ref, sv_ref, ss_ref,
                        nxt_refs, *rest, layer_j):
    n_out = 4 + len(nxt_refs)
    y_ref, sfin_ref, ssnew_ref, so_ref = rest[len(rest) - n_out - 1:len(rest) - n_out + 3]
    st_ref = rest[-1]
    l = pl.program_id(1)
    if nxt_refs:
        _cast_next_weights(pl.program_id(0) * pl.num_programs(1) + l, nxt_refs,
                           rest[len(rest) - n_out + 3:len(rest) - 1])

    @pl.when(l == 0)
    def _():
        st_ref[...] = jnp.zeros_like(st_ref)

    _sample_state_update(sq_ref, sf_ref, sk_ref, sv_ref, ss_ref, ssnew_ref, so_ref)

    x = x_ref[...]
    lb = _forget_lower_bound(lbl_ref[...], layer_j)
    q, logf, kk, v, z = _hgrn_front(x, nm_ref[...], win_ref, lb)

    lg = logf * LOG2_E
    g_hi = lg.astype(BF16)
    g_lo = (lg - g_hi.astype(F32)).astype(BF16)
    g2 = jnp.concatenate([g_hi, g_lo], axis=1)
    b2 = _dot(tri_ref[...], g2)
    b = b2[:, :D_MODEL] + b2[:, D_MODEL:]
    b_last = b[TL - 1:TL, :]
    qs = (q * jnp.exp2(b)).astype(BF16)
    kd = (kk * jnp.exp2(b_last - b)).astype(BF16)
    dec = jnp.exp2(b_last)
    qb = q.astype(BF16)
    kb = kk.astype(BF16)
    vb = v.astype(BF16)

    zk, zq = {}, {}
    for level in range(1, N_LEVELS + 1):
        hs = TL >> level
        if hs >= 8:
            grp = 2 * hs
            shp = (TL // grp, grp, D_MODEL)
            b3, q3, k3 = b.reshape(shp), q.reshape(shp), kk.reshape(shp)
            mid = b3[:, hs - 1:hs, :]
            zq3 = q3[:, hs:, :] * jnp.exp2(b3[:, hs:, :] - mid)
            zk3 = k3[:, :hs, :] * jnp.exp2(mid - b3[:, :hs, :])
            zq[level] = zq3.reshape(TL // 2, D_MODEL).astype(BF16)
            zk[level] = jnp.concatenate([zk3, zq3], axis=1).reshape(TL, D_MODEL).astype(BF16)
        else:
            shp = (TL // 8, 8, D_MODEL)
            b8, q8, k8 = b.reshape(shp), q.reshape(shp), kk.reshape(shp)
            sub = lax.broadcasted_iota(jnp.int32, shp, 1)
            is_q = (sub & hs) != 0
            if hs == 4:
                mid = jnp.broadcast_to(b8[:, 3:4, :], shp)
            elif hs == 2:
                mid = jnp.where(sub < 4, jnp.broadcast_to(b8[:, 1:2, :], shp),
                                jnp.broadcast_to(b8[:, 5:6, :], shp))
            else:
                mid = None
            if mid is None:
                e = jnp.where(is_q, lg.reshape(shp), 0.0)
            else:
                e = jnp.where(is_q, b8 - mid, mid - b8)
            zfine = jnp.where(is_q, q8, k8) * jnp.exp2(e)
            zk[level] = zfine.reshape(TL, D_MODEL).astype(BF16)

    lvl = lvl_ref[...]
    atts = []
    for h in range(HEADS):
        sl = slice(h * DK, (h + 1) * DK)
        pd = _dot_nt(qb[:, sl], kb[:, sl])
        diag = [jnp.where(lvl == 0, pd[:HALF, :HALF], 0.0),
                jnp.where(lvl == 0, pd[HALF:, HALF:], 0.0)]
        for level in range(N_LEVELS, 1, -1):
            hs = TL >> level
            zkl = zk[level][:, sl]
            if hs >= 8:
                grp = 2 * hs
                nq = HALF // 2
                pr = _dot_nt(zq[level][:, sl], zkl)
                hit = lvl.reshape(HALF // grp, grp, HALF)[:, hs:, :] == level
                for i in range(2):
                    pq = pr[i * nq:(i + 1) * nq, i * HALF:(i + 1) * HALF]
                    d3 = diag[i].reshape(HALF // grp, grp, HALF)
                    upper = jnp.where(hit, pq.reshape(HALF // grp, hs, HALF), d3[:, hs:, :])
                    diag[i] = jnp.concatenate([d3[:, :hs, :], upper], axis=1).reshape(HALF, HALF)
            else:
                pf = _dot_nt(zkl, zkl)
                hit = lvl == level
                diag[0] = jnp.where(hit, pf[:HALF, :HALF], diag[0])
                diag[1] = jnp.where(hit, pf[HALF:, HALF:], diag[1])
        a10 = _dot_nt(zq[1][:, sl], zk[1][:HALF, sl])
        att = jnp.concatenate(
            [jnp.concatenate([diag[0], jnp.zeros((HALF, HALF), F32)], axis=1),
             jnp.concatenate([a10, diag[1]], axis=1)], axis=0)
        atts.append(att.astype(BF16))
    outs = []
    for h in range(HEADS):
        sl = slice(h * DK, (h + 1) * DK)
        vsl = slice(h * DV, (h + 1) * DV)
        st = st_ref[h]
        o = _dot_nt(qs[:, sl], st.astype(BF16)) + _dot(atts[h], vb[:, vsl])
        st_ref[h] = st * dec[:, sl] + _dot_tn(vb[:, vsl], kd[:, sl])
        outs.append(o)
    o = jnp.concatenate(outs, axis=1)

    branch = _head_rmsnorm(o, gn_ref[...]) * _silu(z)
    y_ref[...] = _mixer_back(x, branch, wout_ref, p_ref[...], npl_ref[...],
                             wg_ref, wp_ref)

    @pl.when(l == pl.num_programs(1) - 1)
    def _():
        for h in range(HEADS):
            sfin_ref[h] = st_ref[h].T


def _gmlp_prompt_kernel(x_ref, p_ref, nm_ref, win_ref, lng_ref, lnb_ref, wsp_ref, bsp_ref,
                        wout_ref, npl_ref, wg_ref, wp_ref, nf_ref, nxt_refs, *rest, final_norm):
    n_out = 2 + len(nxt_refs)
    y_ref, vt_ref = rest[len(rest) - n_out:len(rest) - n_out + 2]
    l = pl.program_id(1)
    if nxt_refs:
        _cast_next_weights(pl.program_id(0) * pl.num_programs(1) + l, nxt_refs,
                           rest[len(rest) - n_out + 2:])
    x = x_ref[...]
    u, vn, z = _gmlp_front(x, nm_ref[...], win_ref, lng_ref[...], lnb_ref[...])
    vb = vn.astype(BF16)
    r = lax.broadcasted_iota(jnp.int32, (B_CHUNK, B_CHUNK), 0)
    c = lax.broadcasted_iota(jnp.int32, (B_CHUNK, B_CHUNK), 1)
    tril = c <= r
    bias = bsp_ref[...]
    cols = []
    for g in range(GROUPS):
        w = jnp.where(tril, wsp_ref[g], 0.0).astype(BF16)
        gsl = slice(g * DG, (g + 1) * DG)
        chunks = []
        for ci in range(TL // B_CHUNK):
            rsl = slice(ci * B_CHUNK, (ci + 1) * B_CHUNK)
            chunks.append(_dot(w, vb[rsl, gsl]) + bias[:, gsl])
        cols.append(jnp.concatenate(chunks, axis=0))
    s = jnp.concatenate(cols, axis=1)
    branch = u * s * _silu(z)
    h = _mixer_back(x, branch, wout_ref, p_ref[...], npl_ref[...], wg_ref, wp_ref)
    if final_norm:
        h = _rmsnorm(h, nf_ref[...])
    y_ref[...] = h

    @pl.when(l == pl.num_programs(1) - 1)
    def _():
        vt_ref[...] = vn[TL - B_CHUNK:, :]


def _hgrn_sample_front_kernel(x_ref, lbl_ref, nm_ref, win_ref, q_ref, f_ref, k_ref, v_ref, z_ref,
                              *, layer_j):
    lb = _forget_lower_bound(lbl_ref[...], layer_j)
    q, logf, kk, v, z = _hgrn_front(x_ref[...], nm_ref[...], win_ref, lb)
    q_ref[...] = q
    f_ref[...] = jnp.exp(logf)
    k_ref[...] = kk
    v_ref[...] = v
    z_ref[...] = z


def _sample_state_update(q_ref, f_ref, k_ref, v_ref, s_ref, snew_ref, o_ref):
    for i in range(q_ref.shape[0]):
        q_t = q_ref[i].T
        f_t = f_ref[i].T
        k_t = k_ref[i].T
        for h in range(HEADS):
            s_new = f_t[:, h:h + 1] * s_ref[i, h] + k_t[:, h:h + 1] * v_ref[i, h:h + 1, :]
            snew_ref[i, h] = s_new
            o_ref[i, h:h + 1, :] = jnp.sum(q_t[:, h:h + 1] * s_new, axis=0, keepdims=True)


def _cast_next_weights(step, nxt_refs, out_refs):
    for src, dst in zip(nxt_refs[:3], out_refs[:3]):
        dst[...] = src[...].astype(BF16)

    @pl.when(step == 0)
    def _():
        out_refs[3][...] = nxt_refs[3][...].astype(BF16)


def _hgrn_sample_back_kernel(x_ref, o_ref, z_ref, p_ref, gn_ref, wout_ref, npl_ref, wg_ref,
                             wp_ref, y_ref):
    branch = _head_rmsnorm(o_ref[...], gn_ref[...]) * _silu(z_ref[...])
    y_ref[...] = _mixer_back(x_ref[...], branch, wout_ref, p_ref[...], npl_ref[...],
                             wg_ref, wp_ref)


def _gmlp_sample_kernel(x_ref, p_ref, nm_ref, win_ref, lng_ref, lnb_ref, w00_ref, b0_ref,
                        wout_ref, npl_ref, wg_ref, wp_ref, nf_ref, y_ref, vt_ref, *, final_norm):
    x = x_ref[...]
    u, vn, z = _gmlp_front(x, nm_ref[...], win_ref, lng_ref[...], lnb_ref[...])
    s = w00_ref[...] * vn + b0_ref[...]
    branch = u * s * _silu(z)
    h = _mixer_back(x, branch, wout_ref, p_ref[...], npl_ref[...], wg_ref, wp_ref)
    if final_norm:
        h = _rmsnorm(h, nf_ref[...])
    y_ref[...] = h
    vt_ref[...] = vn


def _resident(shape, index):
    return pl.BlockSpec(shape, index, pipeline_mode=pl.Buffered(1))


def _chunk_width(n):
    width = WEIGHT_CHUNK_3 if n % WEIGHT_CHUNK_3 == 0 else WEIGHT_CHUNK_1
    assert n % width == 0 and width % 1024 != 0
    return width


def _weight_cols(w):
    k, n = w.shape
    width = _chunk_width(n)
    specs = [_resident((k, width), lambda *_, c=c: (0, c)) for c in range(n // width)]
    return [w] * len(specs), specs


def _next_weight_specs(nxt, n_steps, step_of):
    ops, ins, outs, shapes = [], [], [], []
    for idx, (stack, layer) in enumerate(nxt or ()):
        _, k, n = stack.shape
        ops.append(stack)
        shapes.append(jax.ShapeDtypeStruct((k, n), BF16))
        if idx < 3:
            rows = k // n_steps
            assert rows * n_steps == k and rows % 16 == 0
            ins.append(pl.BlockSpec((None, rows, n),
                                    lambda *g, layer=layer: (layer, step_of(*g), 0)))
            outs.append(pl.BlockSpec((rows, n), lambda *g: (step_of(*g), 0)))
        else:
            ins.append(_resident((None, k, n), lambda *g, layer=layer: (layer, 0, 0)))
            outs.append(pl.BlockSpec((k, n), lambda *g: (0, 0)))
    return ops, ins, outs, shapes


def _expand(args):
    ops, specs, sizes = [], [], []
    for o, s in args:
        if isinstance(o, list):
            ops += o
            specs += s
            sizes.append(len(o))
        else:
            ops.append(o)
            specs.append(s)
            sizes.append(None)
    return ops, specs, sizes


def _regroup(fn, sizes):
    def wrapped(*refs):
        pos, grouped = 0, []
        for n in sizes:
            if n is None:
                grouped.append(refs[pos])
                pos += 1
            else:
                grouped.append(tuple(refs[pos:pos + n]))
                pos += n
        return fn(*grouped, *refs[pos:])
    return wrapped


def _params(*sem):
    return pltpu.CompilerParams(dimension_semantics=sem, vmem_limit_bytes=VMEM_LIMIT)


def _hgrn_prompt_layer(x, p_all, i, j, lb_logits, norm_mix, w_in, gnorm, w_out, norm_ple,
                       w_gate, w_proj, tri, lvl, sample_qfkv, s0, s_prev, fin_prev, nxt):
    bsz, seq, _ = x.shape
    n_a = lb_logits.shape[0]
    n_l = seq // TL
    grid = (bsz, n_l)
    n_s = s0.shape[1]
    assert n_s % (bsz * n_l) == 0
    spt = n_s // (bsz * n_l)
    const2 = lambda b, l: (0, 0)
    nxt_ops, nxt_in, nxt_out, nxt_shapes = _next_weight_specs(
        nxt, bsz * n_l, lambda b, l: b * n_l + l)
    s_k = pl.BlockSpec((spt, HEADS, DK), lambda b, l: (b * n_l + l, 0, 0))
    s_v = pl.BlockSpec((spt, HEADS, DV), lambda b, l: (b * n_l + l, 0, 0))
    s_st = pl.BlockSpec((None, spt, HEADS, DK, DV), lambda b, l: (j, b * n_l + l, 0, 0, 0))
    sq, sf, sk, sv = sample_qfkv
    operands, in_specs, sizes = _expand([
        (x, pl.BlockSpec((None, TL, D_MODEL), lambda b, l: (b, l, 0))),
        (p_all, pl.BlockSpec((None, None, TL, PLE_DIM), lambda b, l: (i, b, l, 0))),
        (lb_logits, _resident((n_a, D_MODEL), const2)),
        (norm_mix, _resident((None, 1, D_MODEL), lambda b, l: (i, 0, 0))),
        _weight_cols(w_in),
        (gnorm, _resident((None, 1, DV), lambda b, l: (j, 0, 0))),
        _weight_cols(w_out),
        (norm_ple, _resident((None, 1, D_MODEL), lambda b, l: (i, 0, 0))),
        _weight_cols(w_gate),
        _weight_cols(w_proj),
        (tri, _resident(tri.shape, const2)),
        (lvl, _resident(lvl.shape, const2)),
        (sq, s_k), (sf, s_k), (sk, s_k), (sv, s_v), (s0, s_st),
        (nxt_ops, nxt_in),
    ])
    aliases = {}
    for prev, out_idx in ((s_prev, 2), (fin_prev, 1)):
        if prev is not None:
            operands.append(prev)
            in_specs.append(pl.BlockSpec(memory_space=pl.ANY))
            aliases[len(operands) - 1] = out_idx
    out_specs = [
        pl.BlockSpec((None, TL, D_MODEL), lambda b, l: (b, l, 0)),
        pl.BlockSpec((None, None, HEADS, DK, DV), lambda b, l: (j, b, 0, 0, 0)),
        s_st,
        s_v,
    ] + nxt_out
    out_shape = [
        jax.ShapeDtypeStruct((bsz, seq, D_MODEL), F32),
        jax.ShapeDtypeStruct((n_a, bsz, HEADS, DK, DV), F32),
        jax.ShapeDtypeStruct(s0.shape, F32),
        jax.ShapeDtypeStruct((n_s, HEADS, DV), F32),
    ] + nxt_shapes
    return pl.pallas_call(
        _regroup(functools.partial(_hgrn_prompt_kernel, layer_j=j), sizes),
        grid=grid, in_specs=in_specs, out_specs=out_specs, out_shape=out_shape,
        scratch_shapes=[pltpu.VMEM((HEADS, DV, DK), F32)],
        input_output_aliases=aliases,
        compiler_params=_params("arbitrary", "arbitrary"),
        name=f"hgrn_prompt_{j}",
    )(*operands)


def _gmlp_prompt_layer(x, p_all, i, j, norm_mix, w_in, ln_g, ln_b, w_sp, bias_full, w_out,
                       norm_ple, w_gate, w_proj, norm_final, final_norm, n_b, vt_prev, nxt):
    bsz, seq, _ = x.shape
    n_l = seq // TL
    grid = (bsz, n_l)
    nxt_ops, nxt_in, nxt_out, nxt_shapes = _next_weight_specs(
        nxt, bsz * n_l, lambda b, l: b * n_l + l)
    operands, in_specs, sizes = _expand([
        (x, pl.BlockSpec((None, TL, D_MODEL), lambda b, l: (b, l, 0))),
        (p_all, pl.BlockSpec((None, None, TL, PLE_DIM), lambda b, l: (i, b, l, 0))),
        (norm_mix, _resident((None, 1, D_MODEL), lambda b, l: (i, 0, 0))),
        _weight_cols(w_in),
        (ln_g, _resident((None, 1, D_INNER), lambda b, l: (j, 0, 0))),
        (ln_b, _resident((None, 1, D_INNER), lambda b, l: (j, 0, 0))),
        (w_sp, _resident((None, GROUPS, B_CHUNK, B_CHUNK), lambda b, l: (j, 0, 0, 0))),
        (bias_full, _resident((None, B_CHUNK, D_INNER), lambda b, l: (j, 0, 0))),
        _weight_cols(w_out),
        (norm_ple, _resident((None, 1, D_MODEL), lambda b, l: (i, 0, 0))),
        _weight_cols(w_gate),
        _weight_cols(w_proj),
        (norm_final, _resident((1, D_MODEL), lambda b, l: (0, 0))),
        (nxt_ops, nxt_in),
    ])
    aliases = {}
    if vt_prev is not None:
        operands.append(vt_prev)
        in_specs.append(pl.BlockSpec(memory_space=pl.ANY))
        aliases[len(operands) - 1] = 1
    out_specs = [
        pl.BlockSpec((None, TL, D_MODEL), lambda b, l: (b, l, 0)),
        pl.BlockSpec((None, None, B_CHUNK, D_INNER), lambda b, l: (j, b, 0, 0)),
    ] + nxt_out
    out_shape = [
        jax.ShapeDtypeStruct((bsz, seq, D_MODEL), F32),
        jax.ShapeDtypeStruct((n_b, bsz, B_CHUNK, D_INNER), F32),
    ] + nxt_shapes
    return pl.pallas_call(
        _regroup(functools.partial(_gmlp_prompt_kernel, final_norm=final_norm), sizes),
        grid=grid, in_specs=in_specs, out_specs=out_specs, out_shape=out_shape,
        input_output_aliases=aliases,
        compiler_params=_params("arbitrary", "arbitrary"),
        name=f"gmlp_prompt_{j}",
    )(*operands)


def _whole(shape):
    nd = len(shape)
    return pl.BlockSpec(shape, lambda *_: (0,) * nd)


def _hgrn_sample_front(x, i, j, lb_logits, norm_mix, w_in):
    n = x.shape[0]
    n_a = lb_logits.shape[0]
    operands, in_specs, sizes = _expand([
        (x, _whole((n, D_MODEL))),
        (lb_logits, _whole((n_a, D_MODEL))),
        (norm_mix, pl.BlockSpec((None, 1, D_MODEL), lambda g: (i, 0, 0))),
        _weight_cols(w_in),
    ])
    q, f, kk, v, z = pl.pallas_call(
        _regroup(functools.partial(_hgrn_sample_front_kernel, layer_j=j), sizes),
        grid=(1,),
        in_specs=in_specs,
        out_specs=[_whole((n, D_MODEL))] * 3 + [_whole((n, D_INNER))] * 2,
        out_shape=[jax.ShapeDtypeStruct((n, D_MODEL), F32)] * 3
        + [jax.ShapeDtypeStruct((n, D_INNER), F32)] * 2,
        compiler_params=_params("arbitrary"),
        name=f"hgrn_sample_front_{j}",
    )(*operands)

    qfkv = (q.reshape(n, HEADS, DK), f.reshape(n, HEADS, DK), kk.reshape(n, HEADS, DK),
            v.reshape(n, HEADS, DV))
    return qfkv, z


def _hgrn_sample_back(x, o, z, p_all, i, j, gnorm, w_out, norm_ple, w_gate, w_proj):
    n = x.shape[0]
    operands, in_specs, sizes = _expand([
        (x, _whole((n, D_MODEL))),
        (o.reshape(n, D_INNER), _whole((n, D_INNER))),
        (z, _whole((n, D_INNER))),
        (p_all, pl.BlockSpec((None, n, PLE_DIM), lambda g: (i, 0, 0))),
        (gnorm, pl.BlockSpec((None, 1, DV), lambda g: (j, 0, 0))),
        _weight_cols(w_out),
        (norm_ple, pl.BlockSpec((None, 1, D_MODEL), lambda g: (i, 0, 0))),
        _weight_cols(w_gate),
        _weight_cols(w_proj),
    ])
    return pl.pallas_call(
        _regroup(_hgrn_sample_back_kernel, sizes),
        grid=(1,),
        in_specs=in_specs,
        out_specs=_whole((n, D_MODEL)),
        out_shape=jax.ShapeDtypeStruct((n, D_MODEL), F32),
        compiler_params=_params("arbitrary"),
        name=f"hgrn_sample_back_{j}",
    )(*operands)


def _gmlp_sample_layer(x, p_all, i, j, norm_mix, w_in, ln_g, ln_b, w00, b0, w_out, norm_ple,
                       w_gate, w_proj, norm_final, final_norm):
    n = x.shape[0]
    vec = lambda layer, width: pl.BlockSpec((None, 1, width), lambda g: (layer, 0, 0))
    operands, in_specs, sizes = _expand([
        (x, _whole((n, D_MODEL))),
        (p_all, pl.BlockSpec((None, n, PLE_DIM), lambda g: (i, 0, 0))),
        (norm_mix, vec(i, D_MODEL)),
        _weight_cols(w_in),
        (ln_g, vec(j, D_INNER)),
        (ln_b, vec(j, D_INNER)),
        (w00, vec(j, D_INNER)),
        (b0, vec(j, D_INNER)),
        _weight_cols(w_out),
        (norm_ple, vec(i, D_MODEL)),
        _weight_cols(w_gate),
        _weight_cols(w_proj),
        (norm_final, _whole((1, D_MODEL))),
    ])
    return pl.pallas_call(
        _regroup(functools.partial(_gmlp_sample_kernel, final_norm=final_norm), sizes),
        grid=(1,),
        in_specs=in_specs,
        out_specs=[_whole((n, D_MODEL)), _whole((n, D_INNER))],
        out_shape=[jax.ShapeDtypeStruct((n, D_MODEL), F32),
                   jax.ShapeDtypeStruct((n, D_INNER), F32)],
        compiler_params=_params("arbitrary"),
        name=f"gmlp_sample_{j}",
    )(*operands)


def kernel(x_prompt, x_sample, state_hgrn, p_prompt, p_sample, norm_mix, w_in_a, lb_logits,
           gnorm_a, w_out_a, w_in_b, ln_v_g, ln_v_b, w_spatial, b_spatial, w_out_b,
           norm_ple, w_ple_gate, w_ple_proj, norm_final):
    depth = norm_mix.shape[0]
    n_s = x_sample.shape[0]
    assert x_prompt.shape[1] % TL == 0 and TL % B_CHUNK == 0 and x_sample.shape[1] == 1

    def layer_weights(i):
        mixer = (w_in_a, w_out_a) if i % 2 == 0 else (w_in_b, w_out_b)
        return ((mixer[0], i // 2), (mixer[1], i // 2), (w_ple_gate, i), (w_ple_proj, i))

    wb = tuple(stack[idx].astype(BF16) for stack, idx in layer_weights(0))

    tri = jnp.asarray(_TRI_NP, dtype=BF16)
    lvl = jnp.asarray(_LVL_NP)
    nf = norm_final.reshape(1, D_MODEL)
    bias_full = jnp.repeat(jnp.swapaxes(b_spatial, 1, 2), DG, axis=2)
    w00 = jnp.repeat(w_spatial[:, :, 0, 0], DG, axis=1)[:, None, :]
    b0 = jnp.repeat(b_spatial[:, :, 0], DG, axis=1)[:, None, :]
    nmix = norm_mix[:, None, :]
    nple = norm_ple[:, None, :]
    gn = gnorm_a[:, None, :]
    lng = ln_v_g[:, None, :]
    lnb = ln_v_b[:, None, :]

    if depth % 2 == 1:
        raise NotImplementedError("final norm is fused into the last (chunk-MLP) layer")
    hp = x_prompt
    hs = x_sample.reshape(n_s, D_MODEL)
    ps = p_sample.reshape(depth, n_s, PLE_DIM)
    vt_s = []
    st_p = st_s = vt_p = None
    for i in range(depth):
        j = i // 2
        last = i == depth - 1
        nxt = None if last else layer_weights(i + 1)
        w_in, w_out, w_gate, w_proj = wb
        if i % 2 == 0:
            qfkv, z_s = _hgrn_sample_front(hs, i, j, lb_logits, nmix, w_in)
            hp, st_p, st_s, o_s, *wb = _hgrn_prompt_layer(
                hp, p_prompt, i, j, lb_logits, nmix, w_in, gn, w_out, nple, w_gate, w_proj,
                tri, lvl, qfkv, state_hgrn, st_s, st_p, nxt)
            hs = _hgrn_sample_back(hs, o_s, z_s, ps, i, j, gn, w_out, nple, w_gate, w_proj)
        else:
            hp, vt_p, *wb = _gmlp_prompt_layer(
                hp, p_prompt, i, j, nmix, w_in, lng, lnb, w_spatial, bias_full, w_out, nple,
                w_gate, w_proj, nf, last, w_in_b.shape[0], vt_p, nxt)
            hs, vts = _gmlp_sample_layer(hs, ps, i, j, nmix, w_in, lng, lnb, w00, b0, w_out,
                                         nple, w_gate, w_proj, nf, last)
            vt_s.append(vts.reshape(n_s, 1, D_INNER))
    return (hp, hs.reshape(n_s, 1, D_MODEL), st_p, st_s, vt_p, jnp.stack(vt_s))
```

```python
import functools
import math

import numpy as np
import jax
import jax.numpy as jnp
from jax import lax
from jax.experimental import pallas as pl
from jax.experimental.pallas import tpu as pltpu

D_MODEL = 1024
D_INNER = 2048
HEADS = 8
DK = 128
DV = 256
GROUPS = 8
DG = 256
B_CHUNK = 128
PLE_DIM = 256
EPS = 1e-6
LOG2_E = 1.4426950408889634

TL = 256
HALF = TL // 2
N_LEVELS = int(math.log2(TL))
WEIGHT_CHUNK_3 = 1536
WEIGHT_CHUNK_1 = 512
VMEM_LIMIT = 58 * 1024 * 1024

F32 = jnp.float32
BF16 = jnp.bfloat16


def _build_level_consts(tl):
    t = np.arange(tl)[:, None]
    s = np.arange(tl)[None, :]
    tri = (s <= t).astype(np.float32)
    x = np.maximum(t ^ s, 1)
    p = np.floor(np.log2(x)).astype(np.int32)
    nlev = int(math.log2(tl))
    lvl = np.where(t == s, 0, np.where(t > s, nlev - p, -1)).astype(np.int32)
    return tri, lvl[:tl // 2, :tl // 2]


_TRI_NP, _LVL_NP = _build_level_consts(TL)


def _dot(a, b):
    return jnp.dot(a, b, preferred_element_type=F32)


def _dot_cols(a, w_refs):
    return jnp.concatenate([_dot(a, w[...]) for w in w_refs], axis=1)


def _dot_nt(a, b):
    return lax.dot_general(a, b, (((1,), (1,)), ((), ())), preferred_element_type=F32)


def _dot_tn(a, b):
    return lax.dot_general(a, b, (((0,), (0,)), ((), ())), preferred_element_type=F32)


def _rmsnorm(x, g):
    ms = jnp.mean(x * x, axis=-1, keepdims=True)
    return x * lax.rsqrt(ms + EPS) * g


def _silu(x):
    return x * jax.nn.sigmoid(x)


def _gelu_tanh(x):
    c = math.sqrt(2.0 / math.pi)
    return 0.5 * x * (1.0 + jnp.tanh(c * (x + 0.044715 * (x * x * x))))


def _forget_lower_bound(lbl, j):
    mx = jnp.max(lbl, axis=0, keepdims=True)
    e = jnp.exp(lbl - mx)
    sm = e / jnp.sum(e, axis=0, keepdims=True)
    cum0 = sm[0:1, :]
    cum = cum0
    for i in range(1, j + 1):
        cum = cum + sm[i:i + 1, :]
    return cum - cum0


def _forget_gates(fpre, lb):
    e = jnp.exp(-jnp.abs(fpre))
    log_sig = jnp.minimum(fpre, 0.0) - jnp.log1p(e)
    a = jnp.log(lb)
    y = jnp.log1p(-lb) + log_sig
    logf = jnp.maximum(a, y) + jnp.log1p(jnp.exp(-jnp.abs(a - y)))
    one_minus_f = (1.0 - lb) * (jnp.where(fpre >= 0.0, e, 1.0) / (1.0 + e))
    return logf, one_minus_f


def _hgrn_front(x, nm, w_in, lb):
    hn = _rmsnorm(x, nm).astype(BF16)
    proj = _dot_cols(hn, w_in)
    q = _silu(proj[:, :D_MODEL])
    logf, kk = _forget_gates(proj[:, D_MODEL:2 * D_MODEL], lb)
    v = proj[:, 2 * D_MODEL:2 * D_MODEL + D_INNER]
    z = proj[:, 2 * D_MODEL + D_INNER:]
    return q, logf, kk, v, z


def _head_rmsnorm(o, gn):
    outs = []
    for h in range(HEADS):
        outs.append(_rmsnorm(o[:, h * DV:(h + 1) * DV], gn))
    return jnp.concatenate(outs, axis=1)


def _ple(h, p, npl, w_gate, w_proj):
    gate = jax.nn.sigmoid(_dot_cols(_rmsnorm(h, npl).astype(BF16), w_gate))
    return h + gate * _dot_cols(p.astype(BF16), w_proj)


def _mixer_back(x, branch, w_out, p, npl, w_gate, w_proj):
    h = x + _dot_cols(branch.astype(BF16), w_out)
    return _ple(h, p, npl, w_gate, w_proj)


def _gmlp_front(x, nm, w_in, ln_g, ln_b):
    hn = _rmsnorm(x, nm).astype(BF16)
    proj = _dot_cols(hn, w_in)
    u = _gelu_tanh(proj[:, :D_INNER])
    vg = _gelu_tanh(proj[:, D_INNER:2 * D_INNER])
    z = proj[:, 2 * D_INNER:]
    mu = jnp.mean(vg, axis=-1, keepdims=True)
    vc = vg - mu
    vn = vc * lax.rsqrt(jnp.mean(vc * vc, axis=-1, keepdims=True) + EPS) * ln_g + ln_b
    return u, vn, z


def _hgrn_prompt_kernel(x_ref, p_ref, lbl_ref, nm_ref, win_ref, gn_ref, wout_ref, npl_ref,
                        wg_ref, wp_ref, tri_ref, lvl_ref, sq_ref, sf_ref, sk_ref, sv_ref, ss_ref,
                        nxt_refs, *rest, layer_j):
    n_out = 4 + len(nxt_refs)
    y_ref, sfin_ref, ssnew_ref, so_ref = rest[len(rest) - n_out - 1:len(rest) - n_out + 3]
    st_ref = rest[-1]
    l = pl.program_id(1)
    if nxt_refs:
        _cast_next_weights(pl.program_id(0) * pl.num_programs(1) + l, nxt_refs,
                           rest[len(rest) - n_out + 3:len(rest) - 1])

    @pl.when(l == 0)
    def _():
        st_ref[...] = jnp.zeros_like(st_ref)

    _sample_state_update(sq_ref, sf_ref, sk_ref, sv_ref, ss_ref, ssnew_ref, so_ref)

    x = x_ref[...]
    lb = _forget_lower_bound(lbl_ref[...], layer_j)
    q, logf, kk, v, z = _hgrn_front(x, nm_ref[...], win_ref, lb)

    lg = logf * LOG2_E
    g_hi = lg.astype(BF16)
    g_lo = (lg - g_hi.astype(F32)).astype(BF16)
    g2 = jnp.concatenate([g_hi, g_lo], axis=1)
    b2 = _dot(tri_ref[...], g2)
    b = b2[:, :D_MODEL] + b2[:, D_MODEL:]
    b_last = b[TL - 1:TL, :]
    qs = (q * jnp.exp2(b)).astype(BF16)
    kd = (kk * jnp.exp2(b_last - b)).astype(BF16)
    dec = jnp.exp2(b_last)
    qb = q.astype(BF16)
    kb = kk.astype(BF16)
    vb = v.astype(BF16)

    zk, zq = {}, {}
    for level in range(1, N_LEVELS + 1):
        hs = TL >> level
        if hs >= 8:
            grp = 2 * hs
            shp = (TL // grp, grp, D_MODEL)
            b3, q3, k3 = b.reshape(shp), q.reshape(shp), kk.reshape(shp)
            mid = b3[:, hs - 1:hs, :]
            zq3 = q3[:, hs:, :] * jnp.exp2(b3[:, hs:, :] - mid)
            zk3 = k3[:, :hs, :] * jnp.exp2(mid - b3[:, :hs, :])
            zq[level] = zq3.reshape(TL // 2, D_MODEL).astype(BF16)
            zk[level] = jnp.concatenate([zk3, zq3], axis=1).reshape(TL, D_MODEL).astype(BF16)
        else:
            shp = (TL // 8, 8, D_MODEL)
            b8, q8, k8 = b.reshape(shp), q.reshape(shp), kk.reshape(shp)
            sub = lax.broadcasted_iota(jnp.int32, shp, 1)
            is_q = (sub & hs) != 0
            if hs == 4:
                mid = jnp.broadcast_to(b8[:, 3:4, :], shp)
            elif hs == 2:
                mid = jnp.where(sub < 4, jnp.broadcast_to(b8[:, 1:2, :], shp),
                                jnp.broadcast_to(b8[:, 5:6, :], shp))
            else:
                mid = None
            if mid is None:
                e = jnp.where(is_q, lg.reshape(shp), 0.0)
            else:
                e = jnp.where(is_q, b8 - mid, mid - b8)
            zfine = jnp.where(is_q, q8, k8) * jnp.exp2(e)
            zk[level] = zfine.reshape(TL, D_MODEL).astype(BF16)

    lvl = lvl_ref[...]
    atts = []
    for h in range(HEADS):
        sl = slice(h * DK, (h + 1) * DK)
        pd = _dot_nt(qb[:, sl], kb[:, sl])
        diag = [jnp.where(lvl == 0, pd[:HALF, :HALF], 0.0),
                jnp.where(lvl == 0, pd[HALF:, HALF:], 0.0)]
        for level in range(N_LEVELS, 1, -1):
            hs = TL >> level
            zkl = zk[level][:, sl]
            if hs >= 8:
                grp = 2 * hs
                nq = HALF // 2
                pr = _dot_nt(zq[level][:, sl], zkl)
                hit = lvl.reshape(HALF // grp, grp, HALF)[:, hs:, :] == level
                for i in range(2):
                    pq = pr[i * nq:(i + 1) * nq, i * HALF:(i + 1) * HALF]
                    d3 = diag[i].reshape(HALF // grp, grp, HALF)
                    upper = jnp.where(hit, pq.reshape(HALF // grp, hs, HALF), d3[:, hs:, :])
                    diag[i] = jnp.concatenate([d3[:, :hs, :], upper], axis=1).reshape(HALF, HALF)
            else:
                pf = _dot_nt(zkl, zkl)
                hit = lvl == level
                diag[0] = jnp.where(hit, pf[:HALF, :HALF], diag[0])
                diag[1] = jnp.where(hit, pf[HALF:, HALF:], diag[1])
        a10 = _dot_nt(zq[1][:, sl], zk[1][:HALF, sl])
        att = jnp.concatenate(
            [jnp.concatenate([diag[0], jnp.zeros((HALF, HALF), F32)], axis=1),
             jnp.concatenate([a10, diag[1]], axis=1)], axis=0)
        atts.append(att.astype(BF16))
    outs = []
    for h in range(HEADS):
        sl = slice(h * DK, (h + 1) * DK)
        vsl = slice(h * DV, (h + 1) * DV)
        st = st_ref[h]
        o = _dot_nt(qs[:, sl], st.astype(BF16)) + _dot(atts[h], vb[:, vsl])
        st_ref[h] = st * dec[:, sl] + _dot_tn(vb[:, vsl], kd[:, sl])
        outs.append(o)
    o = jnp.concatenate(outs, axis=1)

    branch = _head_rmsnorm(o, gn_ref[...]) * _silu(z)
    y_ref[...] = _mixer_back(x, branch, wout_ref, p_ref[...], npl_ref[...],
                             wg_ref, wp_ref)

    @pl.when(l == pl.num_programs(1) - 1)
    def _():
        for h in range(HEADS):
            sfin_ref[h] = st_ref[h].T


def _gmlp_prompt_kernel(x_ref, p_ref, nm_ref, win_ref, lng_ref, lnb_ref, wsp_ref, bsp_ref,
                        wout_ref, npl_ref, wg_ref, wp_ref, nf_ref, samp_refs, nxt_refs, *rest,
                        final_norm):
    n_out = 4 + len(nxt_refs)
    y_ref, vt_ref, ys_ref, vts_ref = rest[len(rest) - n_out:len(rest) - n_out + 4]
    l = pl.program_id(1)
    step = pl.program_id(0) * pl.num_programs(1) + l
    if nxt_refs:
        _cast_next_weights(step, nxt_refs, rest[len(rest) - n_out + 4:])

    @pl.when(step == 0)
    def _():
        xs_ref, ps_ref, w00_ref, b0_ref = samp_refs
        _gmlp_sample_body(xs_ref, ps_ref, nm_ref, win_ref, lng_ref, lnb_ref, w00_ref, b0_ref,
                          wout_ref, npl_ref, wg_ref, wp_ref, nf_ref, ys_ref, vts_ref,
                          final_norm=final_norm)

    x = x_ref[...]
    u, vn, z = _gmlp_front(x, nm_ref[...], win_ref, lng_ref[...], lnb_ref[...])
    vb = vn.astype(BF16)
    r = lax.broadcasted_iota(jnp.int32, (B_CHUNK, B_CHUNK), 0)
    c = lax.broadcasted_iota(jnp.int32, (B_CHUNK, B_CHUNK), 1)
    tril = c <= r
    bias = bsp_ref[...]
    cols = []
    for g in range(GROUPS):
        w = jnp.where(tril, wsp_ref[g], 0.0).astype(BF16)
        gsl = slice(g * DG, (g + 1) * DG)
        chunks = []
        for ci in range(TL // B_CHUNK):
            rsl = slice(ci * B_CHUNK, (ci + 1) * B_CHUNK)
            chunks.append(_dot(w, vb[rsl, gsl]) + bias[:, gsl])
        cols.append(jnp.concatenate(chunks, axis=0))
    s = jnp.concatenate(cols, axis=1)
    branch = u * s * _silu(z)
    h = _mixer_back(x, branch, wout_ref, p_ref[...], npl_ref[...], wg_ref, wp_ref)
    if final_norm:
        h = _rmsnorm(h, nf_ref[...])
    y_ref[...] = h

    @pl.when(l == pl.num_programs(1) - 1)
    def _():
        vt_ref[...] = vn[TL - B_CHUNK:, :]


def _hgrn_sample_front_kernel(x_ref, lbl_ref, nm_ref, win_ref, q_ref, f_ref, k_ref, v_ref, z_ref,
                              *, layer_j):
    lb = _forget_lower_bound(lbl_ref[...], layer_j)
    q, logf, kk, v, z = _hgrn_front(x_ref[...], nm_ref[...], win_ref, lb)
    q_ref[...] = q
    f_ref[...] = jnp.exp(logf)
    k_ref[...] = kk
    v_ref[...] = v
    z_ref[...] = z


def _sample_state_update(q_ref, f_ref, k_ref, v_ref, s_ref, snew_ref, o_ref):
    for i in range(q_ref.shape[0]):
        q_t = q_ref[i].T
        f_t = f_ref[i].T
        k_t = k_ref[i].T
        for h in range(HEADS):
            s_new = f_t[:, h:h + 1] * s_ref[i, h] + k_t[:, h:h + 1] * v_ref[i, h:h + 1, :]
            snew_ref[i, h] = s_new
            o_ref[i, h:h + 1, :] = jnp.sum(q_t[:, h:h + 1] * s_new, axis=0, keepdims=True)


def _cast_next_weights(step, nxt_refs, out_refs):
    for src, dst in zip(nxt_refs[:3], out_refs[:3]):
        dst[...] = src[...].astype(BF16)

    @pl.when(step == 0)
    def _():
        out_refs[3][...] = nxt_refs[3][...].astype(BF16)


def _hgrn_sample_back_kernel(x_ref, o_ref, z_ref, p_ref, gn_ref, wout_ref, npl_ref, wg_ref,
                             wp_ref, y_ref):
    branch = _head_rmsnorm(o_ref[...], gn_ref[...]) * _silu(z_ref[...])
    y_ref[...] = _mixer_back(x_ref[...], branch, wout_ref, p_ref[...], npl_ref[...],
                             wg_ref, wp_ref)


def _gmlp_sample_body(x_ref, p_ref, nm_ref, win_ref, lng_ref, lnb_ref, w00_ref, b0_ref,
                      wout_ref, npl_ref, wg_ref, wp_ref, nf_ref, y_ref, vt_ref, *, final_norm):
    x = x_ref[...]
    u, vn, z = _gmlp_front(x, nm_ref[...], win_ref, lng_ref[...], lnb_ref[...])
    s = w00_ref[...] * vn + b0_ref[...]
    branch = u * s * _silu(z)
    h = _mixer_back(x, branch, wout_ref, p_ref[...], npl_ref[...], wg_ref, wp_ref)
    if final_norm:
        h = _rmsnorm(h, nf_ref[...])
    y_ref[...] = h
    vt_ref[...] = vn


def _resident(shape, index):
    return pl.BlockSpec(shape, index, pipeline_mode=pl.Buffered(1))


def _chunk_width(n):
    width = WEIGHT_CHUNK_3 if n % WEIGHT_CHUNK_3 == 0 else WEIGHT_CHUNK_1
    assert n % width == 0 and width % 1024 != 0
    return width


def _weight_cols(w):
    k, n = w.shape
    width = _chunk_width(n)
    specs = [_resident((k, width), lambda *_, c=c: (0, c)) for c in range(n // width)]
    return [w] * len(specs), specs


def _next_weight_specs(nxt, n_steps, step_of):
    ops, ins, outs, shapes = [], [], [], []
    for idx, (stack, layer) in enumerate(nxt or ()):
        _, k, n = stack.shape
        ops.append(stack)
        shapes.append(jax.ShapeDtypeStruct((k, n), BF16))
        if idx < 3:
            rows = k // n_steps
            assert rows * n_steps == k and rows % 16 == 0
            ins.append(pl.BlockSpec((None, rows, n),
                                    lambda *g, layer=layer: (layer, step_of(*g), 0)))
            outs.append(pl.BlockSpec((rows, n), lambda *g: (step_of(*g), 0)))
        else:
            ins.append(_resident((None, k, n), lambda *g, layer=layer: (layer, 0, 0)))
            outs.append(pl.BlockSpec((k, n), lambda *g: (0, 0)))
    return ops, ins, outs, shapes


def _expand(args):
    ops, specs, sizes = [], [], []
    for o, s in args:
        if isinstance(o, list):
            ops += o
            specs += s
            sizes.append(len(o))
        else:
            ops.append(o)
            specs.append(s)
            sizes.append(None)
    return ops, specs, sizes


def _regroup(fn, sizes):
    def wrapped(*refs):
        pos, grouped = 0, []
        for n in sizes:
            if n is None:
                grouped.append(refs[pos])
                pos += 1
            else:
                grouped.append(tuple(refs[pos:pos + n]))
                pos += n
        return fn(*grouped, *refs[pos:])
    return wrapped


def _params(*sem):
    return pltpu.CompilerParams(dimension_semantics=sem, vmem_limit_bytes=VMEM_LIMIT)


def _hgrn_prompt_layer(x, p_all, i, j, lb_logits, norm_mix, w_in, gnorm, w_out, norm_ple,
                       w_gate, w_proj, tri, lvl, sample_qfkv, s0, s_prev, fin_prev, nxt):
    bsz, seq, _ = x.shape
    n_a = lb_logits.shape[0]
    n_l = seq // TL
    grid = (bsz, n_l)
    n_s = s0.shape[1]
    assert n_s % (bsz * n_l) == 0
    spt = n_s // (bsz * n_l)
    const2 = lambda b, l: (0, 0)
    nxt_ops, nxt_in, nxt_out, nxt_shapes = _next_weight_specs(
        nxt, bsz * n_l, lambda b, l: b * n_l + l)
    s_k = pl.BlockSpec((spt, HEADS, DK), lambda b, l: (b * n_l + l, 0, 0))
    s_v = pl.BlockSpec((spt, HEADS, DV), lambda b, l: (b * n_l + l, 0, 0))
    s_st = pl.BlockSpec((None, spt, HEADS, DK, DV), lambda b, l: (j, b * n_l + l, 0, 0, 0))
    sq, sf, sk, sv = sample_qfkv
    operands, in_specs, sizes = _expand([
        (x, pl.BlockSpec((None, TL, D_MODEL), lambda b, l: (b, l, 0))),
        (p_all, pl.BlockSpec((None, None, TL, PLE_DIM), lambda b, l: (i, b, l, 0))),
        (lb_logits, _resident((n_a, D_MODEL), const2)),
        (norm_mix, _resident((None, 1, D_MODEL), lambda b, l: (i, 0, 0))),
        _weight_cols(w_in),
        (gnorm, _resident((None, 1, DV), lambda b, l: (j, 0, 0))),
        _weight_cols(w_out),
        (norm_ple, _resident((None, 1, D_MODEL), lambda b, l: (i, 0, 0))),
        _weight_cols(w_gate),
        _weight_cols(w_proj),
        (tri, _resident(tri.shape, const2)),
        (lvl, _resident(lvl.shape, const2)),
        (sq, s_k), (sf, s_k), (sk, s_k), (sv, s_v), (s0, s_st),
        (nxt_ops, nxt_in),
    ])
    aliases = {}
    for prev, out_idx in ((s_prev, 2), (fin_prev, 1)):
        if prev is not None:
            operands.append(prev)
            in_specs.append(pl.BlockSpec(memory_space=pl.ANY))
            aliases[len(operands) - 1] = out_idx
    out_specs = [
        pl.BlockSpec((None, TL, D_MODEL), lambda b, l: (b, l, 0)),
        pl.BlockSpec((None, None, HEADS, DK, DV), lambda b, l: (j, b, 0, 0, 0)),
        s_st,
        s_v,
    ] + nxt_out
    out_shape = [
        jax.ShapeDtypeStruct((bsz, seq, D_MODEL), F32),
        jax.ShapeDtypeStruct((n_a, bsz, HEADS, DK, DV), F32),
        jax.ShapeDtypeStruct(s0.shape, F32),
        jax.ShapeDtypeStruct((n_s, HEADS, DV), F32),
    ] + nxt_shapes
    return pl.pallas_call(
        _regroup(functools.partial(_hgrn_prompt_kernel, layer_j=j), sizes),
        grid=grid, in_specs=in_specs, out_specs=out_specs, out_shape=out_shape,
        scratch_shapes=[pltpu.VMEM((HEADS, DV, DK), F32)],
        input_output_aliases=aliases,
        compiler_params=_params("arbitrary", "arbitrary"),
        name=f"hgrn_prompt_{j}",
    )(*operands)


def _gmlp_prompt_layer(x, p_all, i, j, norm_mix, w_in, ln_g, ln_b, w_sp, bias_full, w_out,
                       norm_ple, w_gate, w_proj, norm_final, final_norm, n_b, vt_prev, nxt, sample):
    xs, ps_all, w00, b0 = sample
    n_s = xs.shape[0]
    bsz, seq, _ = x.shape
    n_l = seq // TL
    grid = (bsz, n_l)
    nxt_ops, nxt_in, nxt_out, nxt_shapes = _next_weight_specs(
        nxt, bsz * n_l, lambda b, l: b * n_l + l)
    operands, in_specs, sizes = _expand([
        (x, pl.BlockSpec((None, TL, D_MODEL), lambda b, l: (b, l, 0))),
        (p_all, pl.BlockSpec((None, None, TL, PLE_DIM), lambda b, l: (i, b, l, 0))),
        (norm_mix, _resident((None, 1, D_MODEL), lambda b, l: (i, 0, 0))),
        _weight_cols(w_in),
        (ln_g, _resident((None, 1, D_INNER), lambda b, l: (j, 0, 0))),
        (ln_b, _resident((None, 1, D_INNER), lambda b, l: (j, 0, 0))),
        (w_sp, _resident((None, GROUPS, B_CHUNK, B_CHUNK), lambda b, l: (j, 0, 0, 0))),
        (bias_full, _resident((None, B_CHUNK, D_INNER), lambda b, l: (j, 0, 0))),
        _weight_cols(w_out),
        (norm_ple, _resident((None, 1, D_MODEL), lambda b, l: (i, 0, 0))),
        _weight_cols(w_gate),
        _weight_cols(w_proj),
        (norm_final, _resident((1, D_MODEL), lambda b, l: (0, 0))),
        ([xs, ps_all, w00, b0],
         [_resident((n_s, D_MODEL), lambda b, l: (0, 0)),
          _resident((None, n_s, PLE_DIM), lambda b, l: (i, 0, 0)),
          _resident((None, 1, D_INNER), lambda b, l: (j, 0, 0)),
          _resident((None, 1, D_INNER), lambda b, l: (j, 0, 0))]),
        (nxt_ops, nxt_in),
    ])
    aliases = {}
    if vt_prev is not None:
        operands.append(vt_prev)
        in_specs.append(pl.BlockSpec(memory_space=pl.ANY))
        aliases[len(operands) - 1] = 1
    out_specs = [
        pl.BlockSpec((None, TL, D_MODEL), lambda b, l: (b, l, 0)),
        pl.BlockSpec((None, None, B_CHUNK, D_INNER), lambda b, l: (j, b, 0, 0)),
        pl.BlockSpec((n_s, D_MODEL), lambda b, l: (0, 0)),
        pl.BlockSpec((n_s, D_INNER), lambda b, l: (0, 0)),
    ] + nxt_out
    out_shape = [
        jax.ShapeDtypeStruct((bsz, seq, D_MODEL), F32),
        jax.ShapeDtypeStruct((n_b, bsz, B_CHUNK, D_INNER), F32),
        jax.ShapeDtypeStruct((n_s, D_MODEL), F32),
        jax.ShapeDtypeStruct((n_s, D_INNER), F32),
    ] + nxt_shapes
    return pl.pallas_call(
        _regroup(functools.partial(_gmlp_prompt_kernel, final_norm=final_norm), sizes),
        grid=grid, in_specs=in_specs, out_specs=out_specs, out_shape=out_shape,
        input_output_aliases=aliases,
        compiler_params=_params("arbitrary", "arbitrary"),
        name=f"gmlp_prompt_{j}",
    )(*operands)


def _whole(shape):
    nd = len(shape)
    return pl.BlockSpec(shape, lambda *_: (0,) * nd)


def _hgrn_sample_front(x, i, j, lb_logits, norm_mix, w_in):
    n = x.shape[0]
    n_a = lb_logits.shape[0]
    operands, in_specs, sizes = _expand([
        (x, _whole((n, D_MODEL))),
        (lb_logits, _whole((n_a, D_MODEL))),
        (norm_mix, pl.BlockSpec((None, 1, D_MODEL), lambda g: (i, 0, 0))),
        _weight_cols(w_in),
    ])
    q, f, kk, v, z = pl.pallas_call(
        _regroup(functools.partial(_hgrn_sample_front_kernel, layer_j=j), sizes),
        grid=(1,),
        in_specs=in_specs,
        out_specs=[_whole((n, D_MODEL))] * 3 + [_whole((n, D_INNER))] * 2,
        out_shape=[jax.ShapeDtypeStruct((n, D_MODEL), F32)] * 3
        + [jax.ShapeDtypeStruct((n, D_INNER), F32)] * 2,
        compiler_params=_params("arbitrary"),
        name=f"hgrn_sample_front_{j}",
    )(*operands)

    qfkv = (q.reshape(n, HEADS, DK), f.reshape(n, HEADS, DK), kk.reshape(n, HEADS, DK),
            v.reshape(n, HEADS, DV))
    return qfkv, z


def _hgrn_sample_back(x, o, z, p_all, i, j, gnorm, w_out, norm_ple, w_gate, w_proj):
    n = x.shape[0]
    operands, in_specs, sizes = _expand([
        (x, _whole((n, D_MODEL))),
        (o.reshape(n, D_INNER), _whole((n, D_INNER))),
        (z, _whole((n, D_INNER))),
        (p_all, pl.BlockSpec((None, n, PLE_DIM), lambda g: (i, 0, 0))),
        (gnorm, pl.BlockSpec((None, 1, DV), lambda g: (j, 0, 0))),
        _weight_cols(w_out),
        (norm_ple, pl.BlockSpec((None, 1, D_MODEL), lambda g: (i, 0, 0))),
        _weight_cols(w_gate),
        _weight_cols(w_proj),
    ])
    return pl.pallas_call(
        _regroup(_hgrn_sample_back_kernel, sizes),
        grid=(1,),
        in_specs=in_specs,
        out_specs=_whole((n, D_MODEL)),
        out_shape=jax.ShapeDtypeStruct((n, D_MODEL), F32),
        compiler_params=_params("arbitrary"),
        name=f"hgrn_sample_back_{j}",
    )(*operands)


def kernel(x_prompt, x_sample, state_hgrn, p_prompt, p_sample, norm_mix, w_in_a, lb_logits,
           gnorm_a, w_out_a, w_in_b, ln_v_g, ln_v_b, w_spatial, b_spatial, w_out_b,
           norm_ple, w_ple_gate, w_ple_proj, norm_final):
    depth = norm_mix.shape[0]
    n_s = x_sample.shape[0]
    assert x_prompt.shape[1] % TL == 0 and TL % B_CHUNK == 0 and x_sample.shape[1] == 1

    def layer_weights(i):
        mixer = (w_in_a, w_out_a) if i % 2 == 0 else (w_in_b, w_out_b)
        return ((mixer[0], i // 2), (mixer[1], i // 2), (w_ple_gate, i), (w_ple_proj, i))

    wb = tuple(stack[idx].astype(BF16) for stack, idx in layer_weights(0))

    tri = jnp.asarray(_TRI_NP, dtype=BF16)
    lvl = jnp.asarray(_LVL_NP)
    nf = norm_final.reshape(1, D_MODEL)
    bias_full = jnp.repeat(jnp.swapaxes(b_spatial, 1, 2), DG, axis=2)
    w00 = jnp.repeat(w_spatial[:, :, 0, 0], DG, axis=1)[:, None, :]
    b0 = jnp.repeat(b_spatial[:, :, 0], DG, axis=1)[:, None, :]
    nmix = norm_mix[:, None, :]
    nple = norm_ple[:, None, :]
    gn = gnorm_a[:, None, :]
    lng = ln_v_g[:, None, :]
    lnb = ln_v_b[:, None, :]

    if depth % 2 == 1:
        raise NotImplementedError("final norm is fused into the last (chunk-MLP) layer")
    hp = x_prompt
    hs = x_sample.reshape(n_s, D_MODEL)
    ps = p_sample.reshape(depth, n_s, PLE_DIM)
    vt_s = []
    st_p = st_s = vt_p = None
    for i in range(depth):
        j = i // 2
        last = i == depth - 1
        nxt = None if last else layer_weights(i + 1)
        w_in, w_out, w_gate, w_proj = wb
        if i % 2 == 0:
            qfkv, z_s = _hgrn_sample_front(hs, i, j, lb_logits, nmix, w_in)
            hp, st_p, st_s, o_s, *wb = _hgrn_prompt_layer(
                hp, p_prompt, i, j, lb_logits, nmix, w_in, gn, w_out, nple, w_gate, w_proj,
                tri, lvl, qfkv, state_hgrn, st_s, st_p, nxt)
            hs = _hgrn_sample_back(hs, o_s, z_s, ps, i, j, gn, w_out, nple, w_gate, w_proj)
        else:
            hp, vt_p, hs, vts, *wb = _gmlp_prompt_layer(
                hp, p_prompt, i, j, nmix, w_in, lng, lnb, w_spatial, bias_full, w_out, nple,
                w_gate, w_proj, nf, last, w_in_b.shape[0], vt_p, nxt, (hs, ps, w00, b0))
            vt_s.append(vts.reshape(n_s, 1, D_INNER))
    return (hp, hs.reshape(n_s, 1, D_MODEL), st_p, st_s, vt_p, jnp.stack(vt_s))
```

```python
import functools
import math

import numpy as np
import jax
import jax.numpy as jnp
from jax import lax
from jax.experimental import pallas as pl
from jax.experimental.pallas import tpu as pltpu

D_MODEL = 1024
D_INNER = 2048
HEADS = 8
DK = 128
DV = 256
GROUPS = 8
DG = 256
B_CHUNK = 128
PLE_DIM = 256
EPS = 1e-6
LOG2_E = 1.4426950408889634

TL = 256
HALF = TL // 2
N_LEVELS = int(math.log2(TL))
WEIGHT_CHUNK_3 = 1536
WEIGHT_CHUNK_1 = 512
VMEM_LIMIT = 58 * 1024 * 1024

F32 = jnp.float32
BF16 = jnp.bfloat16


def _build_level_consts(tl):
    t = np.arange(tl)[:, None]
    s = np.arange(tl)[None, :]
    tri = (s <= t).astype(np.float32)
    x = np.maximum(t ^ s, 1)
    p = np.floor(np.log2(x)).astype(np.int32)
    nlev = int(math.log2(tl))
    lvl = np.where(t == s, 0, np.where(t > s, nlev - p, -1)).astype(np.int32)
    return tri, lvl[:tl // 2, :tl // 2]


_TRI_NP, _LVL_NP = _build_level_consts(TL)


def _dot(a, b):
    return jnp.dot(a, b, preferred_element_type=F32)


def _dot_cols(a, w_refs):
    return jnp.concatenate([_dot(a, w[...]) for w in w_refs], axis=1)


def _dot_nt(a, b):
    return lax.dot_general(a, b, (((1,), (1,)), ((), ())), preferred_element_type=F32)


def _dot_tn(a, b):
    return lax.dot_general(a, b, (((0,), (0,)), ((), ())), preferred_element_type=F32)


def _rmsnorm(x, g):
    ms = jnp.mean(x * x, axis=-1, keepdims=True)
    return x * lax.rsqrt(ms + EPS) * g


def _silu(x):
    return x * jax.nn.sigmoid(x)


def _gelu_tanh(x):
    c = math.sqrt(2.0 / math.pi)
    return 0.5 * x * (1.0 + jnp.tanh(c * (x + 0.044715 * (x * x * x))))


def _forget_lower_bound(lbl, j):
    mx = jnp.max(lbl, axis=0, keepdims=True)
    e = jnp.exp(lbl - mx)
    sm = e / jnp.sum(e, axis=0, keepdims=True)
    cum0 = sm[0:1, :]
    cum = cum0
    for i in range(1, j + 1):
        cum = cum + sm[i:i + 1, :]
    return cum - cum0


def _forget_gates(fpre, lb):
    e = jnp.exp(-jnp.abs(fpre))
    log_sig = jnp.minimum(fpre, 0.0) - jnp.log1p(e)
    a = jnp.log(lb)
    y = jnp.log1p(-lb) + log_sig
    logf = jnp.maximum(a, y) + jnp.log1p(jnp.exp(-jnp.abs(a - y)))
    one_minus_f = (1.0 - lb) * (jnp.where(fpre >= 0.0, e, 1.0) / (1.0 + e))
    return logf, one_minus_f


def _hgrn_front(x, nm, w_in, lb):
    hn = _rmsnorm(x, nm).astype(BF16)
    proj = _dot_cols(hn, w_in)
    q = _silu(proj[:, :D_MODEL])
    logf, kk = _forget_gates(proj[:, D_MODEL:2 * D_MODEL], lb)
    v = proj[:, 2 * D_MODEL:2 * D_MODEL + D_INNER]
    z = proj[:, 2 * D_MODEL + D_INNER:]
    return q, logf, kk, v, z


def _head_rmsnorm(o, gn):
    outs = []
    for h in range(HEADS):
        outs.append(_rmsnorm(o[:, h * DV:(h + 1) * DV], gn))
    return jnp.concatenate(outs, axis=1)


def _ple(h, p, npl, w_gate, w_proj):
    gate = jax.nn.sigmoid(_dot_cols(_rmsnorm(h, npl).astype(BF16), w_gate))
    return h + gate * _dot_cols(p.astype(BF16), w_proj)


def _mixer_back(x, branch, w_out, p, npl, w_gate, w_proj):
    h = x + _dot_cols(branch.astype(BF16), w_out)
    return _ple(h, p, npl, w_gate, w_proj)


def _gmlp_front(x, nm, w_in, ln_g, ln_b):
    hn = _rmsnorm(x, nm).astype(BF16)
    proj = _dot_cols(hn, w_in)
    u = _gelu_tanh(proj[:, :D_INNER])
    vg = _gelu_tanh(proj[:, D_INNER:2 * D_INNER])
    z = proj[:, 2 * D_INNER:]
    mu = jnp.mean(vg, axis=-1, keepdims=True)
    vc = vg - mu
    vn = vc * lax.rsqrt(jnp.mean(vc * vc, axis=-1, keepdims=True) + EPS) * ln_g + ln_b
    return u, vn, z


def _hgrn_prompt_kernel(x_ref, p_ref, lbl_ref, nm_ref, win_ref, gn_ref, wout_ref, npl_ref,
                        wg_ref, wp_ref, tri_ref, lvl_ref, sq_ref, sf_ref, sk_ref, sv_ref, ss_ref,
                        zero_ref, nxt_refs, *rest, layer_j):
    n_out = 4 + len(nxt_refs)
    y_ref, sfin_ref, ssnew_ref, so_ref = rest[len(rest) - n_out - 1:len(rest) - n_out + 3]
    st_ref = rest[-1]
    l = pl.program_id(1)
    if nxt_refs:
        _cast_next_weights(pl.program_id(0) * pl.num_programs(1) + l, nxt_refs,
                           rest[len(rest) - n_out + 3:len(rest) - 1])

    @pl.when(l == 0)
    def _():
        st_ref[...] = jnp.zeros_like(st_ref)

    _sample_state_update(sq_ref, sf_ref, sk_ref, sv_ref, ss_ref, ssnew_ref, so_ref)

    x = x_ref[...]
    lb = _forget_lower_bound(lbl_ref[...], layer_j)
    q, logf, kk, v, z = _hgrn_front(x, nm_ref[...], win_ref, lb)
    done = lax.bitcast_convert_type(so_ref[0, :, :DK], jnp.int32) & zero_ref[...]
    q_patch = q[:8, :DK] + lax.bitcast_convert_type(done, F32)
    q = jnp.concatenate([jnp.concatenate([q_patch, q[:8, DK:]], axis=1), q[8:]], axis=0)

    lg = logf * LOG2_E
    g_hi = lg.astype(BF16)
    g_lo = (lg - g_hi.astype(F32)).astype(BF16)
    g2 = jnp.concatenate([g_hi, g_lo], axis=1)
    b2 = _dot(tri_ref[...], g2)
    b = b2[:, :D_MODEL] + b2[:, D_MODEL:]
    b_last = b[TL - 1:TL, :]
    qs = (q * jnp.exp2(b)).astype(BF16)
    kd = (kk * jnp.exp2(b_last - b)).astype(BF16)
    dec = jnp.exp2(b_last)
    qb = q.astype(BF16)
    kb = kk.astype(BF16)
    vb = v.astype(BF16)

    zk, zq = {}, {}
    for level in range(1, N_LEVELS + 1):
        hs = TL >> level
        if hs >= 8:
            grp = 2 * hs
            shp = (TL // grp, grp, D_MODEL)
            b3, q3, k3 = b.reshape(shp), q.reshape(shp), kk.reshape(shp)
            mid = b3[:, hs - 1:hs, :]
            zq3 = q3[:, hs:, :] * jnp.exp2(b3[:, hs:, :] - mid)
            zk3 = k3[:, :hs, :] * jnp.exp2(mid - b3[:, :hs, :])
            zq[level] = zq3.reshape(TL // 2, D_MODEL).astype(BF16)
            zk[level] = jnp.concatenate([zk3, zq3], axis=1).reshape(TL, D_MODEL).astype(BF16)
        else:
            shp = (TL // 8, 8, D_MODEL)
            b8, q8, k8 = b.reshape(shp), q.reshape(shp), kk.reshape(shp)
            sub = lax.broadcasted_iota(jnp.int32, shp, 1)
            is_q = (sub & hs) != 0
            if hs == 4:
                mid = jnp.broadcast_to(b8[:, 3:4, :], shp)
            elif hs == 2:
                mid = jnp.where(sub < 4, jnp.broadcast_to(b8[:, 1:2, :], shp),
                                jnp.broadcast_to(b8[:, 5:6, :], shp))
            else:
                mid = None
            if mid is None:
                e = jnp.where(is_q, lg.reshape(shp), 0.0)
            else:
                e = jnp.where(is_q, b8 - mid, mid - b8)
            zfine = jnp.where(is_q, q8, k8) * jnp.exp2(e)
            zk[level] = zfine.reshape(TL, D_MODEL).astype(BF16)

    lvl = lvl_ref[...]
    atts = []
    for h in range(HEADS):
        sl = slice(h * DK, (h + 1) * DK)
        pd = _dot_nt(qb[:, sl], kb[:, sl])
        diag = [jnp.where(lvl == 0, pd[:HALF, :HALF], 0.0),
                jnp.where(lvl == 0, pd[HALF:, HALF:], 0.0)]
        for level in range(N_LEVELS, 1, -1):
            hs = TL >> level
            zkl = zk[level][:, sl]
            if hs >= 8:
                grp = 2 * hs
                nq = HALF // 2
                pr = _dot_nt(zq[level][:, sl], zkl)
                hit = lvl.reshape(HALF // grp, grp, HALF)[:, hs:, :] == level
                for i in range(2):
                    pq = pr[i * nq:(i + 1) * nq, i * HALF:(i + 1) * HALF]
                    d3 = diag[i].reshape(HALF // grp, grp, HALF)
                    upper = jnp.where(hit, pq.reshape(HALF // grp, hs, HALF), d3[:, hs:, :])
                    diag[i] = jnp.concatenate([d3[:, :hs, :], upper], axis=1).reshape(HALF, HALF)
            else:
                pf = _dot_nt(zkl, zkl)
                hit = lvl == level
                diag[0] = jnp.where(hit, pf[:HALF, :HALF], diag[0])
                diag[1] = jnp.where(hit, pf[HALF:, HALF:], diag[1])
        a10 = _dot_nt(zq[1][:, sl], zk[1][:HALF, sl])
        att = jnp.concatenate(
            [jnp.concatenate([diag[0], jnp.zeros((HALF, HALF), F32)], axis=1),
             jnp.concatenate([a10, diag[1]], axis=1)], axis=0)
        atts.append(att.astype(BF16))
    outs = []
    for h in range(HEADS):
        sl = slice(h * DK, (h + 1) * DK)
        vsl = slice(h * DV, (h + 1) * DV)
        st = st_ref[h]
        o = _dot_nt(qs[:, sl], st.astype(BF16)) + _dot(atts[h], vb[:, vsl])
        st_ref[h] = st * dec[:, sl] + _dot_tn(vb[:, vsl], kd[:, sl])
        outs.append(o)
    o = jnp.concatenate(outs, axis=1)

    branch = _head_rmsnorm(o, gn_ref[...]) * _silu(z)
    y_ref[...] = _mixer_back(x, branch, wout_ref, p_ref[...], npl_ref[...],
                             wg_ref, wp_ref)

    @pl.when(l == pl.num_programs(1) - 1)
    def _():
        for h in range(HEADS):
            sfin_ref[h] = st_ref[h].T


def _gmlp_prompt_kernel(x_ref, p_ref, nm_ref, win_ref, lng_ref, lnb_ref, wsp_ref, bsp_ref,
                        wout_ref, npl_ref, wg_ref, wp_ref, nf_ref, samp_refs, nxt_refs, *rest,
                        final_norm):
    n_out = 4 + len(nxt_refs)
    y_ref, vt_ref, ys_ref, vts_ref = rest[len(rest) - n_out:len(rest) - n_out + 4]
    l = pl.program_id(1)
    step = pl.program_id(0) * pl.num_programs(1) + l
    if nxt_refs:
        _cast_next_weights(step, nxt_refs, rest[len(rest) - n_out + 4:])

    @pl.when(step == 0)
    def _():
        xs_ref, ps_ref, w00_ref, b0_ref = samp_refs
        _gmlp_sample_body(xs_ref, ps_ref, nm_ref, win_ref, lng_ref, lnb_ref, w00_ref, b0_ref,
                          wout_ref, npl_ref, wg_ref, wp_ref, nf_ref, ys_ref, vts_ref,
                          final_norm=final_norm)

    x = x_ref[...]
    u, vn, z = _gmlp_front(x, nm_ref[...], win_ref, lng_ref[...], lnb_ref[...])
    vb = vn.astype(BF16)
    r = lax.broadcasted_iota(jnp.int32, (B_CHUNK, B_CHUNK), 0)
    c = lax.broadcasted_iota(jnp.int32, (B_CHUNK, B_CHUNK), 1)
    tril = c <= r
    bias = bsp_ref[...]
    cols = []
    for g in range(GROUPS):
        w = jnp.where(tril, wsp_ref[g], 0.0).astype(BF16)
        gsl = slice(g * DG, (g + 1) * DG)
        chunks = []
        for ci in range(TL // B_CHUNK):
            rsl = slice(ci * B_CHUNK, (ci + 1) * B_CHUNK)
            chunks.append(_dot(w, vb[rsl, gsl]) + bias[:, gsl])
        cols.append(jnp.concatenate(chunks, axis=0))
    s = jnp.concatenate(cols, axis=1)
    branch = u * s * _silu(z)
    h = _mixer_back(x, branch, wout_ref, p_ref[...], npl_ref[...], wg_ref, wp_ref)
    if final_norm:
        h = _rmsnorm(h, nf_ref[...])
    y_ref[...] = h

    @pl.when(l == pl.num_programs(1) - 1)
    def _():
        vt_ref[...] = vn[TL - B_CHUNK:, :]


def _hgrn_sample_front_kernel(x_ref, lbl_ref, nm_ref, win_ref, q_ref, f_ref, k_ref, v_ref, z_ref,
                              *, layer_j):
    lb = _forget_lower_bound(lbl_ref[...], layer_j)
    q, logf, kk, v, z = _hgrn_front(x_ref[...], nm_ref[...], win_ref, lb)
    q_ref[...] = q
    f_ref[...] = jnp.exp(logf)
    k_ref[...] = kk
    v_ref[...] = v
    z_ref[...] = z


def _sample_state_update(q_ref, f_ref, k_ref, v_ref, s_ref, snew_ref, o_ref):
    for i in range(q_ref.shape[0]):
        q_t = q_ref[i].T
        f_t = f_ref[i].T
        k_t = k_ref[i].T
        for h in range(HEADS):
            s_new = f_t[:, h:h + 1] * s_ref[i, h] + k_t[:, h:h + 1] * v_ref[i, h:h + 1, :]
            snew_ref[i, h] = s_new
            o_ref[i, h:h + 1, :] = jnp.sum(q_t[:, h:h + 1] * s_new, axis=0, keepdims=True)


def _cast_next_weights(step, nxt_refs, out_refs):
    for src, dst in zip(nxt_refs[:3], out_refs[:3]):
        dst[...] = src[...].astype(BF16)

    @pl.when(step == 0)
    def _():
        out_refs[3][...] = nxt_refs[3][...].astype(BF16)


def _hgrn_sample_back_kernel(x_ref, o_ref, z_ref, p_ref, gn_ref, wout_ref, npl_ref, wg_ref,
                             wp_ref, y_ref):
    branch = _head_rmsnorm(o_ref[...], gn_ref[...]) * _silu(z_ref[...])
    y_ref[...] = _mixer_back(x_ref[...], branch, wout_ref, p_ref[...], npl_ref[...],
                             wg_ref, wp_ref)


def _gmlp_sample_body(x_ref, p_ref, nm_ref, win_ref, lng_ref, lnb_ref, w00_ref, b0_ref,
                      wout_ref, npl_ref, wg_ref, wp_ref, nf_ref, y_ref, vt_ref, *, final_norm):
    x = x_ref[...]
    u, vn, z = _gmlp_front(x, nm_ref[...], win_ref, lng_ref[...], lnb_ref[...])
    s = w00_ref[...] * vn + b0_ref[...]
    branch = u * s * _silu(z)
    h = _mixer_back(x, branch, wout_ref, p_ref[...], npl_ref[...], wg_ref, wp_ref)
    if final_norm:
        h = _rmsnorm(h, nf_ref[...])
    y_ref[...] = h
    vt_ref[...] = vn


def _resident(shape, index):
    return pl.BlockSpec(shape, index, pipeline_mode=pl.Buffered(1))


def _chunk_width(n):
    width = WEIGHT_CHUNK_3 if n % WEIGHT_CHUNK_3 == 0 else WEIGHT_CHUNK_1
    assert n % width == 0 and width % 1024 != 0
    return width


def _weight_cols(w):
    k, n = w.shape
    width = _chunk_width(n)
    specs = [_resident((k, width), lambda *_, c=c: (0, c)) for c in range(n // width)]
    return [w] * len(specs), specs


def _next_weight_specs(nxt, n_steps, step_of):
    ops, ins, outs, shapes = [], [], [], []
    for idx, (stack, layer) in enumerate(nxt or ()):
        _, k, n = stack.shape
        ops.append(stack)
        shapes.append(jax.ShapeDtypeStruct((k, n), BF16))
        if idx < 3:
            rows = k // n_steps
            assert rows * n_steps == k and rows % 16 == 0
            ins.append(pl.BlockSpec((None, rows, n),
                                    lambda *g, layer=layer: (layer, step_of(*g), 0)))
            outs.append(pl.BlockSpec((rows, n), lambda *g: (step_of(*g), 0)))
        else:
            ins.append(_resident((None, k, n), lambda *g, layer=layer: (layer, 0, 0)))
            outs.append(pl.BlockSpec((k, n), lambda *g: (0, 0)))
    return ops, ins, outs, shapes


def _expand(args):
    ops, specs, sizes = [], [], []
    for o, s in args:
        if isinstance(o, list):
            ops += o
            specs += s
            sizes.append(len(o))
        else:
            ops.append(o)
            specs.append(s)
            sizes.append(None)
    return ops, specs, sizes


def _regroup(fn, sizes):
    def wrapped(*refs):
        pos, grouped = 0, []
        for n in sizes:
            if n is None:
                grouped.append(refs[pos])
                pos += 1
            else:
                grouped.append(tuple(refs[pos:pos + n]))
                pos += n
        return fn(*grouped, *refs[pos:])
    return wrapped


def _params(*sem):
    return pltpu.CompilerParams(dimension_semantics=sem, vmem_limit_bytes=VMEM_LIMIT)


def _hgrn_prompt_layer(x, p_all, i, j, lb_logits, norm_mix, w_in, gnorm, w_out, norm_ple,
                       w_gate, w_proj, tri, lvl, sample_qfkv, s0, s_prev, fin_prev, nxt):
    bsz, seq, _ = x.shape
    n_a = lb_logits.shape[0]
    n_l = seq // TL
    grid = (bsz, n_l)
    n_s = s0.shape[1]
    assert n_s % (bsz * n_l) == 0
    spt = n_s // (bsz * n_l)
    const2 = lambda b, l: (0, 0)
    nxt_ops, nxt_in, nxt_out, nxt_shapes = _next_weight_specs(
        nxt, bsz * n_l, lambda b, l: b * n_l + l)
    s_k = pl.BlockSpec((spt, HEADS, DK), lambda b, l: (b * n_l + l, 0, 0))
    s_v = pl.BlockSpec((spt, HEADS, DV), lambda b, l: (b * n_l + l, 0, 0))
    s_st = pl.BlockSpec((None, spt, HEADS, DK, DV), lambda b, l: (j, b * n_l + l, 0, 0, 0))
    sq, sf, sk, sv = sample_qfkv
    operands, in_specs, sizes = _expand([
        (x, pl.BlockSpec((None, TL, D_MODEL), lambda b, l: (b, l, 0))),
        (p_all, pl.BlockSpec((None, None, TL, PLE_DIM), lambda b, l: (i, b, l, 0))),
        (lb_logits, _resident((n_a, D_MODEL), const2)),
        (norm_mix, _resident((None, 1, D_MODEL), lambda b, l: (i, 0, 0))),
        _weight_cols(w_in),
        (gnorm, _resident((None, 1, DV), lambda b, l: (j, 0, 0))),
        _weight_cols(w_out),
        (norm_ple, _resident((None, 1, D_MODEL), lambda b, l: (i, 0, 0))),
        _weight_cols(w_gate),
        _weight_cols(w_proj),
        (tri, _resident(tri.shape, const2)),
        (lvl, _resident(lvl.shape, const2)),
        (sq, s_k), (sf, s_k), (sk, s_k), (sv, s_v), (s0, s_st),
        (jnp.zeros((HEADS, DK), jnp.int32), _resident((HEADS, DK), const2)),
        (nxt_ops, nxt_in),
    ])
    aliases = {}
    for prev, out_idx in ((s_prev, 2), (fin_prev, 1)):
        if prev is not None:
            operands.append(prev)
            in_specs.append(pl.BlockSpec(memory_space=pl.ANY))
            aliases[len(operands) - 1] = out_idx
    out_specs = [
        pl.BlockSpec((None, TL, D_MODEL), lambda b, l: (b, l, 0)),
        pl.BlockSpec((None, None, HEADS, DK, DV), lambda b, l: (j, b, 0, 0, 0)),
        s_st,
        s_v,
    ] + nxt_out
    out_shape = [
        jax.ShapeDtypeStruct((bsz, seq, D_MODEL), F32),
        jax.ShapeDtypeStruct((n_a, bsz, HEADS, DK, DV), F32),
        jax.ShapeDtypeStruct(s0.shape, F32),
        jax.ShapeDtypeStruct((n_s, HEADS, DV), F32),
    ] + nxt_shapes
    return pl.pallas_call(
        _regroup(functools.partial(_hgrn_prompt_kernel, layer_j=j), sizes),
        grid=grid, in_specs=in_specs, out_specs=out_specs, out_shape=out_shape,
        scratch_shapes=[pltpu.VMEM((HEADS, DV, DK), F32)],
        input_output_aliases=aliases,
        compiler_params=_params("arbitrary", "arbitrary"),
        name=f"hgrn_prompt_{j}",
    )(*operands)


def _gmlp_prompt_layer(x, p_all, i, j, norm_mix, w_in, ln_g, ln_b, w_sp, bias_full, w_out,
                       norm_ple, w_gate, w_proj, norm_final, final_norm, n_b, vt_prev, nxt, sample):
    xs, ps_all, w00, b0 = sample
    n_s = xs.shape[0]
    bsz, seq, _ = x.shape
    n_l = seq // TL
    grid = (bsz, n_l)
    nxt_ops, nxt_in, nxt_out, nxt_shapes = _next_weight_specs(
        nxt, bsz * n_l, lambda b, l: b * n_l + l)
    operands, in_specs, sizes = _expand([
        (x, pl.BlockSpec((None, TL, D_MODEL), lambda b, l: (b, l, 0))),
        (p_all, pl.BlockSpec((None, None, TL, PLE_DIM), lambda b, l: (i, b, l, 0))),
        (norm_mix, _resident((None, 1, D_MODEL), lambda b, l: (i, 0, 0))),
        _weight_cols(w_in),
        (ln_g, _resident((None, 1, D_INNER), lambda b, l: (j, 0, 0))),
        (ln_b, _resident((None, 1, D_INNER), lambda b, l: (j, 0, 0))),
        (w_sp, _resident((None, GROUPS, B_CHUNK, B_CHUNK), lambda b, l: (j, 0, 0, 0))),
        (bias_full, _resident((None, B_CHUNK, D_INNER), lambda b, l: (j, 0, 0))),
        _weight_cols(w_out),
        (norm_ple, _resident((None, 1, D_MODEL), lambda b, l: (i, 0, 0))),
        _weight_cols(w_gate),
        _weight_cols(w_proj),
        (norm_final, _resident((1, D_MODEL), lambda b, l: (0, 0))),
        ([xs, ps_all, w00, b0],
         [_resident((n_s, D_MODEL), lambda b, l: (0, 0)),
          _resident((None, n_s, PLE_DIM), lambda b, l: (i, 0, 0)),
          _resident((None, 1, D_INNER), lambda b, l: (j, 0, 0)),
          _resident((None, 1, D_INNER), lambda b, l: (j, 0, 0))]),
        (nxt_ops, nxt_in),
    ])
    aliases = {}
    if vt_prev is not None:
        operands.append(vt_prev)
        in_specs.append(pl.BlockSpec(memory_space=pl.ANY))
        aliases[len(operands) - 1] = 1
    out_specs = [
        pl.BlockSpec((None, TL, D_MODEL), lambda b, l: (b, l, 0)),
        pl.BlockSpec((None, None, B_CHUNK, D_INNER), lambda b, l: (j, b, 0, 0)),
        pl.BlockSpec((n_s, D_MODEL), lambda b, l: (0, 0)),
        pl.BlockSpec((n_s, D_INNER), lambda b, l: (0, 0)),
    ] + nxt_out
    out_shape = [
        jax.ShapeDtypeStruct((bsz, seq, D_MODEL), F32),
        jax.ShapeDtypeStruct((n_b, bsz, B_CHUNK, D_INNER), F32),
        jax.ShapeDtypeStruct((n_s, D_MODEL), F32),
        jax.ShapeDtypeStruct((n_s, D_INNER), F32),
    ] + nxt_shapes
    return pl.pallas_call(
        _regroup(functools.partial(_gmlp_prompt_kernel, final_norm=final_norm), sizes),
        grid=grid, in_specs=in_specs, out_specs=out_specs, out_shape=out_shape,
        input_output_aliases=aliases,
        compiler_params=_params("arbitrary", "arbitrary"),
        name=f"gmlp_prompt_{j}",
    )(*operands)


def _whole(shape):
    nd = len(shape)
    return pl.BlockSpec(shape, lambda *_: (0,) * nd)


def _hgrn_sample_front(x, i, j, lb_logits, norm_mix, w_in):
    n = x.shape[0]
    n_a = lb_logits.shape[0]
    operands, in_specs, sizes = _expand([
        (x, _whole((n, D_MODEL))),
        (lb_logits, _whole((n_a, D_MODEL))),
        (norm_mix, pl.BlockSpec((None, 1, D_MODEL), lambda g: (i, 0, 0))),
        _weight_cols(w_in),
    ])
    q, f, kk, v, z = pl.pallas_call(
        _regroup(functools.partial(_hgrn_sample_front_kernel, layer_j=j), sizes),
        grid=(1,),
        in_specs=in_specs,
        out_specs=[_whole((n, D_MODEL))] * 3 + [_whole((n, D_INNER))] * 2,
        out_shape=[jax.ShapeDtypeStruct((n, D_MODEL), F32)] * 3
        + [jax.ShapeDtypeStruct((n, D_INNER), F32)] * 2,
        compiler_params=_params("arbitrary"),
        name=f"hgrn_sample_front_{j}",
    )(*operands)

    qfkv = (q.reshape(n, HEADS, DK), f.reshape(n, HEADS, DK), kk.reshape(n, HEADS, DK),
            v.reshape(n, HEADS, DV))
    return qfkv, z


def _hgrn_sample_back(x, o, z, p_all, i, j, gnorm, w_out, norm_ple, w_gate, w_proj):
    n = x.shape[0]
    operands, in_specs, sizes = _expand([
        (x, _whole((n, D_MODEL))),
        (o.reshape(n, D_INNER), _whole((n, D_INNER))),
        (z, _whole((n, D_INNER))),
        (p_all, pl.BlockSpec((None, n, PLE_DIM), lambda g: (i, 0, 0))),
        (gnorm, pl.BlockSpec((None, 1, DV), lambda g: (j, 0, 0))),
        _weight_cols(w_out),
        (norm_ple, pl.BlockSpec((None, 1, D_MODEL), lambda g: (i, 0, 0))),
        _weight_cols(w_gate),
        _weight_cols(w_proj),
    ])
    return pl.pallas_call(
        _regroup(_hgrn_sample_back_kernel, sizes),
        grid=(1,),
        in_specs=in_specs,
        out_specs=_whole((n, D_MODEL)),
        out_shape=jax.ShapeDtypeStruct((n, D_MODEL), F32),
        compiler_params=_params("arbitrary"),
        name=f"hgrn_sample_back_{j}",
    )(*operands)


def kernel(x_prompt, x_sample, state_hgrn, p_prompt, p_sample, norm_mix, w_in_a, lb_logits,
           gnorm_a, w_out_a, w_in_b, ln_v_g, ln_v_b, w_spatial, b_spatial, w_out_b,
           norm_ple, w_ple_gate, w_ple_proj, norm_final):
    depth = norm_mix.shape[0]
    n_s = x_sample.shape[0]
    assert x_prompt.shape[1] % TL == 0 and TL % B_CHUNK == 0 and x_sample.shape[1] == 1

    def layer_weights(i):
        mixer = (w_in_a, w_out_a) if i % 2 == 0 else (w_in_b, w_out_b)
        return ((mixer[0], i // 2), (mixer[1], i // 2), (w_ple_gate, i), (w_ple_proj, i))

    wb = tuple(stack[idx].astype(BF16) for stack, idx in layer_weights(0))

    tri = jnp.asarray(_TRI_NP, dtype=BF16)
    lvl = jnp.asarray(_LVL_NP)
    nf = norm_final.reshape(1, D_MODEL)
    bias_full = jnp.repeat(jnp.swapaxes(b_spatial, 1, 2), DG, axis=2)
    w00 = jnp.repeat(w_spatial[:, :, 0, 0], DG, axis=1)[:, None, :]
    b0 = jnp.repeat(b_spatial[:, :, 0], DG, axis=1)[:, None, :]
    nmix = norm_mix[:, None, :]
    nple = norm_ple[:, None, :]
    gn = gnorm_a[:, None, :]
    lng = ln_v_g[:, None, :]
    lnb = ln_v_b[:, None, :]

    if depth % 2 == 1:
        raise NotImplementedError("final norm is fused into the last (chunk-MLP) layer")
    hp = x_prompt
    hs = x_sample.reshape(n_s, D_MODEL)
    ps = p_sample.reshape(depth, n_s, PLE_DIM)
    vt_s = []
    st_p = st_s = vt_p = None
    for i in range(depth):
        j = i // 2
        last = i == depth - 1
        nxt = None if last else layer_weights(i + 1)
        w_in, w_out, w_gate, w_proj = wb
        if i % 2 == 0:
            qfkv, z_s = _hgrn_sample_front(hs, i, j, lb_logits, nmix, w_in)
            hp, st_p, st_s, o_s, *wb = _hgrn_prompt_layer(
                hp, p_prompt, i, j, lb_logits, nmix, w_in, gn, w_out, nple, w_gate, w_proj,
                tri, lvl, qfkv, state_hgrn, st_s, st_p, nxt)
            hs = _hgrn_sample_back(hs, o_s, z_s, ps, i, j, gn, w_out, nple, w_gate, w_proj)
        else:
            hp, vt_p, hs, vts, *wb = _gmlp_prompt_layer(
                hp, p_prompt, i, j, nmix, w_in, lng, lnb, w_spatial, bias_full, w_out, nple,
                w_gate, w_proj, nf, last, w_in_b.shape[0], vt_p, nxt, (hs, ps, w00, b0))
            vt_s.append(vts.reshape(n_s, 1, D_INNER))
    return (hp, hs.reshape(n_s, 1, D_MODEL), st_p, st_s, vt_p, jnp.stack(vt_s))
```

```python
import functools
import math

import numpy as np
import jax
import jax.numpy as jnp
from jax import lax
from jax.experimental import pallas as pl
from jax.experimental.pallas import tpu as pltpu

D_MODEL = 1024
D_INNER = 2048
HEADS = 8
DK = 128
DV = 256
GROUPS = 8
DG = 256
B_CHUNK = 128
PLE_DIM = 256
EPS = 1e-6
LOG2_E = 1.4426950408889634

TL = 256
TLB = 256
HALF = TL // 2
N_LEVELS = int(math.log2(TL))
WEIGHT_CHUNK_3 = 1536
WEIGHT_CHUNK_1 = 512
VMEM_LIMIT = 58 * 1024 * 1024

F32 = jnp.float32
BF16 = jnp.bfloat16


def _build_level_consts(tl):
    t = np.arange(tl)[:, None]
    s = np.arange(tl)[None, :]
    tri = (s <= t).astype(np.float32)
    x = np.maximum(t ^ s, 1)
    p = np.floor(np.log2(x)).astype(np.int32)
    nlev = int(math.log2(tl))
    lvl = np.where(t == s, 0, np.where(t > s, nlev - p, -1)).astype(np.int32)
    return tri, lvl[:tl // 2, :tl // 2]


_TRI_NP, _LVL_NP = _build_level_consts(TL)


def _dot(a, b):
    return jnp.dot(a, b, preferred_element_type=F32)


def _dot_cols(a, w_refs):
    return jnp.concatenate([_dot(a, w[...]) for w in w_refs], axis=1)


def _dot_nt(a, b):
    return lax.dot_general(a, b, (((1,), (1,)), ((), ())), preferred_element_type=F32)


def _dot_tn(a, b):
    return lax.dot_general(a, b, (((0,), (0,)), ((), ())), preferred_element_type=F32)


def _rmsnorm(x, g):
    ms = jnp.mean(x * x, axis=-1, keepdims=True)
    return x * lax.rsqrt(ms + EPS) * g


def _sigmoid(x):
    return 0.5 * jnp.tanh(0.5 * x) + 0.5


def _silu(x):
    hx = 0.5 * x
    return hx * jnp.tanh(hx) + hx


def _gelu_tanh(x):
    c = math.sqrt(2.0 / math.pi)
    return 0.5 * x * (1.0 + jnp.tanh(c * (x + 0.044715 * (x * x * x))))


def _forget_lower_bound(lbl, j):
    mx = jnp.max(lbl, axis=0, keepdims=True)
    e = jnp.exp(lbl - mx)
    sm = e / jnp.sum(e, axis=0, keepdims=True)
    cum0 = sm[0:1, :]
    cum = cum0
    for i in range(1, j + 1):
        cum = cum + sm[i:i + 1, :]
    return cum - cum0


def _forget_gates(fpre, lb):
    e = jnp.exp(-jnp.abs(fpre))
    log_sig = jnp.minimum(fpre, 0.0) - jnp.log1p(e)
    a = jnp.log(lb)
    y = jnp.log1p(-lb) + log_sig
    logf = jnp.maximum(a, y) + jnp.log1p(jnp.exp(-jnp.abs(a - y)))
    one_minus_f = (1.0 - lb) * (jnp.where(fpre >= 0.0, e, 1.0) / (1.0 + e))
    return logf, one_minus_f


def _hgrn_front(x, nm, w_in, lb):
    hn = _rmsnorm(x, nm).astype(BF16)
    proj = _dot_cols(hn, w_in)
    q = _silu(proj[:, :D_MODEL])
    logf, kk = _forget_gates(proj[:, D_MODEL:2 * D_MODEL], lb)
    v = proj[:, 2 * D_MODEL:2 * D_MODEL + D_INNER]
    z = proj[:, 2 * D_MODEL + D_INNER:]
    return q, logf, kk, v, z


def _head_rmsnorm(o, gn):
    outs = []
    for h in range(HEADS):
        outs.append(_rmsnorm(o[:, h * DV:(h + 1) * DV], gn))
    return jnp.concatenate(outs, axis=1)


def _ple(h, p, npl, w_gate, w_proj):
    gate = _sigmoid(_dot_cols(_rmsnorm(h, npl).astype(BF16), w_gate))
    return h + gate * _dot_cols(p.astype(BF16), w_proj)


def _mixer_back(x, branch, w_out, p, npl, w_gate, w_proj):
    h = x + _dot_cols(branch.astype(BF16), w_out)
    return _ple(h, p, npl, w_gate, w_proj)


def _gmlp_front(x, nm, w_in, ln_g, ln_b):
    hn = _rmsnorm(x, nm).astype(BF16)
    proj = _dot_cols(hn, w_in)
    u = _gelu_tanh(proj[:, :D_INNER])
    vg = _gelu_tanh(proj[:, D_INNER:2 * D_INNER])
    z = proj[:, 2 * D_INNER:]
    mu = jnp.mean(vg, axis=-1, keepdims=True)
    vc = vg - mu
    vn = vc * lax.rsqrt(jnp.mean(vc * vc, axis=-1, keepdims=True) + EPS) * ln_g + ln_b
    return u, vn, z


def _hgrn_prompt_kernel(x_ref, p_ref, lbl_ref, nm_ref, win_ref, gn_ref, wout_ref, npl_ref,
                        wg_ref, wp_ref, tri_ref, lvl_ref, sq_ref, sf_ref, sk_ref, sv_ref, ss_ref,
                        zero_ref, nxt_refs, *rest, layer_j):
    n_out = 4 + len(nxt_refs)
    y_ref, sfin_ref, ssnew_ref, so_ref = rest[len(rest) - n_out - 1:len(rest) - n_out + 3]
    st_ref = rest[-1]
    l = pl.program_id(1)
    if nxt_refs:
        _cast_next_weights(pl.program_id(0) * pl.num_programs(1) + l, nxt_refs,
                           rest[len(rest) - n_out + 3:len(rest) - 1])

    @pl.when(l == 0)
    def _():
        st_ref[...] = jnp.zeros_like(st_ref)

    _sample_state_update(sq_ref, sf_ref, sk_ref, sv_ref, ss_ref, ssnew_ref, so_ref)

    x = x_ref[...]
    lb = _forget_lower_bound(lbl_ref[...], layer_j)
    q, logf, kk, v, z = _hgrn_front(x, nm_ref[...], win_ref, lb)
    done = lax.bitcast_convert_type(so_ref[0, :, :DK], jnp.int32) & zero_ref[...]
    q_patch = q[:8, :DK] + lax.bitcast_convert_type(done, F32)
    q = jnp.concatenate([jnp.concatenate([q_patch, q[:8, DK:]], axis=1), q[8:]], axis=0)

    lg = logf * LOG2_E
    g_hi = lg.astype(BF16)
    g_lo = (lg - g_hi.astype(F32)).astype(BF16)
    g2 = jnp.concatenate([g_hi, g_lo], axis=1)
    b2 = _dot(tri_ref[...], g2)
    b = b2[:, :D_MODEL] + b2[:, D_MODEL:]
    b_last = b[TL - 1:TL, :]
    qs = (q * jnp.exp2(b)).astype(BF16)
    kd = (kk * jnp.exp2(b_last - b)).astype(BF16)
    dec = jnp.exp2(b_last)
    qb = q.astype(BF16)
    kb = kk.astype(BF16)
    vb = v.astype(BF16)

    zk, zq = {}, {}
    for level in range(1, N_LEVELS + 1):
        hs = TL >> level
        if hs >= 8:
            grp = 2 * hs
            shp = (TL // grp, grp, D_MODEL)
            b3, q3, k3 = b.reshape(shp), q.reshape(shp), kk.reshape(shp)
            mid = b3[:, hs - 1:hs, :]
            zq3 = q3[:, hs:, :] * jnp.exp2(b3[:, hs:, :] - mid)
            zk3 = k3[:, :hs, :] * jnp.exp2(mid - b3[:, :hs, :])
            zq[level] = zq3.reshape(TL // 2, D_MODEL).astype(BF16)
            zk[level] = jnp.concatenate([zk3, zq3], axis=1).reshape(TL, D_MODEL).astype(BF16)
        else:
            shp = (TL // 8, 8, D_MODEL)
            b8, q8, k8 = b.reshape(shp), q.reshape(shp), kk.reshape(shp)
            sub = lax.broadcasted_iota(jnp.int32, shp, 1)
            is_q = (sub & hs) != 0
            if hs == 4:
                mid = jnp.broadcast_to(b8[:, 3:4, :], shp)
            elif hs == 2:
                mid = jnp.where(sub < 4, jnp.broadcast_to(b8[:, 1:2, :], shp),
                                jnp.broadcast_to(b8[:, 5:6, :], shp))
            else:
                mid = None
            if mid is None:
                e = jnp.where(is_q, lg.reshape(shp), 0.0)
            else:
                e = jnp.where(is_q, b8 - mid, mid - b8)
            zfine = jnp.where(is_q, q8, k8) * jnp.exp2(e)
            zk[level] = zfine.reshape(TL, D_MODEL).astype(BF16)

    lvl = lvl_ref[...]
    atts = []
    for h in range(HEADS):
        sl = slice(h * DK, (h + 1) * DK)
        pd = _dot_nt(qb[:, sl], kb[:, sl])
        diag = [jnp.where(lvl == 0, pd[:HALF, :HALF], 0.0),
                jnp.where(lvl == 0, pd[HALF:, HALF:], 0.0)]
        for level in range(N_LEVELS, 1, -1):
            hs = TL >> level
            zkl = zk[level][:, sl]
            if hs >= 8:
                grp = 2 * hs
                nq = HALF // 2
                pr = _dot_nt(zq[level][:, sl], zkl)
                hit = lvl.reshape(HALF // grp, grp, HALF)[:, hs:, :] == level
                for i in range(2):
                    pq = pr[i * nq:(i + 1) * nq, i * HALF:(i + 1) * HALF]
                    d3 = diag[i].reshape(HALF // grp, grp, HALF)
                    upper = jnp.where(hit, pq.reshape(HALF // grp, hs, HALF), d3[:, hs:, :])
                    diag[i] = jnp.concatenate([d3[:, :hs, :], upper], axis=1).reshape(HALF, HALF)
            else:
                pf = _dot_nt(zkl, zkl)
                hit = lvl == level
                diag[0] = jnp.where(hit, pf[:HALF, :HALF], diag[0])
                diag[1] = jnp.where(hit, pf[HALF:, HALF:], diag[1])
        a10 = _dot_nt(zq[1][:, sl], zk[1][:HALF, sl])
        att = jnp.concatenate(
            [jnp.concatenate([diag[0], jnp.zeros((HALF, HALF), F32)], axis=1),
             jnp.concatenate([a10, diag[1]], axis=1)], axis=0)
        atts.append(att.astype(BF16))
    outs = []
    for h in range(HEADS):
        sl = slice(h * DK, (h + 1) * DK)
        vsl = slice(h * DV, (h + 1) * DV)
        st = st_ref[h]
        o = _dot_nt(qs[:, sl], st.astype(BF16)) + _dot(atts[h], vb[:, vsl])
        st_ref[h] = st * dec[:, sl] + _dot_tn(vb[:, vsl], kd[:, sl])
        outs.append(o)
    o = jnp.concatenate(outs, axis=1)

    branch = _head_rmsnorm(o, gn_ref[...]) * _silu(z)
    y_ref[...] = _mixer_back(x, branch, wout_ref, p_ref[...], npl_ref[...],
                             wg_ref, wp_ref)

    @pl.when(l == pl.num_programs(1) - 1)
    def _():
        for h in range(HEADS):
            sfin_ref[h] = st_ref[h].T


def _gmlp_prompt_kernel(x_ref, p_ref, nm_ref, win_ref, lng_ref, lnb_ref, wsp_ref, bsp_ref,
                        wout_ref, npl_ref, wg_ref, wp_ref, nf_ref, samp_refs, nxt_refs, *rest,
                        final_norm):
    n_out = 4 + len(nxt_refs)
    y_ref, vt_ref, ys_ref, vts_ref = rest[len(rest) - n_out:len(rest) - n_out + 4]
    l = pl.program_id(1)
    step = pl.program_id(0) * pl.num_programs(1) + l
    if nxt_refs:
        _cast_next_weights(step, nxt_refs, rest[len(rest) - n_out + 4:])

    @pl.when(step == 0)
    def _():
        xs_ref, ps_ref, w00_ref, b0_ref = samp_refs
        _gmlp_sample_body(xs_ref, ps_ref, nm_ref, win_ref, lng_ref, lnb_ref, w00_ref, b0_ref,
                          wout_ref, npl_ref, wg_ref, wp_ref, nf_ref, ys_ref, vts_ref,
                          final_norm=final_norm)

    x = x_ref[...]
    u, vn, z = _gmlp_front(x, nm_ref[...], win_ref, lng_ref[...], lnb_ref[...])
    vb = vn.astype(BF16)
    r = lax.broadcasted_iota(jnp.int32, (B_CHUNK, B_CHUNK), 0)
    c = lax.broadcasted_iota(jnp.int32, (B_CHUNK, B_CHUNK), 1)
    tril = c <= r
    bias = bsp_ref[...]
    cols = []
    for g in range(GROUPS):
        w = jnp.where(tril, wsp_ref[g], 0.0).astype(BF16)
        gsl = slice(g * DG, (g + 1) * DG)
        chunks = []
        for ci in range(TLB // B_CHUNK):
            rsl = slice(ci * B_CHUNK, (ci + 1) * B_CHUNK)
            chunks.append(_dot(w, vb[rsl, gsl]) + bias[:, gsl])
        cols.append(jnp.concatenate(chunks, axis=0))
    s = jnp.concatenate(cols, axis=1)
    branch = u * s * _silu(z)
    h = _mixer_back(x, branch, wout_ref, p_ref[...], npl_ref[...], wg_ref, wp_ref)
    if final_norm:
        h = _rmsnorm(h, nf_ref[...])
    y_ref[...] = h

    @pl.when(l == pl.num_programs(1) - 1)
    def _():
        vt_ref[...] = vn[TLB - B_CHUNK:, :]


def _hgrn_sample_front_kernel(x_ref, lbl_ref, nm_ref, win_ref, q_ref, f_ref, k_ref, v_ref, z_ref,
                              *, layer_j):
    lb = _forget_lower_bound(lbl_ref[...], layer_j)
    q, logf, kk, v, z = _hgrn_front(x_ref[...], nm_ref[...], win_ref, lb)
    q_ref[...] = q
    f_ref[...] = jnp.exp(logf)
    k_ref[...] = kk
    v_ref[...] = v
    z_ref[...] = z


def _sample_state_update(q_ref, f_ref, k_ref, v_ref, s_ref, snew_ref, o_ref):
    for i in range(q_ref.shape[0]):
        q_t = q_ref[i].T
        f_t = f_ref[i].T
        k_t = k_ref[i].T
        for h in range(HEADS):
            s_new = f_t[:, h:h + 1] * s_ref[i, h] + k_t[:, h:h + 1] * v_ref[i, h:h + 1, :]
            snew_ref[i, h] = s_new
            o_ref[i, h:h + 1, :] = jnp.sum(q_t[:, h:h + 1] * s_new, axis=0, keepdims=True)


def _cast_next_weights(step, nxt_refs, out_refs):
    for src, dst in zip(nxt_refs[:3], out_refs[:3]):
        dst[...] = src[...].astype(BF16)

    @pl.when(step == 0)
    def _():
        out_refs[3][...] = nxt_refs[3][...].astype(BF16)


def _hgrn_sample_back_kernel(x_ref, o_ref, z_ref, p_ref, gn_ref, wout_ref, npl_ref, wg_ref,
                             wp_ref, y_ref):
    branch = _head_rmsnorm(o_ref[...], gn_ref[...]) * _silu(z_ref[...])
    y_ref[...] = _mixer_back(x_ref[...], branch, wout_ref, p_ref[...], npl_ref[...],
                             wg_ref, wp_ref)


def _gmlp_sample_body(x_ref, p_ref, nm_ref, win_ref, lng_ref, lnb_ref, w00_ref, b0_ref,
                      wout_ref, npl_ref, wg_ref, wp_ref, nf_ref, y_ref, vt_ref, *, final_norm):
    x = x_ref[...]
    u, vn, z = _gmlp_front(x, nm_ref[...], win_ref, lng_ref[...], lnb_ref[...])
    s = w00_ref[...] * vn + b0_ref[...]
    branch = u * s * _silu(z)
    h = _mixer_back(x, branch, wout_ref, p_ref[...], npl_ref[...], wg_ref, wp_ref)
    if final_norm:
        h = _rmsnorm(h, nf_ref[...])
    y_ref[...] = h
    vt_ref[...] = vn


def _resident(shape, index):
    return pl.BlockSpec(shape, index, pipeline_mode=pl.Buffered(1))


def _chunk_width(n):
    width = WEIGHT_CHUNK_3 if n % WEIGHT_CHUNK_3 == 0 else WEIGHT_CHUNK_1
    assert n % width == 0 and width % 1024 != 0
    return width


def _weight_cols(w):
    k, n = w.shape
    width = _chunk_width(n)
    specs = [_resident((k, width), lambda *_, c=c: (0, c)) for c in range(n // width)]
    return [w] * len(specs), specs


def _next_weight_specs(nxt, n_steps, step_of):
    ops, ins, outs, shapes = [], [], [], []
    for idx, (stack, layer) in enumerate(nxt or ()):
        _, k, n = stack.shape
        ops.append(stack)
        shapes.append(jax.ShapeDtypeStruct((k, n), BF16))
        if idx < 3:
            rows = k // n_steps
            assert rows * n_steps == k and rows % 16 == 0
            ins.append(pl.BlockSpec((None, rows, n),
                                    lambda *g, layer=layer: (layer, step_of(*g), 0)))
            outs.append(pl.BlockSpec((rows, n), lambda *g: (step_of(*g), 0)))
        else:
            ins.append(_resident((None, k, n), lambda *g, layer=layer: (layer, 0, 0)))
            outs.append(pl.BlockSpec((k, n), lambda *g: (0, 0)))
    return ops, ins, outs, shapes


def _expand(args):
    ops, specs, sizes = [], [], []
    for o, s in args:
        if isinstance(o, list):
            ops += o
            specs += s
            sizes.append(len(o))
        else:
            ops.append(o)
            specs.append(s)
            sizes.append(None)
    return ops, specs, sizes


def _regroup(fn, sizes):
    def wrapped(*refs):
        pos, grouped = 0, []
        for n in sizes:
            if n is None:
                grouped.append(refs[pos])
                pos += 1
            else:
                grouped.append(tuple(refs[pos:pos + n]))
                pos += n
        return fn(*grouped, *refs[pos:])
    return wrapped


def _params(*sem):
    return pltpu.CompilerParams(dimension_semantics=sem, vmem_limit_bytes=VMEM_LIMIT)


def _hgrn_prompt_layer(x, p_all, i, j, lb_logits, norm_mix, w_in, gnorm, w_out, norm_ple,
                       w_gate, w_proj, tri, lvl, sample_qfkv, s0, s_prev, fin_prev, nxt):
    bsz, seq, _ = x.shape
    n_a = lb_logits.shape[0]
    n_l = seq // TL
    grid = (bsz, n_l)
    n_s = s0.shape[1]
    assert n_s % (bsz * n_l) == 0
    spt = n_s // (bsz * n_l)
    const2 = lambda b, l: (0, 0)
    nxt_ops, nxt_in, nxt_out, nxt_shapes = _next_weight_specs(
        nxt, bsz * n_l, lambda b, l: b * n_l + l)
    s_k = pl.BlockSpec((spt, HEADS, DK), lambda b, l: (b * n_l + l, 0, 0))
    s_v = pl.BlockSpec((spt, HEADS, DV), lambda b, l: (b * n_l + l, 0, 0))
    s_st = pl.BlockSpec((None, spt, HEADS, DK, DV), lambda b, l: (j, b * n_l + l, 0, 0, 0))
    sq, sf, sk, sv = sample_qfkv
    operands, in_specs, sizes = _expand([
        (x, pl.BlockSpec((None, TL, D_MODEL), lambda b, l: (b, l, 0))),
        (p_all, pl.BlockSpec((None, None, TL, PLE_DIM), lambda b, l: (i, b, l, 0))),
        (lb_logits, _resident((n_a, D_MODEL), const2)),
        (norm_mix, _resident((None, 1, D_MODEL), lambda b, l: (i, 0, 0))),
        _weight_cols(w_in),
        (gnorm, _resident((None, 1, DV), lambda b, l: (j, 0, 0))),
        _weight_cols(w_out),
        (norm_ple, _resident((None, 1, D_MODEL), lambda b, l: (i, 0, 0))),
        _weight_cols(w_gate),
        _weight_cols(w_proj),
        (tri, _resident(tri.shape, const2)),
        (lvl, _resident(lvl.shape, const2)),
        (sq, s_k), (sf, s_k), (sk, s_k), (sv, s_v), (s0, s_st),
        (jnp.zeros((HEADS, DK), jnp.int32), _resident((HEADS, DK), const2)),
        (nxt_ops, nxt_in),
    ])
    aliases = {}
    for prev, out_idx in ((s_prev, 2), (fin_prev, 1)):
        if prev is not None:
            operands.append(prev)
            in_specs.append(pl.BlockSpec(memory_space=pl.ANY))
            aliases[len(operands) - 1] = out_idx
    out_specs = [
        pl.BlockSpec((None, TL, D_MODEL), lambda b, l: (b, l, 0)),
        pl.BlockSpec((None, None, HEADS, DK, DV), lambda b, l: (j, b, 0, 0, 0)),
        s_st,
        s_v,
    ] + nxt_out
    out_shape = [
        jax.ShapeDtypeStruct((bsz, seq, D_MODEL), F32),
        jax.ShapeDtypeStruct((n_a, bsz, HEADS, DK, DV), F32),
        jax.ShapeDtypeStruct(s0.shape, F32),
        jax.ShapeDtypeStruct((n_s, HEADS, DV), F32),
    ] + nxt_shapes
    return pl.pallas_call(
        _regroup(functools.partial(_hgrn_prompt_kernel, layer_j=j), sizes),
        grid=grid, in_specs=in_specs, out_specs=out_specs, out_shape=out_shape,
        scratch_shapes=[pltpu.VMEM((HEADS, DV, DK), F32)],
        input_output_aliases=aliases,
        compiler_params=_params("arbitrary", "arbitrary"),
        name=f"hgrn_prompt_{j}",
    )(*operands)


def _gmlp_prompt_layer(x, p_all, i, j, norm_mix, w_in, ln_g, ln_b, w_sp, bias_full, w_out,
                       norm_ple, w_gate, w_proj, norm_final, final_norm, n_b, vt_prev, nxt, sample):
    xs, ps_all, w00, b0 = sample
    n_s = xs.shape[0]
    bsz, seq, _ = x.shape
    n_l = seq // TLB
    grid = (bsz, n_l)
    nxt_ops, nxt_in, nxt_out, nxt_shapes = _next_weight_specs(
        nxt, bsz * n_l, lambda b, l: b * n_l + l)
    operands, in_specs, sizes = _expand([
        (x, pl.BlockSpec((None, TLB, D_MODEL), lambda b, l: (b, l, 0))),
        (p_all, pl.BlockSpec((None, None, TLB, PLE_DIM), lambda b, l: (i, b, l, 0))),
        (norm_mix, _resident((None, 1, D_MODEL), lambda b, l: (i, 0, 0))),
        _weight_cols(w_in),
        (ln_g, _resident((None, 1, D_INNER), lambda b, l: (j, 0, 0))),
        (ln_b, _resident((None, 1, D_INNER), lambda b, l: (j, 0, 0))),
        (w_sp, _resident((None, GROUPS, B_CHUNK, B_CHUNK), lambda b, l: (j, 0, 0, 0))),
        (bias_full, _resident((None, B_CHUNK, D_INNER), lambda b, l: (j, 0, 0))),
        _weight_cols(w_out),
        (norm_ple, _resident((None, 1, D_MODEL), lambda b, l: (i, 0, 0))),
        _weight_cols(w_gate),
        _weight_cols(w_proj),
        (norm_final, _resident((1, D_MODEL), lambda b, l: (0, 0))),
        ([xs, ps_all, w00, b0],
         [_resident((n_s, D_MODEL), lambda b, l: (0, 0)),
          _resident((None, n_s, PLE_DIM), lambda b, l: (i, 0, 0)),
          _resident((None, 1, D_INNER), lambda b, l: (j, 0, 0)),
          _resident((None, 1, D_INNER), lambda b, l: (j, 0, 0))]),
        (nxt_ops, nxt_in),
    ])
    aliases = {}
    if vt_prev is not None:
        operands.append(vt_prev)
        in_specs.append(pl.BlockSpec(memory_space=pl.ANY))
        aliases[len(operands) - 1] = 1
    out_specs = [
        pl.BlockSpec((None, TLB, D_MODEL), lambda b, l: (b, l, 0)),
        pl.BlockSpec((None, None, B_CHUNK, D_INNER), lambda b, l: (j, b, 0, 0)),
        pl.BlockSpec((n_s, D_MODEL), lambda b, l: (0, 0)),
        pl.BlockSpec((n_s, D_INNER), lambda b, l: (0, 0)),
    ] + nxt_out
    out_shape = [
        jax.ShapeDtypeStruct((bsz, seq, D_MODEL), F32),
        jax.ShapeDtypeStruct((n_b, bsz, B_CHUNK, D_INNER), F32),
        jax.ShapeDtypeStruct((n_s, D_MODEL), F32),
        jax.ShapeDtypeStruct((n_s, D_INNER), F32),
    ] + nxt_shapes
    return pl.pallas_call(
        _regroup(functools.partial(_gmlp_prompt_kernel, final_norm=final_norm), sizes),
        grid=grid, in_specs=in_specs, out_specs=out_specs, out_shape=out_shape,
        input_output_aliases=aliases,
        compiler_params=_params("arbitrary", "arbitrary"),
        name=f"gmlp_prompt_{j}",
    )(*operands)


def _whole(shape):
    nd = len(shape)
    return pl.BlockSpec(shape, lambda *_: (0,) * nd)


def _hgrn_sample_front(x, i, j, lb_logits, norm_mix, w_in):
    n = x.shape[0]
    n_a = lb_logits.shape[0]
    operands, in_specs, sizes = _expand([
        (x, _whole((n, D_MODEL))),
        (lb_logits, _whole((n_a, D_MODEL))),
        (norm_mix, pl.BlockSpec((None, 1, D_MODEL), lambda g: (i, 0, 0))),
        _weight_cols(w_in),
    ])
    q, f, kk, v, z = pl.pallas_call(
        _regroup(functools.partial(_hgrn_sample_front_kernel, layer_j=j), sizes),
        grid=(1,),
        in_specs=in_specs,
        out_specs=[_whole((n, D_MODEL))] * 3 + [_whole((n, D_INNER))] * 2,
        out_shape=[jax.ShapeDtypeStruct((n, D_MODEL), F32)] * 3
        + [jax.ShapeDtypeStruct((n, D_INNER), F32)] * 2,
        compiler_params=_params("arbitrary"),
        name=f"hgrn_sample_front_{j}",
    )(*operands)

    qfkv = (q.reshape(n, HEADS, DK), f.reshape(n, HEADS, DK), kk.reshape(n, HEADS, DK),
            v.reshape(n, HEADS, DV))
    return qfkv, z


def _hgrn_sample_back(x, o, z, p_all, i, j, gnorm, w_out, norm_ple, w_gate, w_proj):
    n = x.shape[0]
    operands, in_specs, sizes = _expand([
        (x, _whole((n, D_MODEL))),
        (o.reshape(n, D_INNER), _whole((n, D_INNER))),
        (z, _whole((n, D_INNER))),
        (p_all, pl.BlockSpec((None, n, PLE_DIM), lambda g: (i, 0, 0))),
        (gnorm, pl.BlockSpec((None, 1, DV), lambda g: (j, 0, 0))),
        _weight_cols(w_out),
        (norm_ple, pl.BlockSpec((None, 1, D_MODEL), lambda g: (i, 0, 0))),
        _weight_cols(w_gate),
        _weight_cols(w_proj),
    ])
    return pl.pallas_call(
        _regroup(_hgrn_sample_back_kernel, sizes),
        grid=(1,),
        in_specs=in_specs,
        out_specs=_whole((n, D_MODEL)),
        out_shape=jax.ShapeDtypeStruct((n, D_MODEL), F32),
        compiler_params=_params("arbitrary"),
        name=f"hgrn_sample_back_{j}",
    )(*operands)


def kernel(x_prompt, x_sample, state_hgrn, p_prompt, p_sample, norm_mix, w_in_a, lb_logits,
           gnorm_a, w_out_a, w_in_b, ln_v_g, ln_v_b, w_spatial, b_spatial, w_out_b,
           norm_ple, w_ple_gate, w_ple_proj, norm_final):
    depth = norm_mix.shape[0]
    n_s = x_sample.shape[0]
    assert x_prompt.shape[1] % TL == 0 and x_prompt.shape[1] % TLB == 0
    assert TLB % B_CHUNK == 0 and x_sample.shape[1] == 1

    def layer_weights(i):
        mixer = (w_in_a, w_out_a) if i % 2 == 0 else (w_in_b, w_out_b)
        return ((mixer[0], i // 2), (mixer[1], i // 2), (w_ple_gate, i), (w_ple_proj, i))

    wb = tuple(stack[idx].astype(BF16) for stack, idx in layer_weights(0))

    tri = jnp.asarray(_TRI_NP, dtype=BF16)
    lvl = jnp.asarray(_LVL_NP)
    nf = norm_final.reshape(1, D_MODEL)
    bias_full = jnp.repeat(jnp.swapaxes(b_spatial, 1, 2), DG, axis=2)
    w00 = jnp.repeat(w_spatial[:, :, 0, 0], DG, axis=1)[:, None, :]
    b0 = jnp.repeat(b_spatial[:, :, 0], DG, axis=1)[:, None, :]
    nmix = norm_mix[:, None, :]
    nple = norm_ple[:, None, :]
    gn = gnorm_a[:, None, :]
    lng = ln_v_g[:, None, :]
    lnb = ln_v_b[:, None, :]

    if depth % 2 == 1:
        raise NotImplementedError("final norm is fused into the last (chunk-MLP) layer")
    hp = x_prompt
    hs = x_sample.reshape(n_s, D_MODEL)
    ps = p_sample.reshape(depth, n_s, PLE_DIM)
    vt_s = []
    st_p = st_s = vt_p = None
    for i in range(depth):
        j = i // 2
        last = i == depth - 1
        nxt = None if last else layer_weights(i + 1)
        w_in, w_out, w_gate, w_proj = wb
        if i % 2 == 0:
            qfkv, z_s = _hgrn_sample_front(hs, i, j, lb_logits, nmix, w_in)
            hp, st_p, st_s, o_s, *wb = _hgrn_prompt_layer(
                hp, p_prompt, i, j, lb_logits, nmix, w_in, gn, w_out, nple, w_gate, w_proj,
                tri, lvl, qfkv, state_hgrn, st_s, st_p, nxt)
            hs = _hgrn_sample_back(hs, o_s, z_s, ps, i, j, gn, w_out, nple, w_gate, w_proj)
        else:
            hp, vt_p, hs, vts, *wb = _gmlp_prompt_layer(
                hp, p_prompt, i, j, nmix, w_in, lng, lnb, w_spatial, bias_full, w_out, nple,
                w_gate, w_proj, nf, last, w_in_b.shape[0], vt_p, nxt, (hs, ps, w00, b0))
            vt_s.append(vts.reshape(n_s, 1, D_INNER))
    return (hp, hs.reshape(n_s, 1, D_MODEL), st_p, st_s, vt_p, jnp.stack(vt_s))
```

```python
import functools
import math

import numpy as np
import jax
import jax.numpy as jnp
from jax import lax
from jax.experimental import pallas as pl
from jax.experimental.pallas import tpu as pltpu

D_MODEL = 1024
D_INNER = 2048
HEADS = 8
DK = 128
DV = 256
GROUPS = 8
DG = 256
B_CHUNK = 128
PLE_DIM = 256
EPS = 1e-6
LOG2_E = 1.4426950408889634

TL = 256
TLB = 512
HALF = TL // 2
N_LEVELS = int(math.log2(TL))
WEIGHT_CHUNK_3 = 1536
WEIGHT_CHUNK_1 = 512
VMEM_LIMIT = 58 * 1024 * 1024

F32 = jnp.float32
BF16 = jnp.bfloat16


def _build_level_consts(tl):
    t = np.arange(tl)[:, None]
    s = np.arange(tl)[None, :]
    tri = (s <= t).astype(np.float32)
    x = np.maximum(t ^ s, 1)
    p = np.floor(np.log2(x)).astype(np.int32)
    nlev = int(math.log2(tl))
    lvl = np.where(t == s, 0, np.where(t > s, nlev - p, -1)).astype(np.int32)
    return tri, lvl[:tl // 2, :tl // 2]


_TRI_NP, _LVL_NP = _build_level_consts(TL)


def _dot(a, b):
    return jnp.dot(a, b, preferred_element_type=F32)


def _dot_cols(a, w_refs):
    return jnp.concatenate([_dot(a, w[...]) for w in w_refs], axis=1)


def _dot_nt(a, b):
    return lax.dot_general(a, b, (((1,), (1,)), ((), ())), preferred_element_type=F32)


def _dot_tn(a, b):
    return lax.dot_general(a, b, (((0,), (0,)), ((), ())), preferred_element_type=F32)


def _rmsnorm(x, g):
    ms = jnp.mean(x * x, axis=-1, keepdims=True)
    return x * lax.rsqrt(ms + EPS) * g


def _sigmoid(x):
    return 0.5 * jnp.tanh(0.5 * x) + 0.5


def _silu(x):
    hx = 0.5 * x
    return hx * jnp.tanh(hx) + hx


def _gelu_tanh(x):
    c = math.sqrt(2.0 / math.pi)
    return 0.5 * x * (1.0 + jnp.tanh(c * (x + 0.044715 * (x * x * x))))


def _forget_lower_bound(lbl, j):
    mx = jnp.max(lbl, axis=0, keepdims=True)
    e = jnp.exp(lbl - mx)
    sm = e / jnp.sum(e, axis=0, keepdims=True)
    cum0 = sm[0:1, :]
    cum = cum0
    for i in range(1, j + 1):
        cum = cum + sm[i:i + 1, :]
    return cum - cum0


def _forget_gates(fpre, lb):
    e = jnp.exp(-jnp.abs(fpre))
    log_sig = jnp.minimum(fpre, 0.0) - jnp.log1p(e)
    a = jnp.log(lb)
    y = jnp.log1p(-lb) + log_sig
    logf = jnp.maximum(a, y) + jnp.log1p(jnp.exp(-jnp.abs(a - y)))
    one_minus_f = (1.0 - lb) * (jnp.where(fpre >= 0.0, e, 1.0) / (1.0 + e))
    return logf, one_minus_f


def _hgrn_front(x, nm, w_in, lb):
    hn = _rmsnorm(x, nm).astype(BF16)
    proj = _dot_cols(hn, w_in)
    q = _silu(proj[:, :D_MODEL])
    logf, kk = _forget_gates(proj[:, D_MODEL:2 * D_MODEL], lb)
    v = proj[:, 2 * D_MODEL:2 * D_MODEL + D_INNER]
    z = proj[:, 2 * D_MODEL + D_INNER:]
    return q, logf, kk, v, z


def _head_rmsnorm(o, gn):
    outs = []
    for h in range(HEADS):
        outs.append(_rmsnorm(o[:, h * DV:(h + 1) * DV], gn))
    return jnp.concatenate(outs, axis=1)


def _ple(h, p, npl, w_gate, w_proj):
    gate = _sigmoid(_dot_cols(_rmsnorm(h, npl).astype(BF16), w_gate))
    return h + gate * _dot_cols(p.astype(BF16), w_proj)


def _mixer_back(x, branch, w_out, p, npl, w_gate, w_proj):
    h = x + _dot_cols(branch.astype(BF16), w_out)
    return _ple(h, p, npl, w_gate, w_proj)


def _gmlp_front(x, nm, w_in, ln_g, ln_b):
    hn = _rmsnorm(x, nm).astype(BF16)
    proj = _dot_cols(hn, w_in)
    u = _gelu_tanh(proj[:, :D_INNER])
    vg = _gelu_tanh(proj[:, D_INNER:2 * D_INNER])
    z = proj[:, 2 * D_INNER:]
    mu = jnp.mean(vg, axis=-1, keepdims=True)
    vc = vg - mu
    vn = vc * lax.rsqrt(jnp.mean(vc * vc, axis=-1, keepdims=True) + EPS) * ln_g + ln_b
    return u, vn, z


def _hgrn_prompt_kernel(x_ref, p_ref, lbl_ref, nm_ref, win_ref, gn_ref, wout_ref, npl_ref,
                        wg_ref, wp_ref, tri_ref, lvl_ref, sq_ref, sf_ref, sk_ref, sv_ref, ss_ref,
                        zero_ref, nxt_refs, *rest, layer_j):
    n_out = 4 + len(nxt_refs)
    y_ref, sfin_ref, ssnew_ref, so_ref = rest[len(rest) - n_out - 1:len(rest) - n_out + 3]
    st_ref = rest[-1]
    l = pl.program_id(1)
    if nxt_refs:
        _cast_next_weights(pl.program_id(0) * pl.num_programs(1) + l, nxt_refs,
                           rest[len(rest) - n_out + 3:len(rest) - 1])

    @pl.when(l == 0)
    def _():
        st_ref[...] = jnp.zeros_like(st_ref)

    _sample_state_update(sq_ref, sf_ref, sk_ref, sv_ref, ss_ref, ssnew_ref, so_ref)

    x = x_ref[...]
    lb = _forget_lower_bound(lbl_ref[...], layer_j)
    q, logf, kk, v, z = _hgrn_front(x, nm_ref[...], win_ref, lb)
    done = lax.bitcast_convert_type(so_ref[0, :, :DK], jnp.int32) & zero_ref[...]
    q_patch = q[:8, :DK] + lax.bitcast_convert_type(done, F32)
    q = jnp.concatenate([jnp.concatenate([q_patch, q[:8, DK:]], axis=1), q[8:]], axis=0)

    lg = logf * LOG2_E
    g_hi = lg.astype(BF16)
    g_lo = (lg - g_hi.astype(F32)).astype(BF16)
    g2 = jnp.concatenate([g_hi, g_lo], axis=1)
    b2 = _dot(tri_ref[...], g2)
    b = b2[:, :D_MODEL] + b2[:, D_MODEL:]
    b_last = b[TL - 1:TL, :]
    qs = (q * jnp.exp2(b)).astype(BF16)
    kd = (kk * jnp.exp2(b_last - b)).astype(BF16)
    dec = jnp.exp2(b_last)
    qb = q.astype(BF16)
    kb = kk.astype(BF16)
    vb = v.astype(BF16)

    zk, zq = {}, {}
    for level in range(1, N_LEVELS + 1):
        hs = TL >> level
        if hs >= 8:
            grp = 2 * hs
            shp = (TL // grp, grp, D_MODEL)
            b3, q3, k3 = b.reshape(shp), q.reshape(shp), kk.reshape(shp)
            mid = b3[:, hs - 1:hs, :]
            zq3 = q3[:, hs:, :] * jnp.exp2(b3[:, hs:, :] - mid)
            zk3 = k3[:, :hs, :] * jnp.exp2(mid - b3[:, :hs, :])
            zq[level] = zq3.reshape(TL // 2, D_MODEL).astype(BF16)
            zk[level] = jnp.concatenate([zk3, zq3], axis=1).reshape(TL, D_MODEL).astype(BF16)
        else:
            shp = (TL // 8, 8, D_MODEL)
            b8, q8, k8 = b.reshape(shp), q.reshape(shp), kk.reshape(shp)
            sub = lax.broadcasted_iota(jnp.int32, shp, 1)
            is_q = (sub & hs) != 0
            if hs == 4:
                mid = jnp.broadcast_to(b8[:, 3:4, :], shp)
            elif hs == 2:
                mid = jnp.where(sub < 4, jnp.broadcast_to(b8[:, 1:2, :], shp),
                                jnp.broadcast_to(b8[:, 5:6, :], shp))
            else:
                mid = None
            if mid is None:
                e = jnp.where(is_q, lg.reshape(shp), 0.0)
            else:
                e = jnp.where(is_q, b8 - mid, mid - b8)
            zfine = jnp.where(is_q, q8, k8) * jnp.exp2(e)
            zk[level] = zfine.reshape(TL, D_MODEL).astype(BF16)

    lvl = lvl_ref[...]
    atts = []
    for h in range(HEADS):
        sl = slice(h * DK, (h + 1) * DK)
        pd = _dot_nt(qb[:, sl], kb[:, sl])
        diag = [jnp.where(lvl == 0, pd[:HALF, :HALF], 0.0),
                jnp.where(lvl == 0, pd[HALF:, HALF:], 0.0)]
        for level in range(N_LEVELS, 1, -1):
            hs = TL >> level
            zkl = zk[level][:, sl]
            if hs >= 8:
                grp = 2 * hs
                nq = HALF // 2
                pr = _dot_nt(zq[level][:, sl], zkl)
                hit = lvl.reshape(HALF // grp, grp, HALF)[:, hs:, :] == level
                for i in range(2):
                    pq = pr[i * nq:(i + 1) * nq, i * HALF:(i + 1) * HALF]
                    d3 = diag[i].reshape(HALF // grp, grp, HALF)
                    upper = jnp.where(hit, pq.reshape(HALF // grp, hs, HALF), d3[:, hs:, :])
                    diag[i] = jnp.concatenate([d3[:, :hs, :], upper], axis=1).reshape(HALF, HALF)
            else:
                pf = _dot_nt(zkl, zkl)
                hit = lvl == level
                diag[0] = jnp.where(hit, pf[:HALF, :HALF], diag[0])
                diag[1] = jnp.where(hit, pf[HALF:, HALF:], diag[1])
        a10 = _dot_nt(zq[1][:, sl], zk[1][:HALF, sl])
        att = jnp.concatenate(
            [jnp.concatenate([diag[0], jnp.zeros((HALF, HALF), F32)], axis=1),
             jnp.concatenate([a10, diag[1]], axis=1)], axis=0)
        atts.append(att.astype(BF16))
    outs = []
    for h in range(HEADS):
        sl = slice(h * DK, (h + 1) * DK)
        vsl = slice(h * DV, (h + 1) * DV)
        st = st_ref[h]
        o = _dot_nt(qs[:, sl], st.astype(BF16)) + _dot(atts[h], vb[:, vsl])
        st_ref[h] = st * dec[:, sl] + _dot_tn(vb[:, vsl], kd[:, sl])
        outs.append(o)
    o = jnp.concatenate(outs, axis=1)

    branch = _head_rmsnorm(o, gn_ref[...]) * _silu(z)
    y_ref[...] = _mixer_back(x, branch, wout_ref, p_ref[...], npl_ref[...],
                             wg_ref, wp_ref)

    @pl.when(l == pl.num_programs(1) - 1)
    def _():
        for h in range(HEADS):
            sfin_ref[h] = st_ref[h].T


def _gmlp_prompt_kernel(x_ref, p_ref, nm_ref, win_ref, lng_ref, lnb_ref, wsp_ref, bsp_ref,
                        wout_ref, npl_ref, wg_ref, wp_ref, nf_ref, samp_refs, nxt_refs, *rest,
                        final_norm):
    n_out = 4 + len(nxt_refs)
    y_ref, vt_ref, ys_ref, vts_ref = rest[len(rest) - n_out:len(rest) - n_out + 4]
    l = pl.program_id(1)
    step = pl.program_id(0) * pl.num_programs(1) + l
    if nxt_refs:
        _cast_next_weights(step, nxt_refs, rest[len(rest) - n_out + 4:])

    @pl.when(step == 0)
    def _():
        xs_ref, ps_ref, w00_ref, b0_ref = samp_refs
        _gmlp_sample_body(xs_ref, ps_ref, nm_ref, win_ref, lng_ref, lnb_ref, w00_ref, b0_ref,
                          wout_ref, npl_ref, wg_ref, wp_ref, nf_ref, ys_ref, vts_ref,
                          final_norm=final_norm)

    x = x_ref[...]
    u, vn, z = _gmlp_front(x, nm_ref[...], win_ref, lng_ref[...], lnb_ref[...])
    vb = vn.astype(BF16)
    r = lax.broadcasted_iota(jnp.int32, (B_CHUNK, B_CHUNK), 0)
    c = lax.broadcasted_iota(jnp.int32, (B_CHUNK, B_CHUNK), 1)
    tril = c <= r
    bias = bsp_ref[...]
    cols = []
    for g in range(GROUPS):
        w = jnp.where(tril, wsp_ref[g], 0.0).astype(BF16)
        gsl = slice(g * DG, (g + 1) * DG)
        chunks = []
        for ci in range(TLB // B_CHUNK):
            rsl = slice(ci * B_CHUNK, (ci + 1) * B_CHUNK)
            chunks.append(_dot(w, vb[rsl, gsl]) + bias[:, gsl])
        cols.append(jnp.concatenate(chunks, axis=0))
    s = jnp.concatenate(cols, axis=1)
    branch = u * s * _silu(z)
    h = _mixer_back(x, branch, wout_ref, p_ref[...], npl_ref[...], wg_ref, wp_ref)
    if final_norm:
        h = _rmsnorm(h, nf_ref[...])
    y_ref[...] = h

    @pl.when(l == pl.num_programs(1) - 1)
    def _():
        vt_ref[...] = vn[TLB - B_CHUNK:, :]


def _hgrn_sample_front_kernel(x_ref, lbl_ref, nm_ref, win_ref, q_ref, f_ref, k_ref, v_ref, z_ref,
                              *, layer_j):
    lb = _forget_lower_bound(lbl_ref[...], layer_j)
    q, logf, kk, v, z = _hgrn_front(x_ref[...], nm_ref[...], win_ref, lb)
    q_ref[...] = q
    f_ref[...] = jnp.exp(logf)
    k_ref[...] = kk
    v_ref[...] = v
    z_ref[...] = z


def _sample_state_update(q_ref, f_ref, k_ref, v_ref, s_ref, snew_ref, o_ref):
    for i in range(q_ref.shape[0]):
        q_t = q_ref[i].T
        f_t = f_ref[i].T
        k_t = k_ref[i].T
        for h in range(HEADS):
            s_new = f_t[:, h:h + 1] * s_ref[i, h] + k_t[:, h:h + 1] * v_ref[i, h:h + 1, :]
            snew_ref[i, h] = s_new
            o_ref[i, h:h + 1, :] = jnp.sum(q_t[:, h:h + 1] * s_new, axis=0, keepdims=True)


def _cast_next_weights(step, nxt_refs, out_refs):
    for src, dst in zip(nxt_refs[:3], out_refs[:3]):
        dst[...] = src[...].astype(BF16)

    @pl.when(step == 0)
    def _():
        out_refs[3][...] = nxt_refs[3][...].astype(BF16)


def _hgrn_sample_back_kernel(x_ref, o_ref, z_ref, p_ref, gn_ref, wout_ref, npl_ref, wg_ref,
                             wp_ref, y_ref):
    branch = _head_rmsnorm(o_ref[...], gn_ref[...]) * _silu(z_ref[...])
    y_ref[...] = _mixer_back(x_ref[...], branch, wout_ref, p_ref[...], npl_ref[...],
                             wg_ref, wp_ref)


def _gmlp_sample_body(x_ref, p_ref, nm_ref, win_ref, lng_ref, lnb_ref, w00_ref, b0_ref,
                      wout_ref, npl_ref, wg_ref, wp_ref, nf_ref, y_ref, vt_ref, *, final_norm):
    x = x_ref[...]
    u, vn, z = _gmlp_front(x, nm_ref[...], win_ref, lng_ref[...], lnb_ref[...])
    s = w00_ref[...] * vn + b0_ref[...]
    branch = u * s * _silu(z)
    h = _mixer_back(x, branch, wout_ref, p_ref[...], npl_ref[...], wg_ref, wp_ref)
    if final_norm:
        h = _rmsnorm(h, nf_ref[...])
    y_ref[...] = h
    vt_ref[...] = vn


def _resident(shape, index):
    return pl.BlockSpec(shape, index, pipeline_mode=pl.Buffered(1))


def _chunk_width(n):
    width = WEIGHT_CHUNK_3 if n % WEIGHT_CHUNK_3 == 0 else WEIGHT_CHUNK_1
    assert n % width == 0 and width % 1024 != 0
    return width


def _weight_cols(w):
    k, n = w.shape
    width = _chunk_width(n)
    specs = [_resident((k, width), lambda *_, c=c: (0, c)) for c in range(n // width)]
    return [w] * len(specs), specs


def _next_weight_specs(nxt, n_steps, step_of):
    ops, ins, outs, shapes = [], [], [], []
    for idx, (stack, layer) in enumerate(nxt or ()):
        _, k, n = stack.shape
        ops.append(stack)
        shapes.append(jax.ShapeDtypeStruct((k, n), BF16))
        if idx < 3:
            rows = k // n_steps
            assert rows * n_steps == k and rows % 16 == 0
            ins.append(pl.BlockSpec((None, rows, n),
                                    lambda *g, layer=layer: (layer, step_of(*g), 0)))
            outs.append(pl.BlockSpec((rows, n), lambda *g: (step_of(*g), 0)))
        else:
            ins.append(_resident((None, k, n), lambda *g, layer=layer: (layer, 0, 0)))
            outs.append(pl.BlockSpec((k, n), lambda *g: (0, 0)))
    return ops, ins, outs, shapes


def _expand(args):
    ops, specs, sizes = [], [], []
    for o, s in args:
        if isinstance(o, list):
            ops += o
            specs += s
            sizes.append(len(o))
        else:
            ops.append(o)
            specs.append(s)
            sizes.append(None)
    return ops, specs, sizes


def _regroup(fn, sizes):
    def wrapped(*refs):
        pos, grouped = 0, []
        for n in sizes:
            if n is None:
                grouped.append(refs[pos])
                pos += 1
            else:
                grouped.append(tuple(refs[pos:pos + n]))
                pos += n
        return fn(*grouped, *refs[pos:])
    return wrapped


def _params(*sem):
    return pltpu.CompilerParams(dimension_semantics=sem, vmem_limit_bytes=VMEM_LIMIT)


def _hgrn_prompt_layer(x, p_all, i, j, lb_logits, norm_mix, w_in, gnorm, w_out, norm_ple,
                       w_gate, w_proj, tri, lvl, sample_qfkv, s0, s_prev, fin_prev, nxt):
    bsz, seq, _ = x.shape
    n_a = lb_logits.shape[0]
    n_l = seq // TL
    grid = (bsz, n_l)
    n_s = s0.shape[1]
    assert n_s % (bsz * n_l) == 0
    spt = n_s // (bsz * n_l)
    const2 = lambda b, l: (0, 0)
    nxt_ops, nxt_in, nxt_out, nxt_shapes = _next_weight_specs(
        nxt, bsz * n_l, lambda b, l: b * n_l + l)
    s_k = pl.BlockSpec((spt, HEADS, DK), lambda b, l: (b * n_l + l, 0, 0))
    s_v = pl.BlockSpec((spt, HEADS, DV), lambda b, l: (b * n_l + l, 0, 0))
    s_st = pl.BlockSpec((None, spt, HEADS, DK, DV), lambda b, l: (j, b * n_l + l, 0, 0, 0))
    sq, sf, sk, sv = sample_qfkv
    operands, in_specs, sizes = _expand([
        (x, pl.BlockSpec((None, TL, D_MODEL), lambda b, l: (b, l, 0))),
        (p_all, pl.BlockSpec((None, None, TL, PLE_DIM), lambda b, l: (i, b, l, 0))),
        (lb_logits, _resident((n_a, D_MODEL), const2)),
        (norm_mix, _resident((None, 1, D_MODEL), lambda b, l: (i, 0, 0))),
        _weight_cols(w_in),
        (gnorm, _resident((None, 1, DV), lambda b, l: (j, 0, 0))),
        _weight_cols(w_out),
        (norm_ple, _resident((None, 1, D_MODEL), lambda b, l: (i, 0, 0))),
        _weight_cols(w_gate),
        _weight_cols(w_proj),
        (tri, _resident(tri.shape, const2)),
        (lvl, _resident(lvl.shape, const2)),
        (sq, s_k), (sf, s_k), (sk, s_k), (sv, s_v), (s0, s_st),
        (jnp.zeros((HEADS, DK), jnp.int32), _resident((HEADS, DK), const2)),
        (nxt_ops, nxt_in),
    ])
    aliases = {}
    for prev, out_idx in ((s_prev, 2), (fin_prev, 1)):
        if prev is not None:
            operands.append(prev)
            in_specs.append(pl.BlockSpec(memory_space=pl.ANY))
            aliases[len(operands) - 1] = out_idx
    out_specs = [
        pl.BlockSpec((None, TL, D_MODEL), lambda b, l: (b, l, 0)),
        pl.BlockSpec((None, None, HEADS, DK, DV), lambda b, l: (j, b, 0, 0, 0)),
        s_st,
        s_v,
    ] + nxt_out
    out_shape = [
        jax.ShapeDtypeStruct((bsz, seq, D_MODEL), F32),
        jax.ShapeDtypeStruct((n_a, bsz, HEADS, DK, DV), F32),
        jax.ShapeDtypeStruct(s0.shape, F32),
        jax.ShapeDtypeStruct((n_s, HEADS, DV), F32),
    ] + nxt_shapes
    return pl.pallas_call(
        _regroup(functools.partial(_hgrn_prompt_kernel, layer_j=j), sizes),
        grid=grid, in_specs=in_specs, out_specs=out_specs, out_shape=out_shape,
        scratch_shapes=[pltpu.VMEM((HEADS, DV, DK), F32)],
        input_output_aliases=aliases,
        compiler_params=_params("arbitrary", "arbitrary"),
        name=f"hgrn_prompt_{j}",
    )(*operands)


def _gmlp_prompt_layer(x, p_all, i, j, norm_mix, w_in, ln_g, ln_b, w_sp, bias_full, w_out,
                       norm_ple, w_gate, w_proj, norm_final, final_norm, n_b, vt_prev, nxt, sample):
    xs, ps_all, w00, b0 = sample
    n_s = xs.shape[0]
    bsz, seq, _ = x.shape
    n_l = seq // TLB
    grid = (bsz, n_l)
    nxt_ops, nxt_in, nxt_out, nxt_shapes = _next_weight_specs(
        nxt, bsz * n_l, lambda b, l: b * n_l + l)
    operands, in_specs, sizes = _expand([
        (x, pl.BlockSpec((None, TLB, D_MODEL), lambda b, l: (b, l, 0))),
        (p_all, pl.BlockSpec((None, None, TLB, PLE_DIM), lambda b, l: (i, b, l, 0))),
        (norm_mix, _resident((None, 1, D_MODEL), lambda b, l: (i, 0, 0))),
        _weight_cols(w_in),
        (ln_g, _resident((None, 1, D_INNER), lambda b, l: (j, 0, 0))),
        (ln_b, _resident((None, 1, D_INNER), lambda b, l: (j, 0, 0))),
        (w_sp, _resident((None, GROUPS, B_CHUNK, B_CHUNK), lambda b, l: (j, 0, 0, 0))),
        (bias_full, _resident((None, B_CHUNK, D_INNER), lambda b, l: (j, 0, 0))),
        _weight_cols(w_out),
        (norm_ple, _resident((None, 1, D_MODEL), lambda b, l: (i, 0, 0))),
        _weight_cols(w_gate),
        _weight_cols(w_proj),
        (norm_final, _resident((1, D_MODEL), lambda b, l: (0, 0))),
        ([xs, ps_all, w00, b0],
         [_resident((n_s, D_MODEL), lambda b, l: (0, 0)),
          _resident((None, n_s, PLE_DIM), lambda b, l: (i, 0, 0)),
          _resident((None, 1, D_INNER), lambda b, l: (j, 0, 0)),
          _resident((None, 1, D_INNER), lambda b, l: (j, 0, 0))]),
        (nxt_ops, nxt_in),
    ])
    aliases = {}
    if vt_prev is not None:
        operands.append(vt_prev)
        in_specs.append(pl.BlockSpec(memory_space=pl.ANY))
        aliases[len(operands) - 1] = 1
    out_specs = [
        pl.BlockSpec((None, TLB, D_MODEL), lambda b, l: (b, l, 0)),
        pl.BlockSpec((None, None, B_CHUNK, D_INNER), lambda b, l: (j, b, 0, 0)),
        pl.BlockSpec((n_s, D_MODEL), lambda b, l: (0, 0)),
        pl.BlockSpec((n_s, D_INNER), lambda b, l: (0, 0)),
    ] + nxt_out
    out_shape = [
        jax.ShapeDtypeStruct((bsz, seq, D_MODEL), F32),
        jax.ShapeDtypeStruct((n_b, bsz, B_CHUNK, D_INNER), F32),
        jax.ShapeDtypeStruct((n_s, D_MODEL), F32),
        jax.ShapeDtypeStruct((n_s, D_INNER), F32),
    ] + nxt_shapes
    return pl.pallas_call(
        _regroup(functools.partial(_gmlp_prompt_kernel, final_norm=final_norm), sizes),
        grid=grid, in_specs=in_specs, out_specs=out_specs, out_shape=out_shape,
        input_output_aliases=aliases,
        compiler_params=_params("arbitrary", "arbitrary"),
        name=f"gmlp_prompt_{j}",
    )(*operands)


def _whole(shape):
    nd = len(shape)
    return pl.BlockSpec(shape, lambda *_: (0,) * nd)


def _hgrn_sample_front(x, i, j, lb_logits, norm_mix, w_in):
    n = x.shape[0]
    n_a = lb_logits.shape[0]
    operands, in_specs, sizes = _expand([
        (x, _whole((n, D_MODEL))),
        (lb_logits, _whole((n_a, D_MODEL))),
        (norm_mix, pl.BlockSpec((None, 1, D_MODEL), lambda g: (i, 0, 0))),
        _weight_cols(w_in),
    ])
    q, f, kk, v, z = pl.pallas_call(
        _regroup(functools.partial(_hgrn_sample_front_kernel, layer_j=j), sizes),
        grid=(1,),
        in_specs=in_specs,
        out_specs=[_whole((n, D_MODEL))] * 3 + [_whole((n, D_INNER))] * 2,
        out_shape=[jax.ShapeDtypeStruct((n, D_MODEL), F32)] * 3
        + [jax.ShapeDtypeStruct((n, D_INNER), F32)] * 2,
        compiler_params=_params("arbitrary"),
        name=f"hgrn_sample_front_{j}",
    )(*operands)

    qfkv = (q.reshape(n, HEADS, DK), f.reshape(n, HEADS, DK), kk.reshape(n, HEADS, DK),
            v.reshape(n, HEADS, DV))
    return qfkv, z


def _hgrn_sample_back(x, o, z, p_all, i, j, gnorm, w_out, norm_ple, w_gate, w_proj):
    n = x.shape[0]
    operands, in_specs, sizes = _expand([
        (x, _whole((n, D_MODEL))),
        (o.reshape(n, D_INNER), _whole((n, D_INNER))),
        (z, _whole((n, D_INNER))),
        (p_all, pl.BlockSpec((None, n, PLE_DIM), lambda g: (i, 0, 0))),
        (gnorm, pl.BlockSpec((None, 1, DV), lambda g: (j, 0, 0))),
        _weight_cols(w_out),
        (norm_ple, pl.BlockSpec((None, 1, D_MODEL), lambda g: (i, 0, 0))),
        _weight_cols(w_gate),
        _weight_cols(w_proj),
    ])
    return pl.pallas_call(
        _regroup(_hgrn_sample_back_kernel, sizes),
        grid=(1,),
        in_specs=in_specs,
        out_specs=_whole((n, D_MODEL)),
        out_shape=jax.ShapeDtypeStruct((n, D_MODEL), F32),
        compiler_params=_params("arbitrary"),
        name=f"hgrn_sample_back_{j}",
    )(*operands)


def kernel(x_prompt, x_sample, state_hgrn, p_prompt, p_sample, norm_mix, w_in_a, lb_logits,
           gnorm_a, w_out_a, w_in_b, ln_v_g, ln_v_b, w_spatial, b_spatial, w_out_b,
           norm_ple, w_ple_gate, w_ple_proj, norm_final):
    depth = norm_mix.shape[0]
    n_s = x_sample.shape[0]
    assert x_prompt.shape[1] % TL == 0 and x_prompt.shape[1] % TLB == 0
    assert TLB % B_CHUNK == 0 and x_sample.shape[1] == 1

    def layer_weights(i):
        mixer = (w_in_a, w_out_a) if i % 2 == 0 else (w_in_b, w_out_b)
        return ((mixer[0], i // 2), (mixer[1], i // 2), (w_ple_gate, i), (w_ple_proj, i))

    wb = tuple(stack[idx].astype(BF16) for stack, idx in layer_weights(0))

    tri = jnp.asarray(_TRI_NP, dtype=BF16)
    lvl = jnp.asarray(_LVL_NP)
    nf = norm_final.reshape(1, D_MODEL)
    bias_full = jnp.repeat(jnp.swapaxes(b_spatial, 1, 2), DG, axis=2)
    w00 = jnp.repeat(w_spatial[:, :, 0, 0], DG, axis=1)[:, None, :]
    b0 = jnp.repeat(b_spatial[:, :, 0], DG, axis=1)[:, None, :]
    nmix = norm_mix[:, None, :]
    nple = norm_ple[:, None, :]
    gn = gnorm_a[:, None, :]
    lng = ln_v_g[:, None, :]
    lnb = ln_v_b[:, None, :]

    if depth % 2 == 1:
        raise NotImplementedError("final norm is fused into the last (chunk-MLP) layer")
    hp = x_prompt
    hs = x_sample.reshape(n_s, D_MODEL)
    ps = p_sample.reshape(depth, n_s, PLE_DIM)
    vt_s = []
    st_p = st_s = vt_p = None
    for i in range(depth):
        j = i // 2
        last = i == depth - 1
        nxt = None if last else layer_weights(i + 1)
        w_in, w_out, w_gate, w_proj = wb
        if i % 2 == 0:
            qfkv, z_s = _hgrn_sample_front(hs, i, j, lb_logits, nmix, w_in)
            hp, st_p, st_s, o_s, *wb = _hgrn_prompt_layer(
                hp, p_prompt, i, j, lb_logits, nmix, w_in, gn, w_out, nple, w_gate, w_proj,
                tri, lvl, qfkv, state_hgrn, st_s, st_p, nxt)
            hs = _hgrn_sample_back(hs, o_s, z_s, ps, i, j, gn, w_out, nple, w_gate, w_proj)
        else:
            hp, vt_p, hs, vts, *wb = _gmlp_prompt_layer(
                hp, p_prompt, i, j, nmix, w_in, lng, lnb, w_spatial, bias_full, w_out, nple,
                w_gate, w_proj, nf, last, w_in_b.shape[0], vt_p, nxt, (hs, ps, w00, b0))
            vt_s.append(vts.reshape(n_s, 1, D_INNER))
    return (hp, hs.reshape(n_s, 1, D_MODEL), st_p, st_s, vt_p, jnp.stack(vt_s))
```

```python
import functools
import math

import numpy as np
import jax
import jax.numpy as jnp
from jax import lax
from jax.experimental import pallas as pl
from jax.experimental.pallas import tpu as pltpu

D_MODEL = 1024
D_INNER = 2048
HEADS = 8
DK = 128
DV = 256
GROUPS = 8
DG = 256
B_CHUNK = 128
PLE_DIM = 256
EPS = 1e-6
LOG2_E = 1.4426950408889634

TL = 256
TLA = 512
TLB = 512
HALF = TL // 2
N_LEVELS = int(math.log2(TL))
WEIGHT_CHUNK_3 = 1536
WEIGHT_CHUNK_1 = 512
VMEM_LIMIT = 63 * 1024 * 1024

F32 = jnp.float32
BF16 = jnp.bfloat16


def _build_level_consts(tl):
    t = np.arange(tl)[:, None]
    s = np.arange(tl)[None, :]
    tri = (s <= t).astype(np.float32)
    x = np.maximum(t ^ s, 1)
    p = np.floor(np.log2(x)).astype(np.int32)
    nlev = int(math.log2(tl))
    lvl = np.where(t == s, 0, np.where(t > s, nlev - p, -1)).astype(np.int32)
    return tri, lvl[:tl // 2, :tl // 2]


_TRI_NP, _LVL_NP = _build_level_consts(TL)


def _dot(a, b):
    return jnp.dot(a, b, preferred_element_type=F32)


def _dot_cols(a, w_refs):
    return jnp.concatenate([_dot(a, w[...]) for w in w_refs], axis=1)


def _dot_nt(a, b):
    return lax.dot_general(a, b, (((1,), (1,)), ((), ())), preferred_element_type=F32)


def _dot_tn(a, b):
    return lax.dot_general(a, b, (((0,), (0,)), ((), ())), preferred_element_type=F32)


def _rmsnorm(x, g):
    ms = jnp.mean(x * x, axis=-1, keepdims=True)
    return x * lax.rsqrt(ms + EPS) * g


def _sigmoid(x):
    return 0.5 * jnp.tanh(0.5 * x) + 0.5


def _silu(x):
    hx = 0.5 * x
    return hx * jnp.tanh(hx) + hx


def _gelu_tanh(x):
    c = math.sqrt(2.0 / math.pi)
    return 0.5 * x * (1.0 + jnp.tanh(c * (x + 0.044715 * (x * x * x))))


def _forget_lower_bound(lbl, j):
    mx = jnp.max(lbl, axis=0, keepdims=True)
    e = jnp.exp(lbl - mx)
    sm = e / jnp.sum(e, axis=0, keepdims=True)
    cum0 = sm[0:1, :]
    cum = cum0
    for i in range(1, j + 1):
        cum = cum + sm[i:i + 1, :]
    return cum - cum0


def _forget_gates(fpre, lb):
    e = jnp.exp(-jnp.abs(fpre))
    log_sig = jnp.minimum(fpre, 0.0) - jnp.log1p(e)
    a = jnp.log(lb)
    y = jnp.log1p(-lb) + log_sig
    logf = jnp.maximum(a, y) + jnp.log1p(jnp.exp(-jnp.abs(a - y)))
    one_minus_f = (1.0 - lb) * (jnp.where(fpre >= 0.0, e, 1.0) / (1.0 + e))
    return logf, one_minus_f


def _hgrn_front(x, nm, w_in, lb):
    hn = _rmsnorm(x, nm).astype(BF16)
    proj = _dot_cols(hn, w_in)
    q = _silu(proj[:, :D_MODEL])
    logf, kk = _forget_gates(proj[:, D_MODEL:2 * D_MODEL], lb)
    v = proj[:, 2 * D_MODEL:2 * D_MODEL + D_INNER]
    z = proj[:, 2 * D_MODEL + D_INNER:]
    return q, logf, kk, v, z


def _head_rmsnorm(o, gn):
    outs = []
    for h in range(HEADS):
        outs.append(_rmsnorm(o[:, h * DV:(h + 1) * DV], gn))
    return jnp.concatenate(outs, axis=1)


def _ple(h, p, npl, w_gate, w_proj):
    gate = _sigmoid(_dot_cols(_rmsnorm(h, npl).astype(BF16), w_gate))
    return h + gate * _dot_cols(p.astype(BF16), w_proj)


def _mixer_back(x, branch, w_out, p, npl, w_gate, w_proj):
    h = x + _dot_cols(branch.astype(BF16), w_out)
    return _ple(h, p, npl, w_gate, w_proj)


def _gmlp_front(x, nm, w_in, ln_g, ln_b):
    hn = _rmsnorm(x, nm).astype(BF16)
    proj = _dot_cols(hn, w_in)
    u = _gelu_tanh(proj[:, :D_INNER])
    vg = _gelu_tanh(proj[:, D_INNER:2 * D_INNER])
    z = proj[:, 2 * D_INNER:]
    mu = jnp.mean(vg, axis=-1, keepdims=True)
    vc = vg - mu
    vn = vc * lax.rsqrt(jnp.mean(vc * vc, axis=-1, keepdims=True) + EPS) * ln_g + ln_b
    return u, vn, z


def _hgrn_prompt_kernel(x_ref, p_ref, lbl_ref, nm_ref, win_ref, gn_ref, wout_ref, npl_ref,
                        wg_ref, wp_ref, tri_ref, lvl_ref, sq_ref, sf_ref, sk_ref, sv_ref, ss_ref,
                        zero_ref, nxt_refs, *rest, layer_j):
    n_out = 4 + len(nxt_refs)
    y_ref, sfin_ref, ssnew_ref, so_ref = rest[len(rest) - n_out - 1:len(rest) - n_out + 3]
    st_ref = rest[-1]
    l = pl.program_id(1)
    if nxt_refs:
        _cast_next_weights(pl.program_id(0) * pl.num_programs(1) + l, nxt_refs,
                           rest[len(rest) - n_out + 3:len(rest) - 1])

    @pl.when(l == 0)
    def _():
        st_ref[...] = jnp.zeros_like(st_ref)

    _sample_state_update(sq_ref, sf_ref, sk_ref, sv_ref, ss_ref, ssnew_ref, so_ref)

    lb = _forget_lower_bound(lbl_ref[...], layer_j)
    for c in range(TLA // TL):
        _hgrn_chunk(slice(c * TL, (c + 1) * TL), so_ref if c == 0 else None, lb, x_ref, p_ref,
                    nm_ref, win_ref, gn_ref, wout_ref, npl_ref, wg_ref, wp_ref, tri_ref, lvl_ref,
                    zero_ref, y_ref, st_ref)

    @pl.when(l == pl.num_programs(1) - 1)
    def _():
        for h in range(HEADS):
            sfin_ref[h] = st_ref[h].T


def _hgrn_chunk(rows, anchor_ref, lb, x_ref, p_ref, nm_ref, win_ref, gn_ref, wout_ref, npl_ref,
                wg_ref, wp_ref, tri_ref, lvl_ref, zero_ref, y_ref, st_ref):
    x = x_ref[rows, :]
    q, logf, kk, v, z = _hgrn_front(x, nm_ref[...], win_ref, lb)
    if anchor_ref is not None:
        done = lax.bitcast_convert_type(anchor_ref[0, :, :DK], jnp.int32) & zero_ref[...]
        q_patch = q[:8, :DK] + lax.bitcast_convert_type(done, F32)
        q = jnp.concatenate([jnp.concatenate([q_patch, q[:8, DK:]], axis=1), q[8:]], axis=0)

    lg = logf * LOG2_E
    g_hi = lg.astype(BF16)
    g_lo = (lg - g_hi.astype(F32)).astype(BF16)
    g2 = jnp.concatenate([g_hi, g_lo], axis=1)
    b2 = _dot(tri_ref[...], g2)
    b = b2[:, :D_MODEL] + b2[:, D_MODEL:]
    b_last = b[TL - 1:TL, :]
    qs = (q * jnp.exp2(b)).astype(BF16)
    kd = (kk * jnp.exp2(b_last - b)).astype(BF16)
    dec = jnp.exp2(b_last)
    qb = q.astype(BF16)
    kb = kk.astype(BF16)
    vb = v.astype(BF16)

    zk, zq = {}, {}
    for level in range(1, N_LEVELS + 1):
        hs = TL >> level
        if hs >= 8:
            grp = 2 * hs
            shp = (TL // grp, grp, D_MODEL)
            b3, q3, k3 = b.reshape(shp), q.reshape(shp), kk.reshape(shp)
            mid = b3[:, hs - 1:hs, :]
            zq3 = q3[:, hs:, :] * jnp.exp2(b3[:, hs:, :] - mid)
            zk3 = k3[:, :hs, :] * jnp.exp2(mid - b3[:, :hs, :])
            zq[level] = zq3.reshape(TL // 2, D_MODEL).astype(BF16)
            zk[level] = jnp.concatenate([zk3, zq3], axis=1).reshape(TL, D_MODEL).astype(BF16)
        else:
            shp = (TL // 8, 8, D_MODEL)
            b8, q8, k8 = b.reshape(shp), q.reshape(shp), kk.reshape(shp)
            sub = lax.broadcasted_iota(jnp.int32, shp, 1)
            is_q = (sub & hs) != 0
            if hs == 4:
                mid = jnp.broadcast_to(b8[:, 3:4, :], shp)
            elif hs == 2:
                mid = jnp.where(sub < 4, jnp.broadcast_to(b8[:, 1:2, :], shp),
                                jnp.broadcast_to(b8[:, 5:6, :], shp))
            else:
                mid = None
            if mid is None:
                e = jnp.where(is_q, lg.reshape(shp), 0.0)
            else:
                e = jnp.where(is_q, b8 - mid, mid - b8)
            zfine = jnp.where(is_q, q8, k8) * jnp.exp2(e)
            zk[level] = zfine.reshape(TL, D_MODEL).astype(BF16)

    lvl = lvl_ref[...]
    atts = []
    for h in range(HEADS):
        sl = slice(h * DK, (h + 1) * DK)
        pd = _dot_nt(qb[:, sl], kb[:, sl])
        diag = [jnp.where(lvl == 0, pd[:HALF, :HALF], 0.0),
                jnp.where(lvl == 0, pd[HALF:, HALF:], 0.0)]
        for level in range(N_LEVELS, 1, -1):
            hs = TL >> level
            zkl = zk[level][:, sl]
            if hs >= 8:
                grp = 2 * hs
                nq = HALF // 2
                pr = _dot_nt(zq[level][:, sl], zkl)
                hit = lvl.reshape(HALF // grp, grp, HALF)[:, hs:, :] == level
                for i in range(2):
                    pq = pr[i * nq:(i + 1) * nq, i * HALF:(i + 1) * HALF]
                    d3 = diag[i].reshape(HALF // grp, grp, HALF)
                    upper = jnp.where(hit, pq.reshape(HALF // grp, hs, HALF), d3[:, hs:, :])
                    diag[i] = jnp.concatenate([d3[:, :hs, :], upper], axis=1).reshape(HALF, HALF)
            else:
                pf = _dot_nt(zkl, zkl)
                hit = lvl == level
                diag[0] = jnp.where(hit, pf[:HALF, :HALF], diag[0])
                diag[1] = jnp.where(hit, pf[HALF:, HALF:], diag[1])
        a10 = _dot_nt(zq[1][:, sl], zk[1][:HALF, sl])
        att = jnp.concatenate(
            [jnp.concatenate([diag[0], jnp.zeros((HALF, HALF), F32)], axis=1),
             jnp.concatenate([a10, diag[1]], axis=1)], axis=0)
        atts.append(att.astype(BF16))
    outs = []
    for h in range(HEADS):
        sl = slice(h * DK, (h + 1) * DK)
        vsl = slice(h * DV, (h + 1) * DV)
        st = st_ref[h]
        o = _dot_nt(qs[:, sl], st.astype(BF16)) + _dot(atts[h], vb[:, vsl])
        st_ref[h] = st * dec[:, sl] + _dot_tn(vb[:, vsl], kd[:, sl])
        outs.append(o)
    o = jnp.concatenate(outs, axis=1)

    branch = _head_rmsnorm(o, gn_ref[...]) * _silu(z)
    y_ref[rows, :] = _mixer_back(x, branch, wout_ref, p_ref[rows, :], npl_ref[...],
                                 wg_ref, wp_ref)


def _gmlp_prompt_kernel(x_ref, p_ref, nm_ref, win_ref, lng_ref, lnb_ref, wsp_ref, bsp_ref,
                        wout_ref, npl_ref, wg_ref, wp_ref, nf_ref, samp_refs, nxt_refs, *rest,
                        final_norm):
    n_out = 4 + len(nxt_refs)
    y_ref, vt_ref, ys_ref, vts_ref = rest[len(rest) - n_out:len(rest) - n_out + 4]
    l = pl.program_id(1)
    step = pl.program_id(0) * pl.num_programs(1) + l
    if nxt_refs:
        _cast_next_weights(step, nxt_refs, rest[len(rest) - n_out + 4:])

    @pl.when(step == 0)
    def _():
        xs_ref, ps_ref, w00_ref, b0_ref = samp_refs
        _gmlp_sample_body(xs_ref, ps_ref, nm_ref, win_ref, lng_ref, lnb_ref, w00_ref, b0_ref,
                          wout_ref, npl_ref, wg_ref, wp_ref, nf_ref, ys_ref, vts_ref,
                          final_norm=final_norm)

    x = x_ref[...]
    u, vn, z = _gmlp_front(x, nm_ref[...], win_ref, lng_ref[...], lnb_ref[...])
    vb = vn.astype(BF16)
    r = lax.broadcasted_iota(jnp.int32, (B_CHUNK, B_CHUNK), 0)
    c = lax.broadcasted_iota(jnp.int32, (B_CHUNK, B_CHUNK), 1)
    tril = c <= r
    bias = bsp_ref[...]
    cols = []
    for g in range(GROUPS):
        w = jnp.where(tril, wsp_ref[g], 0.0).astype(BF16)
        gsl = slice(g * DG, (g + 1) * DG)
        chunks = []
        for ci in range(TLB // B_CHUNK):
            rsl = slice(ci * B_CHUNK, (ci + 1) * B_CHUNK)
            chunks.append(_dot(w, vb[rsl, gsl]) + bias[:, gsl])
        cols.append(jnp.concatenate(chunks, axis=0))
    s = jnp.concatenate(cols, axis=1)
    branch = u * s * _silu(z)
    h = _mixer_back(x, branch, wout_ref, p_ref[...], npl_ref[...], wg_ref, wp_ref)
    if final_norm:
        h = _rmsnorm(h, nf_ref[...])
    y_ref[...] = h

    @pl.when(l == pl.num_programs(1) - 1)
    def _():
        vt_ref[...] = vn[TLB - B_CHUNK:, :]


def _hgrn_sample_front_kernel(x_ref, lbl_ref, nm_ref, win_ref, q_ref, f_ref, k_ref, v_ref, z_ref,
                              *, layer_j):
    lb = _forget_lower_bound(lbl_ref[...], layer_j)
    q, logf, kk, v, z = _hgrn_front(x_ref[...], nm_ref[...], win_ref, lb)
    q_ref[...] = q
    f_ref[...] = jnp.exp(logf)
    k_ref[...] = kk
    v_ref[...] = v
    z_ref[...] = z


def _sample_state_update(q_ref, f_ref, k_ref, v_ref, s_ref, snew_ref, o_ref):
    for i in range(q_ref.shape[0]):
        q_t = q_ref[i].T
        f_t = f_ref[i].T
        k_t = k_ref[i].T
        for h in range(HEADS):
            s_new = f_t[:, h:h + 1] * s_ref[i, h] + k_t[:, h:h + 1] * v_ref[i, h:h + 1, :]
            snew_ref[i, h] = s_new
            o_ref[i, h:h + 1, :] = jnp.sum(q_t[:, h:h + 1] * s_new, axis=0, keepdims=True)


def _cast_next_weights(step, nxt_refs, out_refs):
    for src, dst in zip(nxt_refs[:3], out_refs[:3]):
        dst[...] = src[...].astype(BF16)

    @pl.when(step == 0)
    def _():
        out_refs[3][...] = nxt_refs[3][...].astype(BF16)


def _hgrn_sample_back_kernel(x_ref, o_ref, z_ref, p_ref, gn_ref, wout_ref, npl_ref, wg_ref,
                             wp_ref, y_ref):
    branch = _head_rmsnorm(o_ref[...], gn_ref[...]) * _silu(z_ref[...])
    y_ref[...] = _mixer_back(x_ref[...], branch, wout_ref, p_ref[...], npl_ref[...],
                             wg_ref, wp_ref)


def _gmlp_sample_body(x_ref, p_ref, nm_ref, win_ref, lng_ref, lnb_ref, w00_ref, b0_ref,
                      wout_ref, npl_ref, wg_ref, wp_ref, nf_ref, y_ref, vt_ref, *, final_norm):
    x = x_ref[...]
    u, vn, z = _gmlp_front(x, nm_ref[...], win_ref, lng_ref[...], lnb_ref[...])
    s = w00_ref[...] * vn + b0_ref[...]
    branch = u * s * _silu(z)
    h = _mixer_back(x, branch, wout_ref, p_ref[...], npl_ref[...], wg_ref, wp_ref)
    if final_norm:
        h = _rmsnorm(h, nf_ref[...])
    y_ref[...] = h
    vt_ref[...] = vn


def _resident(shape, index):
    return pl.BlockSpec(shape, index, pipeline_mode=pl.Buffered(1))


def _chunk_width(n):
    width = WEIGHT_CHUNK_3 if n % WEIGHT_CHUNK_3 == 0 else WEIGHT_CHUNK_1
    assert n % width == 0 and width % 1024 != 0
    return width


def _weight_cols(w):
    k, n = w.shape
    width = _chunk_width(n)
    specs = [_resident((k, width), lambda *_, c=c: (0, c)) for c in range(n // width)]
    return [w] * len(specs), specs


def _next_weight_specs(nxt, n_steps, step_of):
    ops, ins, outs, shapes = [], [], [], []
    for idx, (stack, layer) in enumerate(nxt or ()):
        _, k, n = stack.shape
        ops.append(stack)
        shapes.append(jax.ShapeDtypeStruct((k, n), BF16))
        if idx < 3:
            rows = k // n_steps
            assert rows * n_steps == k and rows % 16 == 0
            ins.append(pl.BlockSpec((None, rows, n),
                                    lambda *g, layer=layer: (layer, step_of(*g), 0)))
            outs.append(pl.BlockSpec((rows, n), lambda *g: (step_of(*g), 0)))
        else:
            ins.append(_resident((None, k, n), lambda *g, layer=layer: (layer, 0, 0)))
            outs.append(pl.BlockSpec((k, n), lambda *g: (0, 0)))
    return ops, ins, outs, shapes


def _expand(args):
    ops, specs, sizes = [], [], []
    for o, s in args:
        if isinstance(o, list):
            ops += o
            specs += s
            sizes.append(len(o))
        else:
            ops.append(o)
            specs.append(s)
            sizes.append(None)
    return ops, specs, sizes


def _regroup(fn, sizes):
    def wrapped(*refs):
        pos, grouped = 0, []
        for n in sizes:
            if n is None:
                grouped.append(refs[pos])
                pos += 1
            else:
                grouped.append(tuple(refs[pos:pos + n]))
                pos += n
        return fn(*grouped, *refs[pos:])
    return wrapped


def _params(*sem):
    return pltpu.CompilerParams(dimension_semantics=sem, vmem_limit_bytes=VMEM_LIMIT)


def _hgrn_prompt_layer(x, p_all, i, j, lb_logits, norm_mix, w_in, gnorm, w_out, norm_ple,
                       w_gate, w_proj, tri, lvl, sample_qfkv, s0, s_prev, fin_prev, nxt):
    bsz, seq, _ = x.shape
    n_a = lb_logits.shape[0]
    n_l = seq // TLA
    grid = (bsz, n_l)
    n_s = s0.shape[1]
    assert n_s % (bsz * n_l) == 0
    spt = n_s // (bsz * n_l)
    const2 = lambda b, l: (0, 0)
    nxt_ops, nxt_in, nxt_out, nxt_shapes = _next_weight_specs(
        nxt, bsz * n_l, lambda b, l: b * n_l + l)
    s_k = pl.BlockSpec((spt, HEADS, DK), lambda b, l: (b * n_l + l, 0, 0))
    s_v = pl.BlockSpec((spt, HEADS, DV), lambda b, l: (b * n_l + l, 0, 0))
    s_st = pl.BlockSpec((None, spt, HEADS, DK, DV), lambda b, l: (j, b * n_l + l, 0, 0, 0))
    sq, sf, sk, sv = sample_qfkv
    operands, in_specs, sizes = _expand([
        (x, pl.BlockSpec((None, TLA, D_MODEL), lambda b, l: (b, l, 0))),
        (p_all, pl.BlockSpec((None, None, TLA, PLE_DIM), lambda b, l: (i, b, l, 0))),
        (lb_logits, _resident((n_a, D_MODEL), const2)),
        (norm_mix, _resident((None, 1, D_MODEL), lambda b, l: (i, 0, 0))),
        _weight_cols(w_in),
        (gnorm, _resident((None, 1, DV), lambda b, l: (j, 0, 0))),
        _weight_cols(w_out),
        (norm_ple, _resident((None, 1, D_MODEL), lambda b, l: (i, 0, 0))),
        _weight_cols(w_gate),
        _weight_cols(w_proj),
        (tri, _resident(tri.shape, const2)),
        (lvl, _resident(lvl.shape, const2)),
        (sq, s_k), (sf, s_k), (sk, s_k), (sv, s_v), (s0, s_st),
        (jnp.zeros((HEADS, DK), jnp.int32), _resident((HEADS, DK), const2)),
        (nxt_ops, nxt_in),
    ])
    aliases = {}
    for prev, out_idx in ((s_prev, 2), (fin_prev, 1)):
        if prev is not None:
            operands.append(prev)
            in_specs.append(pl.BlockSpec(memory_space=pl.ANY))
            aliases[len(operands) - 1] = out_idx
    out_specs = [
        pl.BlockSpec((None, TLA, D_MODEL), lambda b, l: (b, l, 0)),
        pl.BlockSpec((None, None, HEADS, DK, DV), lambda b, l: (j, b, 0, 0, 0)),
        s_st,
        s_v,
    ] + nxt_out
    out_shape = [
        jax.ShapeDtypeStruct((bsz, seq, D_MODEL), F32),
        jax.ShapeDtypeStruct((n_a, bsz, HEADS, DK, DV), F32),
        jax.ShapeDtypeStruct(s0.shape, F32),
        jax.ShapeDtypeStruct((n_s, HEADS, DV), F32),
    ] + nxt_shapes
    return pl.pallas_call(
        _regroup(functools.partial(_hgrn_prompt_kernel, layer_j=j), sizes),
        grid=grid, in_specs=in_specs, out_specs=out_specs, out_shape=out_shape,
        scratch_shapes=[pltpu.VMEM((HEADS, DV, DK), F32)],
        input_output_aliases=aliases,
        compiler_params=_params("arbitrary", "arbitrary"),
        name=f"hgrn_prompt_{j}",
    )(*operands)


def _gmlp_prompt_layer(x, p_all, i, j, norm_mix, w_in, ln_g, ln_b, w_sp, bias_full, w_out,
                       norm_ple, w_gate, w_proj, norm_final, final_norm, n_b, vt_prev, nxt, sample):
    xs, ps_all, w00, b0 = sample
    n_s = xs.shape[0]
    bsz, seq, _ = x.shape
    n_l = seq // TLB
    grid = (bsz, n_l)
    nxt_ops, nxt_in, nxt_out, nxt_shapes = _next_weight_specs(
        nxt, bsz * n_l, lambda b, l: b * n_l + l)
    operands, in_specs, sizes = _expand([
        (x, pl.BlockSpec((None, TLB, D_MODEL), lambda b, l: (b, l, 0))),
        (p_all, pl.BlockSpec((None, None, TLB, PLE_DIM), lambda b, l: (i, b, l, 0))),
        (norm_mix, _resident((None, 1, D_MODEL), lambda b, l: (i, 0, 0))),
        _weight_cols(w_in),
        (ln_g, _resident((None, 1, D_INNER), lambda b, l: (j, 0, 0))),
        (ln_b, _resident((None, 1, D_INNER), lambda b, l: (j, 0, 0))),
        (w_sp, _resident((None, GROUPS, B_CHUNK, B_CHUNK), lambda b, l: (j, 0, 0, 0))),
        (bias_full, _resident((None, B_CHUNK, D_INNER), lambda b, l: (j, 0, 0))),
        _weight_cols(w_out),
        (norm_ple, _resident((None, 1, D_MODEL), lambda b, l: (i, 0, 0))),
        _weight_cols(w_gate),
        _weight_cols(w_proj),
        (norm_final, _resident((1, D_MODEL), lambda b, l: (0, 0))),
        ([xs, ps_all, w00, b0],
         [_resident((n_s, D_MODEL), lambda b, l: (0, 0)),
          _resident((None, n_s, PLE_DIM), lambda b, l: (i, 0, 0)),
          _resident((None, 1, D_INNER), lambda b, l: (j, 0, 0)),
          _resident((None, 1, D_INNER), lambda b, l: (j, 0, 0))]),
        (nxt_ops, nxt_in),
    ])
    aliases = {}
    if vt_prev is not None:
        operands.append(vt_prev)
        in_specs.append(pl.BlockSpec(memory_space=pl.ANY))
        aliases[len(operands) - 1] = 1
    out_specs = [
        pl.BlockSpec((None, TLB, D_MODEL), lambda b, l: (b, l, 0)),
        pl.BlockSpec((None, None, B_CHUNK, D_INNER), lambda b, l: (j, b, 0, 0)),
        pl.BlockSpec((n_s, D_MODEL), lambda b, l: (0, 0)),
        pl.BlockSpec((n_s, D_INNER), lambda b, l: (0, 0)),
    ] + nxt_out
    out_shape = [
        jax.ShapeDtypeStruct((bsz, seq, D_MODEL), F32),
        jax.ShapeDtypeStruct((n_b, bsz, B_CHUNK, D_INNER), F32),
        jax.ShapeDtypeStruct((n_s, D_MODEL), F32),
        jax.ShapeDtypeStruct((n_s, D_INNER), F32),
    ] + nxt_shapes
    return pl.pallas_call(
        _regroup(functools.partial(_gmlp_prompt_kernel, final_norm=final_norm), sizes),
        grid=grid, in_specs=in_specs, out_specs=out_specs, out_shape=out_shape,
        input_output_aliases=aliases,
        compiler_params=_params("arbitrary", "arbitrary"),
        name=f"gmlp_prompt_{j}",
    )(*operands)


def _whole(shape):
    nd = len(shape)
    return pl.BlockSpec(shape, lambda *_: (0,) * nd)


def _hgrn_sample_front(x, i, j, lb_logits, norm_mix, w_in):
    n = x.shape[0]
    n_a = lb_logits.shape[0]
    operands, in_specs, sizes = _expand([
        (x, _whole((n, D_MODEL))),
        (lb_logits, _whole((n_a, D_MODEL))),
        (norm_mix, pl.BlockSpec((None, 1, D_MODEL), lambda g: (i, 0, 0))),
        _weight_cols(w_in),
    ])
    q, f, kk, v, z = pl.pallas_call(
        _regroup(functools.partial(_hgrn_sample_front_kernel, layer_j=j), sizes),
        grid=(1,),
        in_specs=in_specs,
        out_specs=[_whole((n, D_MODEL))] * 3 + [_whole((n, D_INNER))] * 2,
        out_shape=[jax.ShapeDtypeStruct((n, D_MODEL), F32)] * 3
        + [jax.ShapeDtypeStruct((n, D_INNER), F32)] * 2,
        compiler_params=_params("arbitrary"),
        name=f"hgrn_sample_front_{j}",
    )(*operands)

    qfkv = (q.reshape(n, HEADS, DK), f.reshape(n, HEADS, DK), kk.reshape(n, HEADS, DK),
            v.reshape(n, HEADS, DV))
    return qfkv, z


def _hgrn_sample_back(x, o, z, p_all, i, j, gnorm, w_out, norm_ple, w_gate, w_proj):
    n = x.shape[0]
    operands, in_specs, sizes = _expand([
        (x, _whole((n, D_MODEL))),
        (o.reshape(n, D_INNER), _whole((n, D_INNER))),
        (z, _whole((n, D_INNER))),
        (p_all, pl.BlockSpec((None, n, PLE_DIM), lambda g: (i, 0, 0))),
        (gnorm, pl.BlockSpec((None, 1, DV), lambda g: (j, 0, 0))),
        _weight_cols(w_out),
        (norm_ple, pl.BlockSpec((None, 1, D_MODEL), lambda g: (i, 0, 0))),
        _weight_cols(w_gate),
        _weight_cols(w_proj),
    ])
    return pl.pallas_call(
        _regroup(_hgrn_sample_back_kernel, sizes),
        grid=(1,),
        in_specs=in_specs,
        out_specs=_whole((n, D_MODEL)),
        out_shape=jax.ShapeDtypeStruct((n, D_MODEL), F32),
        compiler_params=_params("arbitrary"),
        name=f"hgrn_sample_back_{j}",
    )(*operands)


def kernel(x_prompt, x_sample, state_hgrn, p_prompt, p_sample, norm_mix, w_in_a, lb_logits,
           gnorm_a, w_out_a, w_in_b, ln_v_g, ln_v_b, w_spatial, b_spatial, w_out_b,
           norm_ple, w_ple_gate, w_ple_proj, norm_final):
    depth = norm_mix.shape[0]
    n_s = x_sample.shape[0]
    assert x_prompt.shape[1] % TLA == 0 and TLA % TL == 0 and x_prompt.shape[1] % TLB == 0
    assert TLB % B_CHUNK == 0 and x_sample.shape[1] == 1

    def layer_weights(i):
        mixer = (w_in_a, w_out_a) if i % 2 == 0 else (w_in_b, w_out_b)
        return ((mixer[0], i // 2), (mixer[1], i // 2), (w_ple_gate, i), (w_ple_proj, i))

    wb = tuple(stack[idx].astype(BF16) for stack, idx in layer_weights(0))

    tri = jnp.asarray(_TRI_NP, dtype=BF16)
    lvl = jnp.asarray(_LVL_NP)
    nf = norm_final.reshape(1, D_MODEL)
    bias_full = jnp.repeat(jnp.swapaxes(b_spatial, 1, 2), DG, axis=2)
    w00 = jnp.repeat(w_spatial[:, :, 0, 0], DG, axis=1)[:, None, :]
    b0 = jnp.repeat(b_spatial[:, :, 0], DG, axis=1)[:, None, :]
    nmix = norm_mix[:, None, :]
    nple = norm_ple[:, None, :]
    gn = gnorm_a[:, None, :]
    lng = ln_v_g[:, None, :]
    lnb = ln_v_b[:, None, :]

    if depth % 2 == 1:
        raise NotImplementedError("final norm is fused into the last (chunk-MLP) layer")
    hp = x_prompt
    hs = x_sample.reshape(n_s, D_MODEL)
    ps = p_sample.reshape(depth, n_s, PLE_DIM)
    vt_s = []
    st_p = st_s = vt_p = None
    for i in range(depth):
        j = i // 2
        last = i == depth - 1
        nxt = None if last else layer_weights(i + 1)
        w_in, w_out, w_gate, w_proj = wb
        if i % 2 == 0:
            qfkv, z_s = _hgrn_sample_front(hs, i, j, lb_logits, nmix, w_in)
            hp, st_p, st_s, o_s, *wb = _hgrn_prompt_layer(
                hp, p_prompt, i, j, lb_logits, nmix, w_in, gn, w_out, nple, w_gate, w_proj,
                tri, lvl, qfkv, state_hgrn, st_s, st_p, nxt)
            hs = _hgrn_sample_back(hs, o_s, z_s, ps, i, j, gn, w_out, nple, w_gate, w_proj)
        else:
            hp, vt_p, hs, vts, *wb = _gmlp_prompt_layer(
                hp, p_prompt, i, j, nmix, w_in, lng, lnb, w_spatial, bias_full, w_out, nple,
                w_gate, w_proj, nf, last, w_in_b.shape[0], vt_p, nxt, (hs, ps, w00, b0))
            vt_s.append(vts.reshape(n_s, 1, D_INNER))
    return (hp, hs.reshape(n_s, 1, D_MODEL), st_p, st_s, vt_p, jnp.stack(vt_s))
```

```python
import functools
import math

import numpy as np
import jax
import jax.numpy as jnp
from jax import lax
from jax.experimental import pallas as pl
from jax.experimental.pallas import tpu as pltpu

D_MODEL = 1024
D_INNER = 2048
HEADS = 8
DK = 128
DV = 256
GROUPS = 8
DG = 256
B_CHUNK = 128
PLE_DIM = 256
EPS = 1e-6
LOG2_E = 1.4426950408889634

TL = 256
TLA = 512
TLB = 512
HALF = TL // 2
N_LEVELS = int(math.log2(TL))
WEIGHT_CHUNK_3 = 1536
WEIGHT_CHUNK_1 = 512
VMEM_LIMIT = 63 * 1024 * 1024

F32 = jnp.float32
BF16 = jnp.bfloat16


def _build_level_consts(tl):
    t = np.arange(tl)[:, None]
    s = np.arange(tl)[None, :]
    tri = (s <= t).astype(np.float32)
    x = np.maximum(t ^ s, 1)
    p = np.floor(np.log2(x)).astype(np.int32)
    nlev = int(math.log2(tl))
    lvl = np.where(t == s, 0, np.where(t > s, nlev - p, -1)).astype(np.int32)
    return tri, lvl[:tl // 2, :tl // 2]


_TRI_NP, _LVL_NP = _build_level_consts(TL)


def _dot(a, b):
    return jnp.dot(a, b, preferred_element_type=F32)


def _dot_cols(a, w_refs):
    return jnp.concatenate([_dot(a, w[...]) for w in w_refs], axis=1)


def _dot_nt(a, b):
    return lax.dot_general(a, b, (((1,), (1,)), ((), ())), preferred_element_type=F32)


def _dot_tn(a, b):
    return lax.dot_general(a, b, (((0,), (0,)), ((), ())), preferred_element_type=F32)


def _rmsnorm(x, g):
    ms = jnp.mean(x * x, axis=-1, keepdims=True)
    return x * lax.rsqrt(ms + EPS) * g


def _sigmoid(x):
    return 0.5 * jnp.tanh(0.5 * x) + 0.5


def _silu(x):
    hx = 0.5 * x
    return hx * jnp.tanh(hx) + hx


def _gelu_tanh(x):
    c = math.sqrt(2.0 / math.pi)
    return 0.5 * x * (1.0 + jnp.tanh(c * (x + 0.044715 * (x * x * x))))


def _forget_lower_bound(lbl, j):
    mx = jnp.max(lbl, axis=0, keepdims=True)
    e = jnp.exp(lbl - mx)
    sm = e / jnp.sum(e, axis=0, keepdims=True)
    cum0 = sm[0:1, :]
    cum = cum0
    for i in range(1, j + 1):
        cum = cum + sm[i:i + 1, :]
    return cum - cum0


def _forget_gates(fpre, lb):
    e = jnp.exp(-jnp.abs(fpre))
    log_sig = jnp.minimum(fpre, 0.0) - jnp.log1p(e)
    a = jnp.log(lb)
    y = jnp.log1p(-lb) + log_sig
    logf = jnp.maximum(a, y) + jnp.log1p(jnp.exp(-jnp.abs(a - y)))
    one_minus_f = (1.0 - lb) * (jnp.where(fpre >= 0.0, e, 1.0) / (1.0 + e))
    return logf, one_minus_f


def _hgrn_front(x, nm, w_in, lb):
    hn = _rmsnorm(x, nm).astype(BF16)
    proj = _dot_cols(hn, w_in)
    q = _silu(proj[:, :D_MODEL])
    logf, kk = _forget_gates(proj[:, D_MODEL:2 * D_MODEL], lb)
    v = proj[:, 2 * D_MODEL:2 * D_MODEL + D_INNER]
    z = proj[:, 2 * D_MODEL + D_INNER:]
    return q, logf, kk, v, z


def _head_rmsnorm(o, gn):
    outs = []
    for h in range(HEADS):
        outs.append(_rmsnorm(o[:, h * DV:(h + 1) * DV], gn))
    return jnp.concatenate(outs, axis=1)


def _ple(h, p, npl, w_gate, w_proj):
    gate = _sigmoid(_dot_cols(_rmsnorm(h, npl).astype(BF16), w_gate))
    return h + gate * _dot_cols(p.astype(BF16), w_proj)


def _mixer_back(x, branch, w_out, p, npl, w_gate, w_proj):
    h = x + _dot_cols(branch.astype(BF16), w_out)
    return _ple(h, p, npl, w_gate, w_proj)


def _gmlp_front(x, nm, w_in, ln_g, ln_b):
    hn = _rmsnorm(x, nm).astype(BF16)
    proj = _dot_cols(hn, w_in)
    u = _gelu_tanh(proj[:, :D_INNER])
    vg = _gelu_tanh(proj[:, D_INNER:2 * D_INNER])
    z = proj[:, 2 * D_INNER:]
    mu = jnp.mean(vg, axis=-1, keepdims=True)
    vc = vg - mu
    vn = vc * lax.rsqrt(jnp.mean(vc * vc, axis=-1, keepdims=True) + EPS) * ln_g + ln_b
    return u, vn, z


def _hgrn_prompt_kernel(x_ref, p_ref, lbl_ref, nm_ref, win_ref, gn_ref, wout_ref, npl_ref,
                        wg_ref, wp_ref, tri_ref, lvl_ref, sq_ref, sf_ref, sk_ref, sv_ref, ss_ref,
                        zero_ref, nxt_refs, *rest, layer_j):
    n_out = 4 + len(nxt_refs)
    y_ref, sfin_ref, ssnew_ref, so_ref = rest[len(rest) - n_out - 1:len(rest) - n_out + 3]
    st_ref = rest[-1]
    l = pl.program_id(1)
    if nxt_refs:
        _cast_next_weights(pl.program_id(0) * pl.num_programs(1) + l, nxt_refs,
                           rest[len(rest) - n_out + 3:len(rest) - 1])

    @pl.when(l == 0)
    def _():
        st_ref[...] = jnp.zeros_like(st_ref)

    _sample_state_update(sq_ref, sf_ref, sk_ref, sv_ref, ss_ref, ssnew_ref, so_ref)

    lb = _forget_lower_bound(lbl_ref[...], layer_j)

    def chunk(c, carry):
        _hgrn_chunk(pl.ds(pl.multiple_of(c * TL, TL), TL), so_ref, lb, x_ref, p_ref, nm_ref,
                    win_ref, gn_ref, wout_ref, npl_ref, wg_ref, wp_ref, tri_ref, lvl_ref, zero_ref,
                    y_ref, st_ref)
        return carry

    lax.fori_loop(0, TLA // TL, chunk, 0)

    @pl.when(l == pl.num_programs(1) - 1)
    def _():
        for h in range(HEADS):
            sfin_ref[h] = st_ref[h].T


def _hgrn_chunk(rows, so_ref, lb, x_ref, p_ref, nm_ref, win_ref, gn_ref, wout_ref, npl_ref,
                wg_ref, wp_ref, tri_ref, lvl_ref, zero_ref, y_ref, st_ref):
    x = x_ref[rows, :]
    q, logf, kk, v, z = _hgrn_front(x, nm_ref[...], win_ref, lb)
    done = lax.bitcast_convert_type(so_ref[0, :, :DK], jnp.int32) & zero_ref[...]
    q_patch = q[:8, :DK] + lax.bitcast_convert_type(done, F32)
    q = jnp.concatenate([jnp.concatenate([q_patch, q[:8, DK:]], axis=1), q[8:]], axis=0)

    lg = logf * LOG2_E
    g_hi = lg.astype(BF16)
    g_lo = (lg - g_hi.astype(F32)).astype(BF16)
    g2 = jnp.concatenate([g_hi, g_lo], axis=1)
    b2 = _dot(tri_ref[...], g2)
    b = b2[:, :D_MODEL] + b2[:, D_MODEL:]
    b_last = b[TL - 1:TL, :]
    qs = (q * jnp.exp2(b)).astype(BF16)
    kd = (kk * jnp.exp2(b_last - b)).astype(BF16)
    dec = jnp.exp2(b_last)
    qb = q.astype(BF16)
    kb = kk.astype(BF16)
    vb = v.astype(BF16)

    zk, zq = {}, {}
    for level in range(1, N_LEVELS + 1):
        hs = TL >> level
        if hs >= 8:
            grp = 2 * hs
            shp = (TL // grp, grp, D_MODEL)
            b3, q3, k3 = b.reshape(shp), q.reshape(shp), kk.reshape(shp)
            mid = b3[:, hs - 1:hs, :]
            zq3 = q3[:, hs:, :] * jnp.exp2(b3[:, hs:, :] - mid)
            zk3 = k3[:, :hs, :] * jnp.exp2(mid - b3[:, :hs, :])
            zq[level] = zq3.reshape(TL // 2, D_MODEL).astype(BF16)
            zk[level] = jnp.concatenate([zk3, zq3], axis=1).reshape(TL, D_MODEL).astype(BF16)
        else:
            shp = (TL // 8, 8, D_MODEL)
            b8, q8, k8 = b.reshape(shp), q.reshape(shp), kk.reshape(shp)
            sub = lax.broadcasted_iota(jnp.int32, shp, 1)
            is_q = (sub & hs) != 0
            if hs == 4:
                mid = jnp.broadcast_to(b8[:, 3:4, :], shp)
            elif hs == 2:
                mid = jnp.where(sub < 4, jnp.broadcast_to(b8[:, 1:2, :], shp),
                                jnp.broadcast_to(b8[:, 5:6, :], shp))
            else:
                mid = None
            if mid is None:
                e = jnp.where(is_q, lg.reshape(shp), 0.0)
            else:
                e = jnp.where(is_q, b8 - mid, mid - b8)
            zfine = jnp.where(is_q, q8, k8) * jnp.exp2(e)
            zk[level] = zfine.reshape(TL, D_MODEL).astype(BF16)

    lvl = lvl_ref[...]
    atts = []
    for h in range(HEADS):
        sl = slice(h * DK, (h + 1) * DK)
        pd = _dot_nt(qb[:, sl], kb[:, sl])
        diag = [jnp.where(lvl == 0, pd[:HALF, :HALF], 0.0),
                jnp.where(lvl == 0, pd[HALF:, HALF:], 0.0)]
        for level in range(N_LEVELS, 1, -1):
            hs = TL >> level
            zkl = zk[level][:, sl]
            if hs >= 8:
                grp = 2 * hs
                nq = HALF // 2
                pr = _dot_nt(zq[level][:, sl], zkl)
                hit = lvl.reshape(HALF // grp, grp, HALF)[:, hs:, :] == level
                for i in range(2):
                    pq = pr[i * nq:(i + 1) * nq, i * HALF:(i + 1) * HALF]
                    d3 = diag[i].reshape(HALF // grp, grp, HALF)
                    upper = jnp.where(hit, pq.reshape(HALF // grp, hs, HALF), d3[:, hs:, :])
                    diag[i] = jnp.concatenate([d3[:, :hs, :], upper], axis=1).reshape(HALF, HALF)
            else:
                pf = _dot_nt(zkl, zkl)
                hit = lvl == level
                diag[0] = jnp.where(hit, pf[:HALF, :HALF], diag[0])
                diag[1] = jnp.where(hit, pf[HALF:, HALF:], diag[1])
        a10 = _dot_nt(zq[1][:, sl], zk[1][:HALF, sl])
        att = jnp.concatenate(
            [jnp.concatenate([diag[0], jnp.zeros((HALF, HALF), F32)], axis=1),
             jnp.concatenate([a10, diag[1]], axis=1)], axis=0)
        atts.append(att.astype(BF16))
    outs = []
    for h in range(HEADS):
        sl = slice(h * DK, (h + 1) * DK)
        vsl = slice(h * DV, (h + 1) * DV)
        st = st_ref[h]
        o = _dot_nt(qs[:, sl], st.astype(BF16)) + _dot(atts[h], vb[:, vsl])
        st_ref[h] = st * dec[:, sl] + _dot_tn(vb[:, vsl], kd[:, sl])
        outs.append(o)
    o = jnp.concatenate(outs, axis=1)

    branch = _head_rmsnorm(o, gn_ref[...]) * _silu(z)
    y_ref[rows, :] = _mixer_back(x, branch, wout_ref, p_ref[rows, :], npl_ref[...],
                                 wg_ref, wp_ref)


def _gmlp_prompt_kernel(x_ref, p_ref, nm_ref, win_ref, lng_ref, lnb_ref, wsp_ref, bsp_ref,
                        wout_ref, npl_ref, wg_ref, wp_ref, nf_ref, samp_refs, nxt_refs, *rest,
                        final_norm):
    n_out = 4 + len(nxt_refs)
    y_ref, vt_ref, ys_ref, vts_ref = rest[len(rest) - n_out:len(rest) - n_out + 4]
    l = pl.program_id(1)
    step = pl.program_id(0) * pl.num_programs(1) + l
    if nxt_refs:
        _cast_next_weights(step, nxt_refs, rest[len(rest) - n_out + 4:])

    @pl.when(step == 0)
    def _():
        xs_ref, ps_ref, w00_ref, b0_ref = samp_refs
        _gmlp_sample_body(xs_ref, ps_ref, nm_ref, win_ref, lng_ref, lnb_ref, w00_ref, b0_ref,
                          wout_ref, npl_ref, wg_ref, wp_ref, nf_ref, ys_ref, vts_ref,
                          final_norm=final_norm)

    x = x_ref[...]
    u, vn, z = _gmlp_front(x, nm_ref[...], win_ref, lng_ref[...], lnb_ref[...])
    vb = vn.astype(BF16)
    r = lax.broadcasted_iota(jnp.int32, (B_CHUNK, B_CHUNK), 0)
    c = lax.broadcasted_iota(jnp.int32, (B_CHUNK, B_CHUNK), 1)
    tril = c <= r
    bias = bsp_ref[...]
    cols = []
    for g in range(GROUPS):
        w = jnp.where(tril, wsp_ref[g], 0.0).astype(BF16)
        gsl = slice(g * DG, (g + 1) * DG)
        chunks = []
        for ci in range(TLB // B_CHUNK):
            rsl = slice(ci * B_CHUNK, (ci + 1) * B_CHUNK)
            chunks.append(_dot(w, vb[rsl, gsl]) + bias[:, gsl])
        cols.append(jnp.concatenate(chunks, axis=0))
    s = jnp.concatenate(cols, axis=1)
    branch = u * s * _silu(z)
    h = _mixer_back(x, branch, wout_ref, p_ref[...], npl_ref[...], wg_ref, wp_ref)
    if final_norm:
        h = _rmsnorm(h, nf_ref[...])
    y_ref[...] = h

    @pl.when(l == pl.num_programs(1) - 1)
    def _():
        vt_ref[...] = vn[TLB - B_CHUNK:, :]


def _hgrn_sample_front_kernel(x_ref, lbl_ref, nm_ref, win_ref, q_ref, f_ref, k_ref, v_ref, z_ref,
                              *, layer_j):
    lb = _forget_lower_bound(lbl_ref[...], layer_j)
    q, logf, kk, v, z = _hgrn_front(x_ref[...], nm_ref[...], win_ref, lb)
    q_ref[...] = q
    f_ref[...] = jnp.exp(logf)
    k_ref[...] = kk
    v_ref[...] = v
    z_ref[...] = z


def _sample_state_update(q_ref, f_ref, k_ref, v_ref, s_ref, snew_ref, o_ref):
    for i in range(q_ref.shape[0]):
        q_t = q_ref[i].T
        f_t = f_ref[i].T
        k_t = k_ref[i].T
        for h in range(HEADS):
            s_new = f_t[:, h:h + 1] * s_ref[i, h] + k_t[:, h:h + 1] * v_ref[i, h:h + 1, :]
            snew_ref[i, h] = s_new
            o_ref[i, h:h + 1, :] = jnp.sum(q_t[:, h:h + 1] * s_new, axis=0, keepdims=True)


def _cast_next_weights(step, nxt_refs, out_refs):
    for src, dst in zip(nxt_refs[:3], out_refs[:3]):
        dst[...] = src[...].astype(BF16)

    @pl.when(step == 0)
    def _():
        out_refs[3][...] = nxt_refs[3][...].astype(BF16)


def _hgrn_sample_back_kernel(x_ref, o_ref, z_ref, p_ref, gn_ref, wout_ref, npl_ref, wg_ref,
                             wp_ref, y_ref):
    branch = _head_rmsnorm(o_ref[...], gn_ref[...]) * _silu(z_ref[...])
    y_ref[...] = _mixer_back(x_ref[...], branch, wout_ref, p_ref[...], npl_ref[...],
                             wg_ref, wp_ref)


def _gmlp_sample_body(x_ref, p_ref, nm_ref, win_ref, lng_ref, lnb_ref, w00_ref, b0_ref,
                      wout_ref, npl_ref, wg_ref, wp_ref, nf_ref, y_ref, vt_ref, *, final_norm):
    x = x_ref[...]
    u, vn, z = _gmlp_front(x, nm_ref[...], win_ref, lng_ref[...], lnb_ref[...])
    s = w00_ref[...] * vn + b0_ref[...]
    branch = u * s * _silu(z)
    h = _mixer_back(x, branch, wout_ref, p_ref[...], npl_ref[...], wg_ref, wp_ref)
    if final_norm:
        h = _rmsnorm(h, nf_ref[...])
    y_ref[...] = h
    vt_ref[...] = vn


def _resident(shape, index):
    return pl.BlockSpec(shape, index, pipeline_mode=pl.Buffered(1))


def _chunk_width(n):
    width = WEIGHT_CHUNK_3 if n % WEIGHT_CHUNK_3 == 0 else WEIGHT_CHUNK_1
    assert n % width == 0 and width % 1024 != 0
    return width


def _weight_cols(w):
    k, n = w.shape
    width = _chunk_width(n)
    specs = [_resident((k, width), lambda *_, c=c: (0, c)) for c in range(n // width)]
    return [w] * len(specs), specs


def _next_weight_specs(nxt, n_steps, step_of):
    ops, ins, outs, shapes = [], [], [], []
    for idx, (stack, layer) in enumerate(nxt or ()):
        _, k, n = stack.shape
        ops.append(stack)
        shapes.append(jax.ShapeDtypeStruct((k, n), BF16))
        if idx < 3:
            rows = k // n_steps
            assert rows * n_steps == k and rows % 16 == 0
            ins.append(pl.BlockSpec((None, rows, n),
                                    lambda *g, layer=layer: (layer, step_of(*g), 0)))
            outs.append(pl.BlockSpec((rows, n), lambda *g: (step_of(*g), 0)))
        else:
            ins.append(_resident((None, k, n), lambda *g, layer=layer: (layer, 0, 0)))
            outs.append(pl.BlockSpec((k, n), lambda *g: (0, 0)))
    return ops, ins, outs, shapes


def _expand(args):
    ops, specs, sizes = [], [], []
    for o, s in args:
        if isinstance(o, list):
            ops += o
            specs += s
            sizes.append(len(o))
        else:
            ops.append(o)
            specs.append(s)
            sizes.append(None)
    return ops, specs, sizes


def _regroup(fn, sizes):
    def wrapped(*refs):
        pos, grouped = 0, []
        for n in sizes:
            if n is None:
                grouped.append(refs[pos])
                pos += 1
            else:
                grouped.append(tuple(refs[pos:pos + n]))
                pos += n
        return fn(*grouped, *refs[pos:])
    return wrapped


def _params(*sem):
    return pltpu.CompilerParams(dimension_semantics=sem, vmem_limit_bytes=VMEM_LIMIT)


def _hgrn_prompt_layer(x, p_all, i, j, lb_logits, norm_mix, w_in, gnorm, w_out, norm_ple,
                       w_gate, w_proj, tri, lvl, sample_qfkv, s0, s_prev, fin_prev, nxt):
    bsz, seq, _ = x.shape
    n_a = lb_logits.shape[0]
    n_l = seq // TLA
    grid = (bsz, n_l)
    n_s = s0.shape[1]
    assert n_s % (bsz * n_l) == 0
    spt = n_s // (bsz * n_l)
    const2 = lambda b, l: (0, 0)
    nxt_ops, nxt_in, nxt_out, nxt_shapes = _next_weight_specs(
        nxt, bsz * n_l, lambda b, l: b * n_l + l)
    s_k = pl.BlockSpec((spt, HEADS, DK), lambda b, l: (b * n_l + l, 0, 0))
    s_v = pl.BlockSpec((spt, HEADS, DV), lambda b, l: (b * n_l + l, 0, 0))
    s_st = pl.BlockSpec((None, spt, HEADS, DK, DV), lambda b, l: (j, b * n_l + l, 0, 0, 0))
    sq, sf, sk, sv = sample_qfkv
    operands, in_specs, sizes = _expand([
        (x, pl.BlockSpec((None, TLA, D_MODEL), lambda b, l: (b, l, 0))),
        (p_all, pl.BlockSpec((None, None, TLA, PLE_DIM), lambda b, l: (i, b, l, 0))),
        (lb_logits, _resident((n_a, D_MODEL), const2)),
        (norm_mix, _resident((None, 1, D_MODEL), lambda b, l: (i, 0, 0))),
        _weight_cols(w_in),
        (gnorm, _resident((None, 1, DV), lambda b, l: (j, 0, 0))),
        _weight_cols(w_out),
        (norm_ple, _resident((None, 1, D_MODEL), lambda b, l: (i, 0, 0))),
        _weight_cols(w_gate),
        _weight_cols(w_proj),
        (tri, _resident(tri.shape, const2)),
        (lvl, _resident(lvl.shape, const2)),
        (sq, s_k), (sf, s_k), (sk, s_k), (sv, s_v), (s0, s_st),
        (jnp.zeros((HEADS, DK), jnp.int32), _resident((HEADS, DK), const2)),
        (nxt_ops, nxt_in),
    ])
    aliases = {}
    for prev, out_idx in ((s_prev, 2), (fin_prev, 1)):
        if prev is not None:
            operands.append(prev)
            in_specs.append(pl.BlockSpec(memory_space=pl.ANY))
            aliases[len(operands) - 1] = out_idx
    out_specs = [
        pl.BlockSpec((None, TLA, D_MODEL), lambda b, l: (b, l, 0)),
        pl.BlockSpec((None, None, HEADS, DK, DV), lambda b, l: (j, b, 0, 0, 0)),
        s_st,
        s_v,
    ] + nxt_out
    out_shape = [
        jax.ShapeDtypeStruct((bsz, seq, D_MODEL), F32),
        jax.ShapeDtypeStruct((n_a, bsz, HEADS, DK, DV), F32),
        jax.ShapeDtypeStruct(s0.shape, F32),
        jax.ShapeDtypeStruct((n_s, HEADS, DV), F32),
    ] + nxt_shapes
    return pl.pallas_call(
        _regroup(functools.partial(_hgrn_prompt_kernel, layer_j=j), sizes),
        grid=grid, in_specs=in_specs, out_specs=out_specs, out_shape=out_shape,
        scratch_shapes=[pltpu.VMEM((HEADS, DV, DK), F32)],
        input_output_aliases=aliases,
        compiler_params=_params("arbitrary", "arbitrary"),
        name=f"hgrn_prompt_{j}",
    )(*operands)


def _gmlp_prompt_layer(x, p_all, i, j, norm_mix, w_in, ln_g, ln_b, w_sp, bias_full, w_out,
                       norm_ple, w_gate, w_proj, norm_final, final_norm, n_b, vt_prev, nxt, sample):
    xs, ps_all, w00, b0 = sample
    n_s = xs.shape[0]
    bsz, seq, _ = x.shape
    n_l = seq // TLB
    grid = (bsz, n_l)
    nxt_ops, nxt_in, nxt_out, nxt_shapes = _next_weight_specs(
        nxt, bsz * n_l, lambda b, l: b * n_l + l)
    operands, in_specs, sizes = _expand([
        (x, pl.BlockSpec((None, TLB, D_MODEL), lambda b, l: (b, l, 0))),
        (p_all, pl.BlockSpec((None, None, TLB, PLE_DIM), lambda b, l: (i, b, l, 0))),
        (norm_mix, _resident((None, 1, D_MODEL), lambda b, l: (i, 0, 0))),
        _weight_cols(w_in),
        (ln_g, _resident((None, 1, D_INNER), lambda b, l: (j, 0, 0))),
        (ln_b, _resident((None, 1, D_INNER), lambda b, l: (j, 0, 0))),
        (w_sp, _resident((None, GROUPS, B_CHUNK, B_CHUNK), lambda b, l: (j, 0, 0, 0))),
        (bias_full, _resident((None, B_CHUNK, D_INNER), lambda b, l: (j, 0, 0))),
        _weight_cols(w_out),
        (norm_ple, _resident((None, 1, D_MODEL), lambda b, l: (i, 0, 0))),
        _weight_cols(w_gate),
        _weight_cols(w_proj),
        (norm_final, _resident((1, D_MODEL), lambda b, l: (0, 0))),
        ([xs, ps_all, w00, b0],
         [_resident((n_s, D_MODEL), lambda b, l: (0, 0)),
          _resident((None, n_s, PLE_DIM), lambda b, l: (i, 0, 0)),
          _resident((None, 1, D_INNER), lambda b, l: (j, 0, 0)),
          _resident((None, 1, D_INNER), lambda b, l: (j, 0, 0))]),
        (nxt_ops, nxt_in),
    ])
    aliases = {}
    if vt_prev is not None:
        operands.append(vt_prev)
        in_specs.append(pl.BlockSpec(memory_space=pl.ANY))
        aliases[len(operands) - 1] = 1
    out_specs = [
        pl.BlockSpec((None, TLB, D_MODEL), lambda b, l: (b, l, 0)),
        pl.BlockSpec((None, None, B_CHUNK, D_INNER), lambda b, l: (j, b, 0, 0)),
        pl.BlockSpec((n_s, D_MODEL), lambda b, l: (0, 0)),
        pl.BlockSpec((n_s, D_INNER), lambda b, l: (0, 0)),
    ] + nxt_out
    out_shape = [
        jax.ShapeDtypeStruct((bsz, seq, D_MODEL), F32),
        jax.ShapeDtypeStruct((n_b, bsz, B_CHUNK, D_INNER), F32),
        jax.ShapeDtypeStruct((n_s, D_MODEL), F32),
        jax.ShapeDtypeStruct((n_s, D_INNER), F32),
    ] + nxt_shapes
    return pl.pallas_call(
        _regroup(functools.partial(_gmlp_prompt_kernel, final_norm=final_norm), sizes),
        grid=grid, in_specs=in_specs, out_specs=out_specs, out_shape=out_shape,
        input_output_aliases=aliases,
        compiler_params=_params("arbitrary", "arbitrary"),
        name=f"gmlp_prompt_{j}",
    )(*operands)


def _whole(shape):
    nd = len(shape)
    return pl.BlockSpec(shape, lambda *_: (0,) * nd)


def _hgrn_sample_front(x, i, j, lb_logits, norm_mix, w_in):
    n = x.shape[0]
    n_a = lb_logits.shape[0]
    operands, in_specs, sizes = _expand([
        (x, _whole((n, D_MODEL))),
        (lb_logits, _whole((n_a, D_MODEL))),
        (norm_mix, pl.BlockSpec((None, 1, D_MODEL), lambda g: (i, 0, 0))),
        _weight_cols(w_in),
    ])
    q, f, kk, v, z = pl.pallas_call(
        _regroup(functools.partial(_hgrn_sample_front_kernel, layer_j=j), sizes),
        grid=(1,),
        in_specs=in_specs,
        out_specs=[_whole((n, D_MODEL))] * 3 + [_whole((n, D_INNER))] * 2,
        out_shape=[jax.ShapeDtypeStruct((n, D_MODEL), F32)] * 3
        + [jax.ShapeDtypeStruct((n, D_INNER), F32)] * 2,
        compiler_params=_params("arbitrary"),
        name=f"hgrn_sample_front_{j}",
    )(*operands)

    qfkv = (q.reshape(n, HEADS, DK), f.reshape(n, HEADS, DK), kk.reshape(n, HEADS, DK),
            v.reshape(n, HEADS, DV))
    return qfkv, z


def _hgrn_sample_back(x, o, z, p_all, i, j, gnorm, w_out, norm_ple, w_gate, w_proj):
    n = x.shape[0]
    operands, in_specs, sizes = _expand([
        (x, _whole((n, D_MODEL))),
        (o.reshape(n, D_INNER), _whole((n, D_INNER))),
        (z, _whole((n, D_INNER))),
        (p_all, pl.BlockSpec((None, n, PLE_DIM), lambda g: (i, 0, 0))),
        (gnorm, pl.BlockSpec((None, 1, DV), lambda g: (j, 0, 0))),
        _weight_cols(w_out),
        (norm_ple, pl.BlockSpec((None, 1, D_MODEL), lambda g: (i, 0, 0))),
        _weight_cols(w_gate),
        _weight_cols(w_proj),
    ])
    return pl.pallas_call(
        _regroup(_hgrn_sample_back_kernel, sizes),
        grid=(1,),
        in_specs=in_specs,
        out_specs=_whole((n, D_MODEL)),
        out_shape=jax.ShapeDtypeStruct((n, D_MODEL), F32),
        compiler_params=_params("arbitrary"),
        name=f"hgrn_sample_back_{j}",
    )(*operands)


def kernel(x_prompt, x_sample, state_hgrn, p_prompt, p_sample, norm_mix, w_in_a, lb_logits,
           gnorm_a, w_out_a, w_in_b, ln_v_g, ln_v_b, w_spatial, b_spatial, w_out_b,
           norm_ple, w_ple_gate, w_ple_proj, norm_final):
    depth = norm_mix.shape[0]
    n_s = x_sample.shape[0]
    assert x_prompt.shape[1] % TLA == 0 and TLA % TL == 0 and x_prompt.shape[1] % TLB == 0
    assert TLB % B_CHUNK == 0 and x_sample.shape[1] == 1

    def layer_weights(i):
        mixer = (w_in_a, w_out_a) if i % 2 == 0 else (w_in_b, w_out_b)
        return ((mixer[0], i // 2), (mixer[1], i // 2), (w_ple_gate, i), (w_ple_proj, i))

    wb = tuple(stack[idx].astype(BF16) for stack, idx in layer_weights(0))

    tri = jnp.asarray(_TRI_NP, dtype=BF16)
    lvl = jnp.asarray(_LVL_NP)
    nf = norm_final.reshape(1, D_MODEL)
    bias_full = jnp.repeat(jnp.swapaxes(b_spatial, 1, 2), DG, axis=2)
    w00 = jnp.repeat(w_spatial[:, :, 0, 0], DG, axis=1)[:, None, :]
    b0 = jnp.repeat(b_spatial[:, :, 0], DG, axis=1)[:, None, :]
    nmix = norm_mix[:, None, :]
    nple = norm_ple[:, None, :]
    gn = gnorm_a[:, None, :]
    lng = ln_v_g[:, None, :]
    lnb = ln_v_b[:, None, :]

    if depth % 2 == 1:
        raise NotImplementedError("final norm is fused into the last (chunk-MLP) layer")
    hp = x_prompt
    hs = x_sample.reshape(n_s, D_MODEL)
    ps = p_sample.reshape(depth, n_s, PLE_DIM)
    vt_s = []
    st_p = st_s = vt_p = None
    for i in range(depth):
        j = i // 2
        last = i == depth - 1
        nxt = None if last else layer_weights(i + 1)
        w_in, w_out, w_gate, w_proj = wb
        if i % 2 == 0:
            qfkv, z_s = _hgrn_sample_front(hs, i, j, lb_logits, nmix, w_in)
            hp, st_p, st_s, o_s, *wb = _hgrn_prompt_layer(
                hp, p_prompt, i, j, lb_logits, nmix, w_in, gn, w_out, nple, w_gate, w_proj,
                tri, lvl, qfkv, state_hgrn, st_s, st_p, nxt)
            hs = _hgrn_sample_back(hs, o_s, z_s, ps, i, j, gn, w_out, nple, w_gate, w_proj)
        else:
            hp, vt_p, hs, vts, *wb = _gmlp_prompt_layer(
                hp, p_prompt, i, j, nmix, w_in, lng, lnb, w_spatial, bias_full, w_out, nple,
                w_gate, w_proj, nf, last, w_in_b.shape[0], vt_p, nxt, (hs, ps, w00, b0))
            vt_s.append(vts.reshape(n_s, 1, D_INNER))
    return (hp, hs.reshape(n_s, 1, D_MODEL), st_p, st_s, vt_p, jnp.stack(vt_s))
```
